```python
import jax, jax.numpy as jnp
from jax import lax
import numpy as np

D_MODEL = 2048
BATCH = 2
SEQ = 16384
DEPTH = 1
DEC_BATCH = 8
DEC_SEQ = 32
PAST_LEN = 4096

CHUNK = 64
D_PLE = 256
RMS_EPS = 1e-6
GLA_HEADS = 4
GLA_DK = D_MODEL // 16
GLA_DV = D_MODEL // 8
GLA_KW = GLA_HEADS * GLA_DK
GLA_WIDTH = GLA_HEADS * GLA_DV
GLA_RANK = 16
GLA_TAU = 16.0
SWA_HEADS = 16
SWA_KV_HEADS = 4
SWA_HEAD_DIM = 64
SWA_GROUP = SWA_HEADS // SWA_KV_HEADS
SWA_WIDTH = SWA_HEADS * SWA_HEAD_DIM
SWA_KVW = SWA_KV_HEADS * SWA_HEAD_DIM
WINDOW = 128
MIX_WIDTH = GLA_WIDTH + SWA_WIDTH
IN_SIZES = (GLA_KW, GLA_KW, GLA_WIDTH, GLA_WIDTH, GLA_RANK, SWA_WIDTH, SWA_KVW, SWA_KVW, SWA_WIDTH)
IN_DIM = GLA_KW * 2 + GLA_WIDTH * 2 + GLA_RANK + SWA_WIDTH * 2 + SWA_KVW * 2

kernel_name = "hymba_gla_swa_sink_stream_step"


def rmsnorm(x, g):
    xf = x.astype(jnp.float32)
    y = xf * lax.rsqrt(jnp.mean(xf * xf, axis=-1, keepdims=True) + RMS_EPS)
    return (y * g.astype(jnp.float32)).astype(x.dtype)


def split_columns(u):
    outs, off = [], 0
    for size in IN_SIZES:
        outs.append(u[..., off:off + size])
        off += size
    return outs


def gla_chunk(s, xs):
    q, k, v, g = xs
    c = q.shape[-2]
    b = jnp.cumsum(g, axis=-2)
    b_last = b[..., -1:, :]
    qe = q * jnp.exp(b)
    ke = k * jnp.exp(-b)
    causal = jnp.tril(jnp.ones((c, c), dtype=bool))
    a = jnp.where(causal, jnp.einsum('bhik,bhjk->bhij', qe, ke), 0.0)
    o = jnp.einsum('bhij,bhjv->bhiv', a, v) + jnp.einsum('bhik,bhkv->bhiv', qe, s)
    s_new = jnp.exp(b_last[..., 0, :])[..., None] * s + jnp.einsum('bhjk,bhjv->bhkv', k * jnp.exp(b_last - b), v)
    return s_new, o


def gla_forward(s0, q, k, v, g):
    bsz, hh, length, _ = q.shape
    c = min(CHUNK, length)
    nc = length // c
    def to_chunks(t):
        return jnp.moveaxis(t.reshape(bsz, hh, nc, c, t.shape[-1]), 2, 0)
    s, o = lax.scan(gla_chunk, s0, (to_chunks(q), to_chunks(k), to_chunks(v), to_chunks(g)))
    o = jnp.moveaxis(o, 0, 2).reshape(bsz, hh, length, v.shape[-1])
    return s, o


def sink_attention(q, k, v, sinks, mask):
    s = jnp.einsum('...qhgd,...khd->...hgqk', q, k).astype(jnp.float32) * (SWA_HEAD_DIM ** -0.5)
    if mask is not None:
        s = jnp.where(mask, s, jnp.finfo(jnp.float32).min)
    sink = sinks.astype(jnp.float32)[..., None, None]
    m = jnp.maximum(jnp.max(s, axis=-1, keepdims=True), sink)
    pr = jnp.exp(s - m)
    w = pr / (jnp.sum(pr, axis=-1, keepdims=True) + jnp.exp(sink - m))
    return jnp.einsum('...hgqk,...khd->...qhgd', w.astype(v.dtype), v)


def swa_prompt(q, k, v, sinks):
    bsz, length = q.shape[0], q.shape[1]
    nc = length // CHUNK
    nb = WINDOW // CHUNK
    pad = ((0, 0), (WINDOW, 0), (0, 0), (0, 0))
    kp = jnp.pad(k, pad).reshape(bsz, nc + nb, CHUNK, SWA_KV_HEADS, SWA_HEAD_DIM)
    vp = jnp.pad(v, pad).reshape(bsz, nc + nb, CHUNK, SWA_KV_HEADS, SWA_HEAD_DIM)
    kb = jnp.concatenate([kp[:, i:i + nc] for i in range(nb + 1)], axis=2)
    vb = jnp.concatenate([vp[:, i:i + nc] for i in range(nb + 1)], axis=2)
    qb = q.reshape(bsz, nc, CHUNK, SWA_KV_HEADS, SWA_GROUP, SWA_HEAD_DIM)
    key_pos = jnp.arange(nc)[:, None] * CHUNK - WINDOW + jnp.arange(WINDOW + CHUNK)[None, :]
    mask = (key_pos >= 0)[:, None, None, None, :]
    o = sink_attention(qb, kb, vb, sinks, mask)
    return o.reshape(bsz, length, SWA_WIDTH)


def swa_sample(q, k, v, k_past, v_past, sinks):
    bsz, length = q.shape[0], q.shape[1]
    kk = jnp.concatenate([k_past.astype(k.dtype), k], axis=1)
    vv = jnp.concatenate([v_past.astype(v.dtype), v], axis=1)
    o = sink_attention(q, kk, vv, sinks, None)
    return o.reshape(bsz, length, SWA_WIDTH)


def trunk_layer(h, p_l, gla_s0, k_past, v_past, g_mix, w_in, w_gate_up, b_gate, g_gla_out,
                swa_sinks, w_out, g_ple, w_ple_gate, w_ple_proj):
    bsz, length, _ = h.shape
    n = rmsnorm(h, g_mix)
    u = n @ w_in
    q1, k1, v1, z1, lr, q2, k2, v2, z2 = split_columns(u)
    def heads(t, d):
        return t.reshape(bsz, length, GLA_HEADS, d).transpose(0, 2, 1, 3).astype(jnp.float32)
    g = jax.nn.log_sigmoid((lr @ w_gate_up + b_gate).astype(jnp.float32)) / GLA_TAU
    gla_s, o1 = gla_forward(gla_s0.astype(jnp.float32), heads(q1, GLA_DK) * (GLA_DK ** -0.5),
                            heads(k1, GLA_DK), heads(v1, GLA_DV), heads(g, GLA_DK))
    o1 = rmsnorm(o1.transpose(0, 2, 1, 3), g_gla_out).astype(h.dtype)
    o1 = o1.reshape(bsz, length, GLA_WIDTH) * jax.nn.silu(z1)
    qh = q2.reshape(bsz, length, SWA_KV_HEADS, SWA_GROUP, SWA_HEAD_DIM)
    kh = k2.reshape(bsz, length, SWA_KV_HEADS, SWA_HEAD_DIM)
    vh = v2.reshape(bsz, length, SWA_KV_HEADS, SWA_HEAD_DIM)
    if k_past is None:
        o2 = swa_prompt(qh, kh, vh, swa_sinks)
        keep = min(WINDOW, length)
        k_keep, v_keep = kh[:, length - keep:], vh[:, length - keep:]
    else:
        o2 = swa_sample(qh, kh, vh, k_past, v_past, swa_sinks)
        k_keep, v_keep = kh, vh
    o2 = o2 * jax.nn.silu(z2)
    h = h + jnp.concatenate([o1, o2], axis=-1) @ w_out
    gate = jax.nn.sigmoid(rmsnorm(h, g_ple) @ w_ple_gate)
    h = h + gate * (p_l @ w_ple_proj)
    return h, gla_s, k_keep, v_keep


def setup_inputs(seed: int = 0) -> dict:
    key = jax.random.key(seed)
    ks = jax.random.split(key, 20)
    f32 = jnp.float32
    cache_len = min(WINDOW, PAST_LEN)
    nrm = lambda k, shape, scale: jax.random.normal(k, shape, f32) * scale
    return {
        "x_prompt": nrm(ks[0], (BATCH, SEQ, D_MODEL), 1.0),
        "x_sample": nrm(ks[1], (DEC_BATCH, DEC_SEQ, D_MODEL), 1.0),
        "p_prompt": nrm(ks[2], (DEPTH, BATCH, SEQ, D_PLE), 1.0),
        "p_sample": nrm(ks[3], (DEPTH, DEC_BATCH, DEC_SEQ, D_PLE), 1.0),
        "state_gla": nrm(ks[4], (DEPTH, DEC_BATCH, GLA_HEADS, GLA_DK, GLA_DV), 1.0),
        "cache_swa_k": nrm(ks[5], (DEPTH, DEC_BATCH, cache_len, SWA_KV_HEADS, SWA_HEAD_DIM), 1.0),
        "cache_swa_v": nrm(ks[6], (DEPTH, DEC_BATCH, cache_len, SWA_KV_HEADS, SWA_HEAD_DIM), 1.0),
        "g_mix": 1.0 + nrm(ks[7], (DEPTH, D_MODEL), 0.01),
        "w_in": nrm(ks[8], (DEPTH, D_MODEL, IN_DIM), D_MODEL ** -0.5),
        "w_gate_up": nrm(ks[9], (DEPTH, GLA_RANK, GLA_KW), GLA_RANK ** -0.5),
        "b_gate": nrm(ks[10], (DEPTH, GLA_KW), 0.01),
        "g_gla_out": 1.0 + nrm(ks[11], (DEPTH, GLA_DV), 0.01),
        "swa_sinks": nrm(ks[12], (DEPTH, SWA_KV_HEADS, SWA_GROUP), 0.5),
        "w_out": nrm(ks[13], (DEPTH, MIX_WIDTH, D_MODEL), MIX_WIDTH ** -0.5),
        "g_ple": 1.0 + nrm(ks[14], (DEPTH, D_MODEL), 0.01),
        "w_ple_gate": nrm(ks[15], (DEPTH, D_MODEL, D_MODEL), D_MODEL ** -0.5),
        "w_ple_proj": nrm(ks[16], (DEPTH, D_PLE, D_MODEL), D_PLE ** -0.5),
        "g_final": 1.0 + nrm(ks[17], (D_MODEL,), 0.01),
    }


def reference(x_prompt, x_sample, p_prompt, p_sample, state_gla, cache_swa_k, cache_swa_v,
              g_mix, w_in, w_gate_up, b_gate, g_gla_out, swa_sinks, w_out, g_ple, w_ple_gate,
              w_ple_proj, g_final):
    hp, hs = x_prompt, x_sample
    bp = x_prompt.shape[0]
    sp_list, kp_list, vp_list, ss_list, ks_list, vs_list = [], [], [], [], [], []
    for i in range(DEPTH):
        wts = (g_mix[i], w_in[i], w_gate_up[i], b_gate[i], g_gla_out[i], swa_sinks[i], w_out[i],
               g_ple[i], w_ple_gate[i], w_ple_proj[i])
        s0 = jnp.zeros((bp, GLA_HEADS, GLA_DK, GLA_DV), jnp.float32)
        hp, s_p, k_p, v_p = trunk_layer(hp, p_prompt[i], s0, None, None, *wts)
        hs, s_s, k_s, v_s = trunk_layer(hs, p_sample[i], state_gla[i], cache_swa_k[i], cache_swa_v[i], *wts)
        sp_list.append(s_p); kp_list.append(k_p); vp_list.append(v_p)
        ss_list.append(s_s); ks_list.append(k_s); vs_list.append(v_s)
    y_prompt = rmsnorm(hp, g_final)
    y_sample = rmsnorm(hs, g_final)
    return (y_prompt, y_sample, jnp.stack(sp_list), jnp.stack(kp_list), jnp.stack(vp_list),
            jnp.stack(ss_list), jnp.stack(ks_list), jnp.stack(vs_list))
```

```python
import functools

import jax
import jax.numpy as jnp
from jax import lax
from jax.experimental import pallas as pl
from jax.experimental.pallas import tpu as pltpu

F32 = jnp.float32
BF16 = jnp.bfloat16

CHUNK = 64
RMS_EPS = 1e-6
GLA_HEADS = 4
GLA_DK = 128
GLA_DV = 256
GLA_KW = GLA_HEADS * GLA_DK
GLA_WIDTH = GLA_HEADS * GLA_DV
GLA_RANK = 16
GLA_TAU = 16.0
SWA_KV_HEADS = 4
SWA_GROUP = 4
SWA_HEAD_DIM = 64
SWA_WIDTH = SWA_KV_HEADS * SWA_GROUP * SWA_HEAD_DIM
SWA_KVW = SWA_KV_HEADS * SWA_HEAD_DIM
WINDOW = 128

LANES = 128
VMEM_LIMIT_BYTES = 56 * 1024 * 1024

NT_DIMS = (((1,), (1,)), ((), ()))
TN_DIMS = (((0,), (0,)), ((), ()))


def _dot(a, b):
    return jnp.dot(a, b, preferred_element_type=F32)


def _dot_nt(a, b):
    return lax.dot_general(a, b, NT_DIMS, preferred_element_type=F32)


def _dot_tn(a, b):
    return lax.dot_general(a, b, TN_DIMS, preferred_element_type=F32)


def _rms_scale(x):
    return lax.rsqrt(jnp.mean(x * x, axis=-1, keepdims=True) + RMS_EPS)


def _sigmoid(x):
    return 1.0 / (1.0 + jnp.exp(-x))


def _resident(shape):
    zeros = (0,) * len(shape)
    return pl.BlockSpec(shape, lambda *_: zeros, pipeline_mode=pl.Buffered(1))


def _inproj_kernel(x_ref, g_ref, *refs):
    n_out = len(refs) // 2
    w_refs, o_refs = refs[:n_out], refs[n_out:]
    x = x_ref[...]
    n = ((x * _rms_scale(x)) * g_ref[...]).astype(BF16)
    for w_ref, o_ref in zip(w_refs, o_refs):
        o_ref[...] = _dot(n, w_ref[...]).astype(o_ref.dtype)


def _inproj(x, g_mix, weights, out_dtypes, tm):
    t, d = x.shape
    outs = tuple(jax.ShapeDtypeStruct((t, w.shape[1]), dt) for w, dt in zip(weights, out_dtypes))
    return pl.pallas_call(
        _inproj_kernel,
        grid=(t // tm,),
        in_specs=[pl.BlockSpec((tm, d), lambda i: (i, 0)), _resident((1, d))]
        + [_resident(w.shape) for w in weights],
        out_specs=tuple(pl.BlockSpec((tm, w.shape[1]), lambda i: (i, 0)) for w in weights),
        out_shape=outs,
        compiler_params=pltpu.CompilerParams(
            dimension_semantics=("arbitrary",), vmem_limit_bytes=VMEM_LIMIT_BYTES),
        name="inproj",
    )(x, g_mix, *weights)


def _gla_kernel(q_ref, k_ref, v_ref, z_ref, lr_ref, wg_ref, bg_ref, gout_ref, s0_ref,
                o_ref, s_ref, *, chunk, n_chunks):
    @pl.when(pl.program_id(1) == 0)
    def _():
        s_ref[...] = s0_ref[...]

    row = lax.broadcasted_iota(jnp.int32, (chunk, chunk), 0)
    col = lax.broadcasted_iota(jnp.int32, (chunk, chunk), 1)
    causal = row >= col
    tril = jnp.where(causal, 1.0, 0.0).astype(BF16)
    eye = (lax.broadcasted_iota(jnp.int32, (GLA_DK, GLA_DK), 0)
           == lax.broadcasted_iota(jnp.int32, (GLA_DK, GLA_DK), 1))
    scale = GLA_DK ** -0.5

    def chunk_body(c, carry):
        r = pl.ds(pl.multiple_of(c * chunk, chunk), chunk)
        gpre = _dot(lr_ref[r, :], wg_ref[...]) + bg_ref[...]
        g = (jnp.minimum(gpre, 0.0) - jnp.log1p(jnp.exp(-jnp.abs(gpre)))) * (1.0 / GLA_TAU)
        g_hi = g.astype(BF16)
        rem = g - g_hi.astype(F32)
        g_mid = rem.astype(BF16)
        g_lo = (rem - g_mid.astype(F32)).astype(BF16)
        b = _dot(tril, g_hi) + _dot(tril, g_mid) + _dot(tril, g_lo)
        b_last = b[chunk - 1:chunk, :]
        q = q_ref[r, :].astype(F32)
        k = k_ref[r, :].astype(F32)
        qe = ((q * scale) * jnp.exp(b)).astype(BF16)
        ke = (k * jnp.exp(-b)).astype(BF16)
        kd = (k * jnp.exp(b_last - b)).astype(BF16)
        decay = jnp.exp(b_last)
        for h in range(GLA_HEADS):
            ks = slice(h * GLA_DK, (h + 1) * GLA_DK)
            vs = slice(h * GLA_DV, (h + 1) * GLA_DV)
            a = jnp.where(causal, _dot_nt(qe[:, ks], ke[:, ks]), 0.0).astype(BF16)
            v = v_ref[r, vs]
            s = s_ref[h]
            o = _dot(a, v) + _dot(qe[:, ks], s.astype(BF16))
            dcol = jnp.sum(jnp.where(eye, decay[:, ks], 0.0), axis=1, keepdims=True)
            s_ref[h] = dcol * s + _dot_tn(kd[:, ks], v)
            o = (o * _rms_scale(o)) * gout_ref[...]
            z = z_ref[r, vs].astype(F32)
            o_ref[r, vs] = (o * (z * _sigmoid(z))).astype(o_ref.dtype)
        return carry

    lax.fori_loop(0, n_chunks, chunk_body, 0)


def _gla(q, k, v, z, lr, wg, bg, gout, s0, chunk, tb):
    bsz, length, _ = q.shape
    n_chunks = tb // chunk
    blk = lambda w: pl.BlockSpec((None, tb, w), lambda b, i: (b, i, 0))
    st = pl.BlockSpec((None, GLA_HEADS, GLA_DK, GLA_DV), lambda b, i: (b, 0, 0, 0))
    return pl.pallas_call(
        functools.partial(_gla_kernel, chunk=chunk, n_chunks=n_chunks),
        grid=(bsz, length // tb),
        in_specs=[blk(GLA_KW), blk(GLA_KW), blk(GLA_WIDTH), blk(GLA_WIDTH), blk(LANES),
                  _resident(wg.shape), _resident(bg.shape), _resident(gout.shape), st],
        out_specs=(blk(GLA_WIDTH), st),
        out_shape=(jax.ShapeDtypeStruct((bsz, length, GLA_WIDTH), BF16),
                   jax.ShapeDtypeStruct(s0.shape, F32)),
        compiler_params=pltpu.CompilerParams(
            dimension_semantics=("arbitrary", "arbitrary"), vmem_limit_bytes=VMEM_LIMIT_BYTES),
        name="gla",
    )(q, k, v, z, lr, wg, bg, gout, s0)


def _swa_kernel(sink_ref, q_ref, kp_ref, vp_ref, kc_ref, vc_ref, z_ref, o_ref,
                *, chunk, n_chunks, mask_missing_past):
    kvw = SWA_GROUP * SWA_HEAD_DIM
    n_keys = WINDOW + chunk
    kcat = jnp.concatenate([kp_ref[...], kc_ref[...]], axis=0).astype(BF16)
    vcat = jnp.concatenate([vp_ref[...], vc_ref[...]], axis=0).astype(BF16)
    lane = lax.broadcasted_iota(jnp.int32, (chunk, kvw), 1)
    key_j = lax.broadcasted_iota(jnp.int32, (SWA_GROUP * chunk, n_keys), 1)
    qscale = SWA_HEAD_DIM ** -0.5
    first_pos = pl.program_id(1) * (chunk * n_chunks) - WINDOW

    for h in range(SWA_KV_HEADS):
        hd = slice(h * SWA_HEAD_DIM, (h + 1) * SWA_HEAD_DIM)
        k_rep = jnp.concatenate([kcat[:, hd]] * SWA_GROUP, axis=1)
        v_rep = jnp.concatenate([vcat[:, hd]] * SWA_GROUP, axis=1)
        qs_cols = slice(h * kvw, (h + 1) * kvw)
        sink = jnp.concatenate(
            [jnp.full((chunk, 1), sink_ref[h * SWA_GROUP + g], F32) for g in range(SWA_GROUP)],
            axis=0)
        for c in range(n_chunks):
            rows = slice(c * chunk, (c + 1) * chunk)
            keys = slice(c * chunk, c * chunk + n_keys)
            q = q_ref[rows, qs_cols]
            qs = jnp.concatenate(
                [jnp.where((lane >= g * SWA_HEAD_DIM) & (lane < (g + 1) * SWA_HEAD_DIM),
                           q, jnp.zeros_like(q)) for g in range(SWA_GROUP)], axis=0)
            s = _dot_nt(qs, k_rep[keys]) * qscale
            if mask_missing_past:
                s = jnp.where(key_j + (first_pos + c * chunk) >= 0, s, jnp.finfo(F32).min)
            m = jnp.maximum(jnp.max(s, axis=-1, keepdims=True), sink)
            p = jnp.exp(s - m)
            denom = jnp.sum(p, axis=-1, keepdims=True) + jnp.exp(sink - m)
            pv = _dot(p.astype(BF16), v_rep[keys]) / denom
            o = jnp.zeros((chunk, kvw), F32)
            for g in range(SWA_GROUP):
                in_g = (lane >= g * SWA_HEAD_DIM) & (lane < (g + 1) * SWA_HEAD_DIM)
                o = jnp.where(in_g, pv[g * chunk:(g + 1) * chunk], o)
            z = z_ref[rows, qs_cols].astype(F32)
            o_ref[rows, qs_cols] = (o * (z * _sigmoid(z))).astype(o_ref.dtype)


def _swa(sinks, q, z, k_past, v_past, k_cur, v_cur, chunk, qb, mask_missing_past):
    bsz, length, _ = q.shape
    n_chunks = qb // chunk
    blk = lambda w: pl.BlockSpec((None, qb, w), lambda b, i: (b, i, 0))
    kp_arr, kp_map = k_past
    vp_arr, vp_map = v_past
    kc_arr, kc_col = k_cur
    vc_arr, vc_col = v_cur
    return pl.pallas_call(
        functools.partial(_swa_kernel, chunk=chunk, n_chunks=n_chunks,
                          mask_missing_past=mask_missing_past),
        grid=(bsz, length // qb),
        in_specs=[pl.BlockSpec(memory_space=pltpu.SMEM),
                  blk(SWA_WIDTH),
                  pl.BlockSpec((None, WINDOW, SWA_KVW), kp_map),
                  pl.BlockSpec((None, WINDOW, SWA_KVW), vp_map),
                  pl.BlockSpec((None, qb, SWA_KVW), lambda b, i: (b, i, kc_col)),
                  pl.BlockSpec((None, qb, SWA_KVW), lambda b, i: (b, i, vc_col)),
                  blk(SWA_WIDTH)],
        out_specs=blk(SWA_WIDTH),
        out_shape=jax.ShapeDtypeStruct((bsz, length, SWA_WIDTH), BF16),
        compiler_params=pltpu.CompilerParams(
            dimension_semantics=("arbitrary", "arbitrary"), vmem_limit_bytes=VMEM_LIMIT_BYTES),
        name="swa",
    )(sinks, q, kp_arr, vp_arr, kc_arr, vc_arr, z)


def _outproj_kernel(x_ref, o1_ref, o2_ref, p_ref, wo1_ref, wo2_ref, gple_ref, wg_ref, wp_ref,
                    gfin_ref, y_ref):
    h = x_ref[...] + (_dot(o1_ref[...], wo1_ref[...]) + _dot(o2_ref[...], wo2_ref[...]))
    n = ((h * _rms_scale(h)) * gple_ref[...]).astype(BF16)
    gate = _sigmoid(_dot(n, wg_ref[...]))
    h = h + gate * _dot(p_ref[...].astype(BF16), wp_ref[...])
    y_ref[...] = (h * _rms_scale(h)) * gfin_ref[...]


def _outproj(x, o1, o2, p, wo1, wo2, gple, wg, wp, gfin, tm):
    t, d = x.shape
    row = lambda w: pl.BlockSpec((tm, w), lambda i: (i, 0))
    return pl.pallas_call(
        _outproj_kernel,
        grid=(t // tm,),
        in_specs=[row(d), row(o1.shape[1]), row(o2.shape[1]), row(p.shape[1]),
                  _resident(wo1.shape), _resident(wo2.shape), _resident(gple.shape),
                  _resident(wg.shape), _resident(wp.shape), _resident(gfin.shape)],
        out_specs=row(d),
        out_shape=jax.ShapeDtypeStruct((t, d), F32),
        compiler_params=pltpu.CompilerParams(
            dimension_semantics=("arbitrary",), vmem_limit_bytes=VMEM_LIMIT_BYTES),
        name="outproj",
    )(x, o1, o2, p, wo1, wo2, gple, wg, wp, gfin)


def _split_w_in(w_in):
    sizes = (GLA_KW, GLA_KW, GLA_WIDTH, GLA_WIDTH, GLA_RANK, SWA_WIDTH, SWA_KVW, SWA_KVW, SWA_WIDTH)
    offs = [0]
    for s in sizes:
        offs.append(offs[-1] + s)
    col = lambda a, b: w_in[:, offs[a]:offs[b]].astype(BF16)
    q1, k1, v1, z1, lr, q2, kv2, z2 = col(0, 1), col(1, 2), col(2, 3), col(3, 4), col(4, 5), \
        col(5, 6), col(6, 8), col(8, 9)
    lr = jnp.pad(lr, ((0, 0), (0, LANES - GLA_RANK)))
    return q1, k1, v1, z1, q2, kv2, z2, lr


def _layer(x, p, s0, past, wts, chunk, tm_in, tm_out, tb_gla, qb_swa):
    (g_mix, w_in_groups, w_gate, b_gate, g_gla_out, sinks, wo1, wo2, g_ple, w_ple_gate,
     w_ple_proj, g_final) = wts
    bsz, length, d = x.shape
    x2 = x.reshape(bsz * length, d)
    q1, k1, v1, z1, q2, kv2, z2, lr = _inproj(
        x2, g_mix, w_in_groups, (BF16, BF16, BF16, BF16, BF16, F32, BF16, BF16), tm_in)
    b3 = lambda a: a.reshape(bsz, length, a.shape[-1])
    o1, s_new = _gla(b3(q1), b3(k1), b3(v1), b3(z1), b3(lr), w_gate, b_gate, g_gla_out, s0,
                     chunk, tb_gla)
    kv3 = b3(kv2)
    if past is None:
        per_win = qb_swa // WINDOW
        prev = lambda col: (lambda b, i: (b, jnp.maximum(i * per_win - 1, 0), col))
        k_past, v_past = (kv3, prev(0)), (kv3, prev(1))
    else:
        k_past = (past[0], lambda b, i: (b, 0, 0))
        v_past = (past[1], lambda b, i: (b, 0, 0))
    o2 = _swa(sinks, b3(q2), b3(z2), k_past, v_past, (kv3, 0), (kv3, 1), chunk, qb_swa,
              mask_missing_past=past is None)
    y = _outproj(x2, o1.reshape(bsz * length, -1), o2.reshape(bsz * length, -1),
                 p.reshape(bsz * length, -1), wo1, wo2, g_ple, w_ple_gate, w_ple_proj, g_final,
                 tm_out)
    return y.reshape(bsz, length, d), s_new, kv3


def kernel(x_prompt, x_sample, p_prompt, p_sample, state_gla, cache_swa_k, cache_swa_v, g_mix, w_in, w_gate_up, b_gate, g_gla_out, swa_sinks, w_out, g_ple, w_ple_gate, w_ple_proj, g_final):
    depth = w_in.shape[0]
    assert depth == 1, "the final RMSNorm is fused into the last layer's output kernel"
    bp, lp, _ = x_prompt.shape
    bs, ls, _ = x_sample.shape
    i = 0
    row = lambda a: a.reshape(1, -1).astype(F32)
    wts = (row(g_mix[i]), _split_w_in(w_in[i]),
           jnp.pad(w_gate_up[i].astype(BF16), ((0, LANES - GLA_RANK), (0, 0))), row(b_gate[i]),
           row(g_gla_out[i]), swa_sinks[i].reshape(-1).astype(F32),
           w_out[i, :GLA_WIDTH].astype(BF16), w_out[i, GLA_WIDTH:].astype(BF16),
           row(g_ple[i]), w_ple_gate[i].astype(BF16), w_ple_proj[i].astype(BF16), row(g_final))

    s0_p = jnp.zeros((bp, GLA_HEADS, GLA_DK, GLA_DV), F32)
    y_p, s_p, kv_p = _layer(x_prompt, p_prompt[i], s0_p, None, wts, chunk=CHUNK,
                            tm_in=512, tm_out=256, tb_gla=512, qb_swa=256)
    keep = min(WINDOW, lp)
    k_p = kv_p[:, lp - keep:, :SWA_KVW].reshape(bp, keep, SWA_KV_HEADS, SWA_HEAD_DIM)
    v_p = kv_p[:, lp - keep:, SWA_KVW:].reshape(bp, keep, SWA_KV_HEADS, SWA_HEAD_DIM)

    past = (cache_swa_k[i].reshape(bs, -1, SWA_KVW), cache_swa_v[i].reshape(bs, -1, SWA_KVW))
    y_s, s_s, kv_s = _layer(x_sample, p_sample[i], state_gla[i], past, wts, chunk=min(CHUNK, ls),
                            tm_in=bs * ls, tm_out=bs * ls, tb_gla=ls, qb_swa=ls)
    k_s = kv_s[:, :, :SWA_KVW].reshape(bs, ls, SWA_KV_HEADS, SWA_HEAD_DIM)
    v_s = kv_s[:, :, SWA_KVW:].reshape(bs, ls, SWA_KV_HEADS, SWA_HEAD_DIM)
    return (y_p, y_s, s_p[None], k_p[None], v_p[None], s_s[None], k_s[None], v_s[None])
```

```python
import functools

import jax
import jax.numpy as jnp
from jax import lax
from jax.experimental import pallas as pl
from jax.experimental.pallas import tpu as pltpu

F32 = jnp.float32
BF16 = jnp.bfloat16

CHUNK = 64
RMS_EPS = 1e-6
GLA_HEADS = 4
GLA_DK = 128
GLA_DV = 256
GLA_KW = GLA_HEADS * GLA_DK
GLA_WIDTH = GLA_HEADS * GLA_DV
GLA_RANK = 16
GLA_TAU = 16.0
SWA_KV_HEADS = 4
SWA_GROUP = 4
SWA_HEAD_DIM = 64
SWA_WIDTH = SWA_KV_HEADS * SWA_GROUP * SWA_HEAD_DIM
SWA_KVW = SWA_KV_HEADS * SWA_HEAD_DIM
WINDOW = 128

LANES = 128
VMEM_LIMIT_BYTES = 56 * 1024 * 1024

NT_DIMS = (((1,), (1,)), ((), ()))
TN_DIMS = (((0,), (0,)), ((), ()))


def _dot(a, b):
    return jnp.dot(a, b, preferred_element_type=F32)


def _dot_nt(a, b):
    return lax.dot_general(a, b, NT_DIMS, preferred_element_type=F32)


def _dot_tn(a, b):
    return lax.dot_general(a, b, TN_DIMS, preferred_element_type=F32)


def _rms_scale(x):
    return lax.rsqrt(jnp.mean(x * x, axis=-1, keepdims=True) + RMS_EPS)


def _sigmoid(x):
    return 1.0 / (1.0 + jnp.exp(-x))


def _resident(shape):
    zeros = (0,) * len(shape)
    return pl.BlockSpec(shape, lambda *_: zeros, pipeline_mode=pl.Buffered(1))


def _inproj_kernel(x_ref, g_ref, *refs):
    n_out = len(refs) // 2
    w_refs, o_refs = refs[:n_out], refs[n_out:]
    x = x_ref[...]
    n = ((x * _rms_scale(x)) * g_ref[...]).astype(BF16)
    for w_ref, o_ref in zip(w_refs, o_refs):
        o_ref[...] = _dot(n, w_ref[...]).astype(o_ref.dtype)


def _inproj(x, g_mix, weights, out_dtypes, tm):
    t, d = x.shape
    outs = tuple(jax.ShapeDtypeStruct((t, w.shape[1]), dt) for w, dt in zip(weights, out_dtypes))
    return pl.pallas_call(
        _inproj_kernel,
        grid=(t // tm,),
        in_specs=[pl.BlockSpec((tm, d), lambda i: (i, 0)), _resident((1, d))]
        + [_resident(w.shape) for w in weights],
        out_specs=tuple(pl.BlockSpec((tm, w.shape[1]), lambda i: (i, 0)) for w in weights),
        out_shape=outs,
        compiler_params=pltpu.CompilerParams(
            dimension_semantics=("arbitrary",), vmem_limit_bytes=VMEM_LIMIT_BYTES),
        name="inproj",
    )(x, g_mix, *weights)


def _gla_kernel(q_ref, k_ref, v_ref, z_ref, lr_ref, wg_ref, bg_ref, gout_ref, s0_ref,
                o_ref, s_ref, *, chunk, n_chunks):
    @pl.when(pl.program_id(1) == 0)
    def _():
        s_ref[...] = s0_ref[...]

    row = lax.broadcasted_iota(jnp.int32, (chunk, chunk), 0)
    col = lax.broadcasted_iota(jnp.int32, (chunk, chunk), 1)
    causal = row >= col
    row3 = lax.broadcasted_iota(jnp.int32, (chunk, 3 * chunk), 0)
    col3 = lax.broadcasted_iota(jnp.int32, (chunk, 3 * chunk), 1)
    tril3 = jnp.where(row3 >= lax.rem(col3, chunk), 1.0, 0.0).astype(BF16)
    eye = (lax.broadcasted_iota(jnp.int32, (GLA_DK, GLA_DK), 0)
           == lax.broadcasted_iota(jnp.int32, (GLA_DK, GLA_DK), 1))
    scale = GLA_DK ** -0.5

    for c in range(n_chunks):
        r = slice(c * chunk, (c + 1) * chunk)
        gpre = _dot(lr_ref[r, :], wg_ref[...]) + bg_ref[...]
        g = (jnp.minimum(gpre, 0.0) - jnp.log1p(jnp.exp(-jnp.abs(gpre)))) * (1.0 / GLA_TAU)
        g_hi = g.astype(BF16)
        rem = g - g_hi.astype(F32)
        g_mid = rem.astype(BF16)
        g_lo = (rem - g_mid.astype(F32)).astype(BF16)
        b = _dot(tril3, jnp.concatenate([g_hi, g_mid, g_lo], axis=0))
        b_last = b[chunk - 1:chunk, :]
        q = q_ref[r, :].astype(F32)
        k = k_ref[r, :].astype(F32)
        qe = ((q * scale) * jnp.exp(b)).astype(BF16)
        ke = (k * jnp.exp(-b)).astype(BF16)
        kd = (k * jnp.exp(b_last - b)).astype(BF16)
        decay = jnp.exp(b_last)
        for h in range(GLA_HEADS):
            ks = slice(h * GLA_DK, (h + 1) * GLA_DK)
            vs = slice(h * GLA_DV, (h + 1) * GLA_DV)
            a = jnp.where(causal, _dot_nt(qe[:, ks], ke[:, ks]), 0.0).astype(BF16)
            v = v_ref[r, vs]
            s = s_ref[h]
            o = _dot(jnp.concatenate([qe[:, ks], a], axis=1),
                     jnp.concatenate([s.astype(BF16), v], axis=0))
            dcol = jnp.sum(jnp.where(eye, decay[:, ks], 0.0), axis=1, keepdims=True)
            s_ref[h] = dcol * s + _dot_tn(kd[:, ks], v)
            o = (o * _rms_scale(o)) * gout_ref[...]
            z = z_ref[r, vs].astype(F32)
            o_ref[r, vs] = (o * (z * _sigmoid(z))).astype(o_ref.dtype)


def _gla(q, k, v, z, lr, wg, bg, gout, s0, chunk, tb):
    bsz, length, _ = q.shape
    n_chunks = tb // chunk
    blk = lambda w: pl.BlockSpec((None, tb, w), lambda b, i: (b, i, 0))
    st = pl.BlockSpec((None, GLA_HEADS, GLA_DK, GLA_DV), lambda b, i: (b, 0, 0, 0))
    return pl.pallas_call(
        functools.partial(_gla_kernel, chunk=chunk, n_chunks=n_chunks),
        grid=(bsz, length // tb),
        in_specs=[blk(GLA_KW), blk(GLA_KW), blk(GLA_WIDTH), blk(GLA_WIDTH), blk(LANES),
                  _resident(wg.shape), _resident(bg.shape), _resident(gout.shape), st],
        out_specs=(blk(GLA_WIDTH), st),
        out_shape=(jax.ShapeDtypeStruct((bsz, length, GLA_WIDTH), BF16),
                   jax.ShapeDtypeStruct(s0.shape, F32)),
        compiler_params=pltpu.CompilerParams(
            dimension_semantics=("arbitrary", "arbitrary"), vmem_limit_bytes=VMEM_LIMIT_BYTES),
        name="gla",
    )(q, k, v, z, lr, wg, bg, gout, s0)


def _swa_kernel(sink_ref, q_ref, kp_ref, vp_ref, kc_ref, vc_ref, z_ref, o_ref,
                *, chunk, n_chunks, mask_missing_past):
    kvw = SWA_GROUP * SWA_HEAD_DIM
    n_keys = WINDOW + chunk
    kcat = jnp.concatenate([kp_ref[...], kc_ref[...]], axis=0).astype(BF16)
    vcat = jnp.concatenate([vp_ref[...], vc_ref[...]], axis=0).astype(BF16)
    gq = SWA_GROUP * chunk
    lane_group = lax.broadcasted_iota(jnp.int32, (chunk, kvw), 1) // SWA_HEAD_DIM
    gq_group = lax.broadcasted_iota(jnp.int32, (1, gq), 1) // chunk
    key_i = lax.broadcasted_iota(jnp.int32, (n_keys, gq), 0)
    qscale = SWA_HEAD_DIM ** -0.5
    first_pos = pl.program_id(1) * (chunk * n_chunks) - WINDOW

    for h in range(SWA_KV_HEADS):
        hd = slice(h * SWA_HEAD_DIM, (h + 1) * SWA_HEAD_DIM)
        k_rep = jnp.concatenate([kcat[:, hd]] * SWA_GROUP, axis=1)
        v_h = vcat[:, hd]
        qs_cols = slice(h * kvw, (h + 1) * kvw)
        sink = jnp.zeros((1, gq), F32)
        for g in range(SWA_GROUP):
            sink = jnp.where(gq_group == g, sink_ref[h * SWA_GROUP + g], sink)
        for c in range(n_chunks):
            rows = slice(c * chunk, (c + 1) * chunk)
            keys = slice(c * chunk, c * chunk + n_keys)
            q = q_ref[rows, qs_cols]
            qs = jnp.concatenate(
                [jnp.where(lane_group == g, q, jnp.zeros_like(q)) for g in range(SWA_GROUP)],
                axis=0)
            s = _dot_nt(k_rep[keys], qs) * qscale
            if mask_missing_past and c * chunk < WINDOW:
                s = jnp.where(key_i + (first_pos + c * chunk) >= 0, s, jnp.finfo(F32).min)
            m = jnp.maximum(jnp.max(s, axis=0, keepdims=True), sink)
            p = jnp.exp(s - m)
            denom = jnp.sum(p, axis=0, keepdims=True) + jnp.exp(sink - m)
            o_t = (_dot_tn(v_h[keys], p.astype(BF16)) / denom).T
            o = jnp.concatenate([o_t[g * chunk:(g + 1) * chunk] for g in range(SWA_GROUP)],
                                axis=1)
            z = z_ref[rows, qs_cols].astype(F32)
            o_ref[rows, qs_cols] = (o * (z * _sigmoid(z))).astype(o_ref.dtype)


def _swa(sinks, q, z, k_past, v_past, k_cur, v_cur, chunk, qb, mask_missing_past):
    bsz, length, _ = q.shape
    n_chunks = qb // chunk
    blk = lambda w: pl.BlockSpec((None, qb, w), lambda b, i: (b, i, 0))
    kp_arr, kp_map = k_past
    vp_arr, vp_map = v_past
    kc_arr, kc_col = k_cur
    vc_arr, vc_col = v_cur
    return pl.pallas_call(
        functools.partial(_swa_kernel, chunk=chunk, n_chunks=n_chunks,
                          mask_missing_past=mask_missing_past),
        grid=(bsz, length // qb),
        in_specs=[pl.BlockSpec(memory_space=pltpu.SMEM),
                  blk(SWA_WIDTH),
                  pl.BlockSpec((None, WINDOW, SWA_KVW), kp_map),
                  pl.BlockSpec((None, WINDOW, SWA_KVW), vp_map),
                  pl.BlockSpec((None, qb, SWA_KVW), lambda b, i: (b, i, kc_col)),
                  pl.BlockSpec((None, qb, SWA_KVW), lambda b, i: (b, i, vc_col)),
                  blk(SWA_WIDTH)],
        out_specs=blk(SWA_WIDTH),
        out_shape=jax.ShapeDtypeStruct((bsz, length, SWA_WIDTH), BF16),
        compiler_params=pltpu.CompilerParams(
            dimension_semantics=("arbitrary", "arbitrary"), vmem_limit_bytes=VMEM_LIMIT_BYTES),
        name="swa",
    )(sinks, q, kp_arr, vp_arr, kc_arr, vc_arr, z)


def _outproj_kernel(x_ref, o1_ref, o2_ref, p_ref, wo1_ref, wo2_ref, gple_ref, wg_ref, wp_ref,
                    gfin_ref, y_ref):
    h = x_ref[...] + (_dot(o1_ref[...], wo1_ref[...]) + _dot(o2_ref[...], wo2_ref[...]))
    n = ((h * _rms_scale(h)) * gple_ref[...]).astype(BF16)
    gate = _sigmoid(_dot(n, wg_ref[...]))
    h = h + gate * _dot(p_ref[...].astype(BF16), wp_ref[...])
    y_ref[...] = (h * _rms_scale(h)) * gfin_ref[...]


def _outproj(x, o1, o2, p, wo1, wo2, gple, wg, wp, gfin, tm):
    t, d = x.shape
    row = lambda w: pl.BlockSpec((tm, w), lambda i: (i, 0))
    return pl.pallas_call(
        _outproj_kernel,
        grid=(t // tm,),
        in_specs=[row(d), row(o1.shape[1]), row(o2.shape[1]), row(p.shape[1]),
                  _resident(wo1.shape), _resident(wo2.shape), _resident(gple.shape),
                  _resident(wg.shape), _resident(wp.shape), _resident(gfin.shape)],
        out_specs=row(d),
        out_shape=jax.ShapeDtypeStruct((t, d), F32),
        compiler_params=pltpu.CompilerParams(
            dimension_semantics=("arbitrary",), vmem_limit_bytes=VMEM_LIMIT_BYTES),
        name="outproj",
    )(x, o1, o2, p, wo1, wo2, gple, wg, wp, gfin)


def _split_w_in(w_in):
    sizes = (GLA_KW, GLA_KW, GLA_WIDTH, GLA_WIDTH, GLA_RANK, SWA_WIDTH, SWA_KVW, SWA_KVW, SWA_WIDTH)
    offs = [0]
    for s in sizes:
        offs.append(offs[-1] + s)
    col = lambda a, b: w_in[:, offs[a]:offs[b]].astype(BF16)
    q1, k1, v1, z1, lr, q2, kv2, z2 = col(0, 1), col(1, 2), col(2, 3), col(3, 4), col(4, 5), \
        col(5, 6), col(6, 8), col(8, 9)
    lr = jnp.pad(lr, ((0, 0), (0, LANES - GLA_RANK)))
    return q1, k1, v1, z1, q2, kv2, z2, lr


def _layer(x, p, s0, past, wts, chunk, tm_in, tm_out, tb_gla, qb_swa):
    (g_mix, w_in_groups, w_gate, b_gate, g_gla_out, sinks, wo1, wo2, g_ple, w_ple_gate,
     w_ple_proj, g_final) = wts
    bsz, length, d = x.shape
    x2 = x.reshape(bsz * length, d)
    q1, k1, v1, z1, q2, kv2, z2, lr = _inproj(
        x2, g_mix, w_in_groups, (BF16, BF16, BF16, BF16, BF16, F32, BF16, BF16), tm_in)
    b3 = lambda a: a.reshape(bsz, length, a.shape[-1])
    o1, s_new = _gla(b3(q1), b3(k1), b3(v1), b3(z1), b3(lr), w_gate, b_gate, g_gla_out, s0,
                     chunk, tb_gla)
    kv3 = b3(kv2)
    if past is None:
        per_win = qb_swa // WINDOW
        prev = lambda col: (lambda b, i: (b, jnp.maximum(i * per_win - 1, 0), col))
        k_past, v_past = (kv3, prev(0)), (kv3, prev(1))
    else:
        k_past = (past[0], lambda b, i: (b, 0, 0))
        v_past = (past[1], lambda b, i: (b, 0, 0))
    o2 = _swa(sinks, b3(q2), b3(z2), k_past, v_past, (kv3, 0), (kv3, 1), chunk, qb_swa,
              mask_missing_past=past is None)
    y = _outproj(x2, o1.reshape(bsz * length, -1), o2.reshape(bsz * length, -1),
                 p.reshape(bsz * length, -1), wo1, wo2, g_ple, w_ple_gate, w_ple_proj, g_final,
                 tm_out)
    return y.reshape(bsz, length, d), s_new, kv3


def kernel(x_prompt, x_sample, p_prompt, p_sample, state_gla, cache_swa_k, cache_swa_v, g_mix, w_in, w_gate_up, b_gate, g_gla_out, swa_sinks, w_out, g_ple, w_ple_gate, w_ple_proj, g_final):
    depth = w_in.shape[0]
    assert depth == 1, "the final RMSNorm is fused into the last layer's output kernel"
    bp, lp, _ = x_prompt.shape
    bs, ls, _ = x_sample.shape
    i = 0
    row = lambda a: a.reshape(1, -1).astype(F32)
    wts = (row(g_mix[i]), _split_w_in(w_in[i]),
           jnp.pad(w_gate_up[i].astype(BF16), ((0, LANES - GLA_RANK), (0, 0))), row(b_gate[i]),
           row(g_gla_out[i]), swa_sinks[i].reshape(-1).astype(F32),
           w_out[i, :GLA_WIDTH].astype(BF16), w_out[i, GLA_WIDTH:].astype(BF16),
           row(g_ple[i]), w_ple_gate[i].astype(BF16), w_ple_proj[i].astype(BF16), row(g_final))

    s0_p = jnp.zeros((bp, GLA_HEADS, GLA_DK, GLA_DV), F32)
    y_p, s_p, kv_p = _layer(x_prompt, p_prompt[i], s0_p, None, wts, chunk=CHUNK,
                            tm_in=512, tm_out=256, tb_gla=512, qb_swa=256)
    keep = min(WINDOW, lp)
    k_p = kv_p[:, lp - keep:, :SWA_KVW].reshape(bp, keep, SWA_KV_HEADS, SWA_HEAD_DIM)
    v_p = kv_p[:, lp - keep:, SWA_KVW:].reshape(bp, keep, SWA_KV_HEADS, SWA_HEAD_DIM)

    past = (cache_swa_k[i].reshape(bs, -1, SWA_KVW), cache_swa_v[i].reshape(bs, -1, SWA_KVW))
    y_s, s_s, kv_s = _layer(x_sample, p_sample[i], state_gla[i], past, wts, chunk=min(CHUNK, ls),
                            tm_in=bs * ls, tm_out=bs * ls, tb_gla=ls, qb_swa=ls)
    k_s = kv_s[:, :, :SWA_KVW].reshape(bs, ls, SWA_KV_HEADS, SWA_HEAD_DIM)
    v_s = kv_s[:, :, SWA_KVW:].reshape(bs, ls, SWA_KV_HEADS, SWA_HEAD_DIM)
    return (y_p, y_s, s_p[None], k_p[None], v_p[None], s_s[None], k_s[None], v_s[None])
```

```python
import functools
from typing import Any, Callable, NamedTuple

import jax
import jax.numpy as jnp
from jax import lax
from jax.experimental import pallas as pl
from jax.experimental.pallas import tpu as pltpu

F32 = jnp.float32
BF16 = jnp.bfloat16

CHUNK = 64
RMS_EPS = 1e-6
GLA_HEADS = 4
GLA_DK = 128
GLA_DV = 256
GLA_KW = GLA_HEADS * GLA_DK
GLA_WIDTH = GLA_HEADS * GLA_DV
GLA_RANK = 16
GLA_TAU = 16.0
SWA_KV_HEADS = 4
SWA_GROUP = 4
SWA_HEAD_DIM = 64
SWA_WIDTH = SWA_KV_HEADS * SWA_GROUP * SWA_HEAD_DIM
SWA_KVW = SWA_KV_HEADS * SWA_HEAD_DIM
WINDOW = 128

LANES = 128
MXU_WIDTH = 256
VMEM_LIMIT_BYTES = 56 * 1024 * 1024
TILE_MATMUL_ONLY = 512
TILE_FUSED = 256

NT_DIMS = (((1,), (1,)), ((), ()))
TN_DIMS = (((0,), (0,)), ((), ()))


def _dot(a, b):
    return jnp.dot(a, b, preferred_element_type=F32)


def _dot_nt(a, b):
    return lax.dot_general(a, b, NT_DIMS, preferred_element_type=F32)


def _dot_tn(a, b):
    return lax.dot_general(a, b, TN_DIMS, preferred_element_type=F32)


def _rms_scale(x):
    return lax.rsqrt(jnp.mean(x * x, axis=-1, keepdims=True) + RMS_EPS)


def _sigmoid(x):
    return 1.0 / (1.0 + jnp.exp(-x))


def _resident(shape):
    zeros = (0,) * len(shape)
    return pl.BlockSpec(shape, lambda *_: zeros, pipeline_mode=pl.Buffered(1))


def _tile(tm, width, batch0=0):
    return pl.BlockSpec((None, tm, width), lambda b, i: (batch0 + b, i, 0))


def _col_pieces(width):
    return [slice(c, min(c + MXU_WIDTH, width)) for c in range(0, width, MXU_WIDTH)]


class _Task(NamedTuple):
    stages: Callable[..., Any]
    operands: tuple
    in_specs: tuple
    out_specs: tuple
    out_shapes: tuple
    scratch: tuple = ()
    aliases: Any = None


_DONE = object()


class _Stream:
    def __init__(self, units, max_active):
        self.pending, self.active, self.max_active = list(units), [], max_active

    def busy(self):
        return bool(self.pending or self.active)

    def step(self):
        self.active = [g for g in self.active if next(g, _DONE) is not _DONE]
        while self.pending and len(self.active) < self.max_active:
            g = self.pending.pop(0)
            if next(g, _DONE) is not _DONE:
                self.active.append(g)


def _drive(heavy, light):
    heavy = [g for g in heavy if g is not None]
    while heavy or any(s.busy() for s in light):
        heavy = [g for g in heavy if next(g, _DONE) is not _DONE]
        for s in light:
            s.step()


def _run(tasks, grid, name):
    n_in = [len(t.operands) for t in tasks]
    n_out = [len(t.out_shapes) for t in tasks]
    n_scr = [len(t.scratch) for t in tasks]

    def body(*refs):
        ins = refs[:sum(n_in)]
        outs = refs[sum(n_in):sum(n_in) + sum(n_out)]
        scrs = refs[sum(n_in) + sum(n_out):]
        heavy, light = [], []
        i = o = s = 0
        for t, ni, no, ns in zip(tasks, n_in, n_out, n_scr):
            h, l = t.stages(*ins[i:i + ni], *outs[o:o + no], *scrs[s:s + ns])
            heavy.append(h)
            light += l
            i, o, s = i + ni, o + no, s + ns
        _drive(heavy, light)

    aliases = {}
    i = o = 0
    for t, ni, no in zip(tasks, n_in, n_out):
        for src, dst in (t.aliases or {}).items():
            aliases[i + src] = o + dst
        i, o = i + ni, o + no

    flat = pl.pallas_call(
        body,
        grid=grid,
        in_specs=[s for t in tasks for s in t.in_specs],
        out_specs=tuple(s for t in tasks for s in t.out_specs),
        out_shape=tuple(s for t in tasks for s in t.out_shapes),
        scratch_shapes=[s for t in tasks for s in t.scratch],
        input_output_aliases=aliases,
        compiler_params=pltpu.CompilerParams(
            dimension_semantics=("arbitrary", "arbitrary"), vmem_limit_bytes=VMEM_LIMIT_BYTES),
        name=name,
    )(*[a for t in tasks for a in t.operands])
    results, o = [], 0
    for no in n_out:
        results.append(tuple(flat[o:o + no]))
        o += no
    return results


def _inproj_stages(x_ref, g_ref, *refs):
    n_out = len(refs) // 2
    w_refs, o_refs = refs[:n_out], refs[n_out:]

    def pieces():
        x = x_ref[...]
        n = ((x * _rms_scale(x)) * g_ref[...]).astype(BF16)
        for w_ref, o_ref in zip(w_refs, o_refs):
            for cs in _col_pieces(w_ref.shape[1]):
                o_ref[:, cs] = _dot(n, w_ref[:, cs]).astype(o_ref.dtype)
                yield

    return pieces(), []


IN_DTYPES = (BF16, BF16, BF16, BF16, BF16, F32, BF16, BF16)


def _inproj_task(x, batch0, n_batch, tm, g_mix, weights):
    _, length, d = x.shape
    return _Task(
        stages=_inproj_stages,
        operands=(x, g_mix, *weights),
        in_specs=(_tile(tm, d, batch0), _resident(g_mix.shape), *[_resident(w.shape) for w in weights]),
        out_specs=tuple(_tile(tm, w.shape[1]) for w in weights),
        out_shapes=tuple(jax.ShapeDtypeStruct((n_batch, length, w.shape[1]), dt)
                         for w, dt in zip(weights, IN_DTYPES)))


def _gla_stages(q_ref, k_ref, v_ref, z_ref, lr_ref, wg_ref, bg_ref, gout_ref, s0_ref,
                o_ref, s_ref, *, chunk, n_chunks):
    @pl.when(pl.program_id(1) == 0)
    def _():
        s_ref[...] = s0_ref[...]

    row = lax.broadcasted_iota(jnp.int32, (chunk, chunk), 0)
    col = lax.broadcasted_iota(jnp.int32, (chunk, chunk), 1)
    causal = row >= col
    row3 = lax.broadcasted_iota(jnp.int32, (chunk, 3 * chunk), 0)
    col3 = lax.broadcasted_iota(jnp.int32, (chunk, 3 * chunk), 1)
    tril3 = jnp.where(row3 >= lax.rem(col3, chunk), 1.0, 0.0).astype(BF16)
    eye = (lax.broadcasted_iota(jnp.int32, (GLA_DK, GLA_DK), 0)
           == lax.broadcasted_iota(jnp.int32, (GLA_DK, GLA_DK), 1))
    scale = GLA_DK ** -0.5
    heads = [(slice(h * GLA_DK, (h + 1) * GLA_DK), slice(h * GLA_DV, (h + 1) * GLA_DV))
             for h in range(GLA_HEADS)]

    def chunk_unit(c):
        r = slice(c * chunk, (c + 1) * chunk)
        gpre = _dot(lr_ref[r, :], wg_ref[...])
        yield
        gpre = gpre + bg_ref[...]
        g = (jnp.minimum(gpre, 0.0) - jnp.log1p(jnp.exp(-jnp.abs(gpre)))) * (1.0 / GLA_TAU)
        g_hi = g.astype(BF16)
        rem = g - g_hi.astype(F32)
        g_mid = rem.astype(BF16)
        g_lo = (rem - g_mid.astype(F32)).astype(BF16)
        b = _dot(tril3, jnp.concatenate([g_hi, g_mid, g_lo], axis=0))
        yield
        b_last = b[chunk - 1:chunk, :]
        q = q_ref[r, :].astype(F32)
        k = k_ref[r, :].astype(F32)
        qe = ((q * scale) * jnp.exp(b)).astype(BF16)
        ke = (k * jnp.exp(-b)).astype(BF16)
        kd = (k * jnp.exp(b_last - b)).astype(BF16)
        decay = jnp.exp(b_last)
        v = [v_ref[r, vs] for _, vs in heads]
        a_raw = [_dot_nt(qe[:, ks], ke[:, ks]) for ks, _ in heads]
        kv = [_dot_tn(kd[:, ks], v[h]) for h, (ks, _) in enumerate(heads)]
        yield
        o = []
        for h, (ks, _) in enumerate(heads):
            a = jnp.where(causal, a_raw[h], 0.0).astype(BF16)
            s = s_ref[h]
            o.append(_dot(jnp.concatenate([qe[:, ks], a], axis=1),
                          jnp.concatenate([s.astype(BF16), v[h]], axis=0)))
            dcol = jnp.sum(jnp.where(eye, decay[:, ks], 0.0), axis=1, keepdims=True)
            s_ref[h] = dcol * s + kv[h]
        yield
        for h, (_, vs) in enumerate(heads):
            oh = (o[h] * _rms_scale(o[h])) * gout_ref[...]
            z = z_ref[r, vs].astype(F32)
            o_ref[r, vs] = (oh * (z * _sigmoid(z))).astype(o_ref.dtype)

    return None, [_Stream([chunk_unit(c) for c in range(n_chunks)], max_active=2)]


def _gla_task(q, k, v, z, lr, wg, bg, gout, s0, state_batch0, chunk, tb):
    n_batch, length, _ = q.shape
    st_in = pl.BlockSpec((None, GLA_HEADS, GLA_DK, GLA_DV),
                         lambda b, i: (state_batch0 + b, 0, 0, 0))
    st_out = pl.BlockSpec((None, GLA_HEADS, GLA_DK, GLA_DV), lambda b, i: (b, 0, 0, 0))
    return _Task(
        stages=functools.partial(_gla_stages, chunk=chunk, n_chunks=tb // chunk),
        operands=(q, k, v, z, lr, wg, bg, gout, s0),
        in_specs=(_tile(tb, GLA_KW), _tile(tb, GLA_KW), _tile(tb, GLA_WIDTH), _tile(tb, GLA_WIDTH),
                  _tile(tb, LANES), _resident(wg.shape), _resident(bg.shape),
                  _resident(gout.shape), st_in),
        out_specs=(_tile(tb, GLA_WIDTH), st_out),
        out_shapes=(jax.ShapeDtypeStruct((n_batch, length, GLA_WIDTH), BF16),
                    jax.ShapeDtypeStruct((n_batch, GLA_HEADS, GLA_DK, GLA_DV), F32)))


def _swa_stages(sink_ref, q_ref, kp_ref, vp_ref, kc_ref, vc_ref, z_ref, o_ref,
                *, chunk, n_chunks, mask_missing_past):
    kvw = SWA_GROUP * SWA_HEAD_DIM
    n_keys = WINDOW + chunk
    gq = SWA_GROUP * chunk
    lane_group = lax.broadcasted_iota(jnp.int32, (chunk, kvw), 1) // SWA_HEAD_DIM
    gq_group = lax.broadcasted_iota(jnp.int32, (1, gq), 1) // chunk
    key_i = lax.broadcasted_iota(jnp.int32, (n_keys, gq), 0)
    qscale = SWA_HEAD_DIM ** -0.5
    first_pos = pl.program_id(1) * (chunk * n_chunks) - WINDOW
    shared = {}

    def head_operands(h):
        if not shared:
            shared["k"] = jnp.concatenate([kp_ref[...], kc_ref[...]], axis=0).astype(BF16)
            shared["v"] = jnp.concatenate([vp_ref[...], vc_ref[...]], axis=0).astype(BF16)
        if h not in shared:
            hd = slice(h * SWA_HEAD_DIM, (h + 1) * SWA_HEAD_DIM)
            k_rep = jnp.concatenate([shared["k"][:, hd]] * SWA_GROUP, axis=1)
            sink = jnp.zeros((1, gq), F32)
            for g in range(SWA_GROUP):
                sink = jnp.where(gq_group == g, sink_ref[h * SWA_GROUP + g], sink)
            shared[h] = (k_rep, shared["v"][:, hd], sink)
        return shared[h]

    def unit(h, c):
        k_rep, v_h, sink = head_operands(h)
        rows = slice(c * chunk, (c + 1) * chunk)
        keys = slice(c * chunk, c * chunk + n_keys)
        cols = slice(h * kvw, (h + 1) * kvw)
        q = q_ref[rows, cols]
        qs = jnp.concatenate(
            [jnp.where(lane_group == g, q, jnp.zeros_like(q)) for g in range(SWA_GROUP)],
            axis=0)
        s = _dot_nt(k_rep[keys], qs)
        yield
        s = s * qscale
        if mask_missing_past and c * chunk < WINDOW:
            s = jnp.where(key_i + (first_pos + c * chunk) >= 0, s, jnp.finfo(F32).min)
        m = jnp.maximum(jnp.max(s, axis=0, keepdims=True), sink)
        p = jnp.exp(s - m)
        denom = jnp.sum(p, axis=0, keepdims=True) + jnp.exp(sink - m)
        o_t = _dot_tn(v_h[keys], p.astype(BF16))
        yield
        o_t = (o_t / denom).T
        o = jnp.concatenate([o_t[g * chunk:(g + 1) * chunk] for g in range(SWA_GROUP)],
                            axis=1)
        z = z_ref[rows, cols].astype(F32)
        o_ref[rows, cols] = (o * (z * _sigmoid(z))).astype(o_ref.dtype)

    units = [unit(h, c) for h in range(SWA_KV_HEADS) for c in range(n_chunks)]
    return None, [_Stream(units, max_active=2)]


def _swa_task(sinks, q, z, kv, past, chunk, qb):
    n_batch, length, _ = q.shape
    if past is None:
        per_win = qb // WINDOW
        prev = lambda col: pl.BlockSpec(
            (None, WINDOW, SWA_KVW), lambda b, i: (b, jnp.maximum(i * per_win - 1, 0), col))
        k_past, v_past, kp_spec, vp_spec = kv, kv, prev(0), prev(1)
    else:
        k_past, v_past = past
        kp_spec = vp_spec = pl.BlockSpec((None, WINDOW, SWA_KVW), lambda b, i: (b, 0, 0))
    cur = lambda col: pl.BlockSpec((None, qb, SWA_KVW), lambda b, i: (b, i, col))
    return _Task(
        stages=functools.partial(_swa_stages, chunk=chunk, n_chunks=qb // chunk,
                                 mask_missing_past=past is None),
        operands=(sinks, q, k_past, v_past, kv, kv, z),
        in_specs=(pl.BlockSpec(memory_space=pltpu.SMEM), _tile(qb, SWA_WIDTH), kp_spec, vp_spec,
                  cur(0), cur(1), _tile(qb, SWA_WIDTH)),
        out_specs=(_tile(qb, SWA_WIDTH),),
        out_shapes=(jax.ShapeDtypeStruct((n_batch, length, SWA_WIDTH), BF16),))


def _outproj_stages(x_ref, o1_ref, o2_ref, p_ref, wo1_ref, wo2_ref, gple_ref, wg_ref, wp_ref,
                    gfin_ref, *refs):
    y_ref, h_ref = refs[-2:]

    def pieces():
        o1, o2 = o1_ref[...], o2_ref[...]
        for cs in _col_pieces(h_ref.shape[1]):
            h_ref[:, cs] = x_ref[:, cs] + (_dot(o1, wo1_ref[:, cs]) + _dot(o2, wo2_ref[:, cs]))
            yield
        h = h_ref[...]
        n = ((h * _rms_scale(h)) * gple_ref[...]).astype(BF16)
        p = p_ref[...].astype(BF16)
        for cs in _col_pieces(h_ref.shape[1]):
            gate = _sigmoid(_dot(n, wg_ref[:, cs]))
            h_ref[:, cs] = h_ref[:, cs] + gate * _dot(p, wp_ref[:, cs])
            yield
        h = h_ref[...]
        y_ref[...] = (h * _rms_scale(h)) * gfin_ref[...]

    return pieces(), []


def _outproj_task(x, o1, o2, p, batch0, tm, wo1, wo2, gple, wg, wp, gfin, y_prev=None):
    _, _, d = x.shape
    consts = (wo1, wo2, gple, wg, wp, gfin)
    operands = (x, o1, o2, p, *consts)
    in_specs = (_tile(tm, d, batch0), _tile(tm, o1.shape[2]), _tile(tm, o2.shape[2]),
                _tile(tm, p.shape[2], batch0), *[_resident(c.shape) for c in consts])
    aliases = None
    if y_prev is not None:
        aliases = {len(operands): 0}
        operands += (y_prev,)
        in_specs += (pl.BlockSpec(memory_space=pl.ANY),)
    return _Task(stages=_outproj_stages, operands=operands, in_specs=in_specs,
                 out_specs=(_tile(tm, d, batch0),),
                 out_shapes=(jax.ShapeDtypeStruct(x.shape, F32),),
                 scratch=(pltpu.VMEM((tm, d), F32),), aliases=aliases)


def _split_w_in(w_in):
    sizes = (GLA_KW, GLA_KW, GLA_WIDTH, GLA_WIDTH, GLA_RANK, SWA_WIDTH, SWA_KVW, SWA_KVW, SWA_WIDTH)
    offs = [0]
    for s in sizes:
        offs.append(offs[-1] + s)
    col = lambda a, b: w_in[:, offs[a]:offs[b]].astype(BF16)
    q1, k1, v1, z1, lr, q2, kv2, z2 = col(0, 1), col(1, 2), col(2, 3), col(3, 4), col(4, 5), \
        col(5, 6), col(6, 8), col(8, 9)
    lr = jnp.pad(lr, ((0, 0), (0, LANES - GLA_RANK)))
    return q1, k1, v1, z1, q2, kv2, z2, lr


class _Weights(NamedTuple):
    g_mix: Any
    w_in: tuple
    w_gate: Any
    b_gate: Any
    g_gla_out: Any
    sinks: Any
    wo1: Any
    wo2: Any
    g_ple: Any
    w_ple_gate: Any
    w_ple_proj: Any
    g_final: Any


def _attn_tasks(w, proj, s0, state_batch0, past, chunk, tile):
    q1, k1, v1, z1, q2, kv2, z2, lr = proj
    return [_gla_task(q1, k1, v1, z1, lr, w.w_gate, w.b_gate, w.g_gla_out, s0, state_batch0,
                      chunk, tile),
            _swa_task(w.sinks, q2, z2, kv2, past, chunk, tile)]


def _prompt_path(w, x, p, s0):
    n_batch, length, _ = x.shape
    assert n_batch == 2, "a call holding all three stages would not fit the VMEM budget"
    proj, attn = {}, {}
    y = None
    for s in range(n_batch + 2):
        has_in, has_attn, has_out = s < n_batch, 1 <= s <= n_batch, s >= 2
        tile = TILE_FUSED if has_attn else TILE_MATMUL_ONLY
        tasks, names = [], []
        if has_attn:
            tasks += _attn_tasks(w, proj[s - 1], s0, s - 1, None, CHUNK, tile)
            names.append("attn")
        if has_in:
            tasks.append(_inproj_task(x, s, 1, tile, w.g_mix, w.w_in))
            names.append("inproj")
        if has_out:
            o1, o2, _ = attn[s - 2]
            tasks.append(_outproj_task(x, o1, o2, p, s - 2, tile, w.wo1, w.wo2, w.g_ple,
                                       w.w_ple_gate, w.w_ple_proj, w.g_final, y_prev=y))
            names.append("outproj")
        res = _run(tasks, (1, length // tile), "_".join(names))
        if has_attn:
            (o1, s_new), (o2,) = res[0], res[1]
            attn[s - 1] = (o1, o2, s_new)
            res = res[2:]
        if has_in:
            proj[s] = res[0]
            res = res[1:]
        if has_out:
            (y,) = res[0]
    states = jnp.concatenate([attn[b][2] for b in range(n_batch)], axis=0)
    kv = [proj[b][5] for b in range(n_batch)]
    return y, states, kv


def _sample_path(w, x, p, s0, past):
    n_batch, length, d = x.shape
    flat = lambda a: a.reshape(1, n_batch * length, a.shape[-1])
    per_batch = lambda a: a.reshape(n_batch, length, a.shape[-1])
    (proj,) = _run([_inproj_task(flat(x), 0, 1, n_batch * length, w.g_mix, w.w_in)], (1, 1),
                   "inproj_sample")
    proj = tuple(per_batch(a) for a in proj)
    (o1, s_new), (o2,) = _run(_attn_tasks(w, proj, s0, 0, past, min(CHUNK, length), length),
                              (n_batch, 1), "attn_sample")
    ((y,),) = _run([_outproj_task(flat(x), flat(o1), flat(o2), flat(p), 0, n_batch * length,
                                  w.wo1, w.wo2, w.g_ple, w.w_ple_gate, w.w_ple_proj, w.g_final)],
                   (1, 1), "outproj_sample")
    return y.reshape(n_batch, length, d), s_new, proj[5]


def kernel(x_prompt, x_sample, p_prompt, p_sample, state_gla, cache_swa_k, cache_swa_v, g_mix, w_in, w_gate_up, b_gate, g_gla_out, swa_sinks, w_out, g_ple, w_ple_gate, w_ple_proj, g_final):
    depth = w_in.shape[0]
    assert depth == 1, "the final RMSNorm is fused into the last layer's output kernel"
    bp, lp, _ = x_prompt.shape
    bs, ls, _ = x_sample.shape
    assert cache_swa_k.shape[2] == WINDOW
    i = 0
    row = lambda a: a.reshape(1, -1).astype(F32)
    w = _Weights(
        g_mix=row(g_mix[i]), w_in=_split_w_in(w_in[i]),
        w_gate=jnp.pad(w_gate_up[i].astype(BF16), ((0, LANES - GLA_RANK), (0, 0))),
        b_gate=row(b_gate[i]), g_gla_out=row(g_gla_out[i]),
        sinks=swa_sinks[i].reshape(-1).astype(F32),
        wo1=w_out[i, :GLA_WIDTH].astype(BF16), wo2=w_out[i, GLA_WIDTH:].astype(BF16),
        g_ple=row(g_ple[i]), w_ple_gate=w_ple_gate[i].astype(BF16),
        w_ple_proj=w_ple_proj[i].astype(BF16), g_final=row(g_final))

    s0_p = jnp.zeros((bp, GLA_HEADS, GLA_DK, GLA_DV), F32)
    y_p, s_p, kv_p = _prompt_path(w, x_prompt, p_prompt[i], s0_p)
    keep = min(WINDOW, lp)
    last = lambda col: jnp.concatenate(
        [a[:, lp - keep:, col * SWA_KVW:(col + 1) * SWA_KVW] for a in kv_p], axis=0
    ).reshape(bp, keep, SWA_KV_HEADS, SWA_HEAD_DIM)
    k_p, v_p = last(0), last(1)

    past = (cache_swa_k[i].reshape(bs, WINDOW, SWA_KVW), cache_swa_v[i].reshape(bs, WINDOW, SWA_KVW))
    y_s, s_s, kv_s = _sample_path(w, x_sample, p_sample[i], state_gla[i], past)
    k_s = kv_s[:, :, :SWA_KVW].reshape(bs, ls, SWA_KV_HEADS, SWA_HEAD_DIM)
    v_s = kv_s[:, :, SWA_KVW:].reshape(bs, ls, SWA_KV_HEADS, SWA_HEAD_DIM)
    return (y_p, y_s, s_p[None], k_p[None], v_p[None], s_s[None], k_s[None], v_s[None])
```

```python
import functools
from typing import Any, Callable, NamedTuple

import jax
import jax.numpy as jnp
from jax import lax
from jax.experimental import pallas as pl
from jax.experimental.pallas import tpu as pltpu

F32 = jnp.float32
BF16 = jnp.bfloat16

CHUNK = 64
RMS_EPS = 1e-6
GLA_HEADS = 4
GLA_DK = 128
GLA_DV = 256
GLA_KW = GLA_HEADS * GLA_DK
GLA_WIDTH = GLA_HEADS * GLA_DV
GLA_RANK = 16
GLA_TAU = 16.0
SWA_KV_HEADS = 4
SWA_GROUP = 4
SWA_HEAD_DIM = 64
SWA_WIDTH = SWA_KV_HEADS * SWA_GROUP * SWA_HEAD_DIM
SWA_KVW = SWA_KV_HEADS * SWA_HEAD_DIM
WINDOW = 128

LANES = 128
MXU_WIDTH = 256
WEIGHT_BLOCK = 512
VMEM_LIMIT_BYTES = 56 * 1024 * 1024
TILE_MATMUL_ONLY = 512
TILE_FUSED = 256
GLA_IN_FLIGHT = 2
SWA_IN_FLIGHT = 4

NT_DIMS = (((1,), (1,)), ((), ()))
TN_DIMS = (((0,), (0,)), ((), ()))


def _dot(a, b):
    return jnp.dot(a, b, preferred_element_type=F32)


def _dot_nt(a, b):
    return lax.dot_general(a, b, NT_DIMS, preferred_element_type=F32)


def _dot_tn(a, b):
    return lax.dot_general(a, b, TN_DIMS, preferred_element_type=F32)


def _rms_scale(x):
    return lax.rsqrt(jnp.mean(x * x, axis=-1, keepdims=True) + RMS_EPS)


def _sigmoid(x):
    return 1.0 / (1.0 + jnp.exp(-x))


def _resident(shape):
    zeros = (0,) * len(shape)
    return pl.BlockSpec(shape, lambda *_: zeros, pipeline_mode=pl.Buffered(1))


def _tile(tm, width, batch0=0):
    return pl.BlockSpec((None, tm, width), lambda b, i: (batch0 + b, i, 0))


def _col_blocks(w):
    return tuple(w[:, c:c + WEIGHT_BLOCK] for c in range(0, w.shape[1], WEIGHT_BLOCK))


def _block_pieces(block_refs):
    out, base = [], 0
    for ref in block_refs:
        width = ref.shape[1]
        for c in range(0, width, MXU_WIDTH):
            hi = min(c + MXU_WIDTH, width)
            out.append((slice(base + c, base + hi), ref, slice(c, hi)))
        base += width
    return out


class _Task(NamedTuple):
    stages: Callable[..., Any]
    operands: tuple
    in_specs: tuple
    out_specs: tuple
    out_shapes: tuple
    scratch: tuple = ()
    aliases: Any = None


_DONE = object()


class _Stream:
    def __init__(self, units, max_active):
        self.pending, self.active, self.max_active = list(units), [], max_active

    def busy(self):
        return bool(self.pending or self.active)

    def step(self):
        self.active = [g for g in self.active if next(g, _DONE) is not _DONE]
        if self.pending and len(self.active) < self.max_active:
            g = self.pending.pop(0)
            if next(g, _DONE) is not _DONE:
                self.active.append(g)


def _drive(heavy, light):
    heavy = [g for g in heavy if g is not None]
    while heavy or any(s.busy() for s in light):
        heavy = [g for g in heavy if next(g, _DONE) is not _DONE]
        for s in light:
            s.step()


def _run(tasks, grid, name):
    n_in = [len(t.operands) for t in tasks]
    n_out = [len(t.out_shapes) for t in tasks]
    n_scr = [len(t.scratch) for t in tasks]

    def body(*refs):
        ins = refs[:sum(n_in)]
        outs = refs[sum(n_in):sum(n_in) + sum(n_out)]
        scrs = refs[sum(n_in) + sum(n_out):]
        heavy, light = [], []
        i = o = s = 0
        for t, ni, no, ns in zip(tasks, n_in, n_out, n_scr):
            h, l = t.stages(*ins[i:i + ni], *outs[o:o + no], *scrs[s:s + ns])
            heavy.append(h)
            light += l
            i, o, s = i + ni, o + no, s + ns
        _drive(heavy, light)

    aliases = {}
    i = o = 0
    for t, ni, no in zip(tasks, n_in, n_out):
        for src, dst in (t.aliases or {}).items():
            aliases[i + src] = o + dst
        i, o = i + ni, o + no

    flat = pl.pallas_call(
        body,
        grid=grid,
        in_specs=[s for t in tasks for s in t.in_specs],
        out_specs=tuple(s for t in tasks for s in t.out_specs),
        out_shape=tuple(s for t in tasks for s in t.out_shapes),
        scratch_shapes=[s for t in tasks for s in t.scratch],
        input_output_aliases=aliases,
        compiler_params=pltpu.CompilerParams(
            dimension_semantics=("arbitrary", "arbitrary"), vmem_limit_bytes=VMEM_LIMIT_BYTES),
        name=name,
    )(*[a for t in tasks for a in t.operands])
    results, o = [], 0
    for no in n_out:
        results.append(tuple(flat[o:o + no]))
        o += no
    return results


def _inproj_stages(x_ref, g_ref, *refs, blocks_per_group):
    n_blocks = sum(blocks_per_group)
    w_refs, o_refs = refs[:n_blocks], refs[n_blocks:]

    def pieces():
        x = x_ref[...]
        n = ((x * _rms_scale(x)) * g_ref[...]).astype(BF16)
        first = 0
        for n_blk, o_ref in zip(blocks_per_group, o_refs):
            for cols, w_ref, cs in _block_pieces(w_refs[first:first + n_blk]):
                o_ref[:, cols] = _dot(n, w_ref[:, cs]).astype(o_ref.dtype)
                yield
            first += n_blk

    return pieces(), []


IN_DTYPES = (BF16, BF16, BF16, BF16, BF16, F32, BF16, BF16)


def _inproj_task(x, batch0, n_batch, tm, g_mix, groups):
    _, length, d = x.shape
    blocks = [w for g in groups for w in g]
    widths = [sum(w.shape[1] for w in g) for g in groups]
    return _Task(
        stages=functools.partial(_inproj_stages, blocks_per_group=tuple(len(g) for g in groups)),
        operands=(x, g_mix, *blocks),
        in_specs=(_tile(tm, d, batch0), _resident(g_mix.shape), *[_resident(w.shape) for w in blocks]),
        out_specs=tuple(_tile(tm, n) for n in widths),
        out_shapes=tuple(jax.ShapeDtypeStruct((n_batch, length, n), dt)
                         for n, dt in zip(widths, IN_DTYPES)))


def _gla_stages(q_ref, k_ref, v_ref, z_ref, lr_ref, wg_ref, bg_ref, gout_ref, s0_ref,
                o_ref, s_ref, *, chunk, n_chunks):
    @pl.when(pl.program_id(1) == 0)
    def _():
        s_ref[...] = s0_ref[...]

    row = lax.broadcasted_iota(jnp.int32, (chunk, chunk), 0)
    col = lax.broadcasted_iota(jnp.int32, (chunk, chunk), 1)
    causal = row >= col
    row3 = lax.broadcasted_iota(jnp.int32, (chunk, 3 * chunk), 0)
    col3 = lax.broadcasted_iota(jnp.int32, (chunk, 3 * chunk), 1)
    tril3 = jnp.where(row3 >= lax.rem(col3, chunk), 1.0, 0.0).astype(BF16)
    eye = (lax.broadcasted_iota(jnp.int32, (GLA_DK, GLA_DK), 0)
           == lax.broadcasted_iota(jnp.int32, (GLA_DK, GLA_DK), 1))
    scale = GLA_DK ** -0.5
    heads = [(slice(h * GLA_DK, (h + 1) * GLA_DK), slice(h * GLA_DV, (h + 1) * GLA_DV))
             for h in range(GLA_HEADS)]
    transpose_ahead = True

    def chunk_unit(c):
        r = slice(c * chunk, (c + 1) * chunk)
        gpre = _dot(lr_ref[r, :], wg_ref[...])
        yield
        gpre = gpre + bg_ref[...]
        g = (jnp.minimum(gpre, 0.0) - jnp.log1p(jnp.exp(-jnp.abs(gpre)))) * (1.0 / GLA_TAU)
        g_hi = g.astype(BF16)
        rem = g - g_hi.astype(F32)
        g_mid = rem.astype(BF16)
        g_lo = (rem - g_mid.astype(F32)).astype(BF16)
        g3 = jnp.concatenate([g_hi, g_mid, g_lo], axis=0)
        yield
        b = _dot(tril3, g3)
        yield
        b_last = b[chunk - 1:chunk, :]
        q = q_ref[r, :].astype(F32)
        k = k_ref[r, :].astype(F32)
        qe = ((q * scale) * jnp.exp(b)).astype(BF16)
        ke = (k * jnp.exp(-b)).astype(BF16)
        kd = k * jnp.exp(b_last - b)
        decay = jnp.exp(b_last)
        v = [v_ref[r, vs] for _, vs in heads]
        if transpose_ahead:
            kd_t = [kd[:, ks].T.astype(BF16) for ks, _ in heads]
        else:
            kd = kd.astype(BF16)
        yield
        a_raw = [_dot_nt(qe[:, ks], ke[:, ks]) for ks, _ in heads]
        if transpose_ahead:
            kv = [_dot(kd_t[h], v[h]) for h in range(GLA_HEADS)]
        else:
            kv = [_dot_tn(kd[:, ks], v[h]) for h, (ks, _) in enumerate(heads)]
        yield
        lhs, rhs = [], []
        for h, (ks, _) in enumerate(heads):
            a = jnp.where(causal, a_raw[h], 0.0).astype(BF16)
            s = s_ref[h]
            lhs.append(jnp.concatenate([qe[:, ks], a], axis=1))
            rhs.append(jnp.concatenate([s.astype(BF16), v[h]], axis=0))
            dcol = jnp.sum(jnp.where(eye, decay[:, ks], 0.0), axis=1, keepdims=True)
            s_ref[h] = dcol * s + kv[h]
        yield
        o = [_dot(lhs[h], rhs[h]) for h in range(GLA_HEADS)]
        yield
        for h, (_, vs) in enumerate(heads):
            oh = (o[h] * _rms_scale(o[h])) * gout_ref[...]
            z = z_ref[r, vs].astype(F32)
            o_ref[r, vs] = (oh * (z * _sigmoid(z))).astype(o_ref.dtype)

    return None, [_Stream([chunk_unit(c) for c in range(n_chunks)], max_active=GLA_IN_FLIGHT)]


def _gla_task(q, k, v, z, lr, wg, bg, gout, s0, state_batch0, chunk, tb):
    n_batch, length, _ = q.shape
    st_in = pl.BlockSpec((None, GLA_HEADS, GLA_DK, GLA_DV),
                         lambda b, i: (state_batch0 + b, 0, 0, 0))
    st_out = pl.BlockSpec((None, GLA_HEADS, GLA_DK, GLA_DV), lambda b, i: (b, 0, 0, 0))
    return _Task(
        stages=functools.partial(_gla_stages, chunk=chunk, n_chunks=tb // chunk),
        operands=(q, k, v, z, lr, wg, bg, gout, s0),
        in_specs=(_tile(tb, GLA_KW), _tile(tb, GLA_KW), _tile(tb, GLA_WIDTH), _tile(tb, GLA_WIDTH),
                  _tile(tb, LANES), _resident(wg.shape), _resident(bg.shape),
                  _resident(gout.shape), st_in),
        out_specs=(_tile(tb, GLA_WIDTH), st_out),
        out_shapes=(jax.ShapeDtypeStruct((n_batch, length, GLA_WIDTH), BF16),
                    jax.ShapeDtypeStruct((n_batch, GLA_HEADS, GLA_DK, GLA_DV), F32)))


def _swa_stages(sink_ref, q_ref, kp_ref, vp_ref, kc_ref, vc_ref, z_ref, o_ref,
                *, chunk, n_chunks, mask_missing_past):
    kvw = SWA_GROUP * SWA_HEAD_DIM
    n_keys = WINDOW + chunk
    gq = SWA_GROUP * chunk
    lane_group = lax.broadcasted_iota(jnp.int32, (chunk, kvw), 1) // SWA_HEAD_DIM
    gq_group = lax.broadcasted_iota(jnp.int32, (1, gq), 1) // chunk
    key_i = lax.broadcasted_iota(jnp.int32, (n_keys, gq), 0)
    qscale = SWA_HEAD_DIM ** -0.5
    first_pos = pl.program_id(1) * (chunk * n_chunks) - WINDOW
    shared = {}

    def head_operands(h):
        if not shared:
            shared["k"] = jnp.concatenate([kp_ref[...], kc_ref[...]], axis=0).astype(BF16)
            v = jnp.concatenate([vp_ref[...], vc_ref[...]], axis=0)
            shared["vt"] = v.T.astype(BF16)
        if h not in shared:
            hd = slice(h * SWA_HEAD_DIM, (h + 1) * SWA_HEAD_DIM)
            k_rep = jnp.concatenate([shared["k"][:, hd]] * SWA_GROUP, axis=1)
            vt = shared["vt"][hd, :]
            vt_at = {}
            for c in range(n_chunks):
                shift = (c * chunk) % LANES
                if shift not in vt_at:
                    vt_at[shift] = vt[:, shift:]
            sink = jnp.zeros((1, gq), F32)
            for g in range(SWA_GROUP):
                sink = jnp.where(gq_group == g, sink_ref[h * SWA_GROUP + g], sink)
            shared[h] = (k_rep, vt_at, sink)
        return shared[h]

    def unit(h, c):
        k_rep, vt_at, sink = head_operands(h)
        rows = slice(c * chunk, (c + 1) * chunk)
        keys = slice(c * chunk, c * chunk + n_keys)
        cols = slice(h * kvw, (h + 1) * kvw)
        q = q_ref[rows, cols]
        qs = jnp.concatenate(
            [jnp.where(lane_group == g, q, jnp.zeros_like(q)) for g in range(SWA_GROUP)],
            axis=0)
        shift = (c * chunk) % LANES
        vt_keys = vt_at[shift][:, c * chunk - shift:c * chunk - shift + n_keys]
        yield
        s = _dot_nt(k_rep[keys], qs)
        yield
        s = s * qscale
        if mask_missing_past and c * chunk < WINDOW:
            s = jnp.where(key_i + (first_pos + c * chunk) >= 0, s, jnp.finfo(F32).min)
        m = jnp.maximum(jnp.max(s, axis=0, keepdims=True), sink)
        p = jnp.exp(s - m)
        denom = jnp.sum(p, axis=0, keepdims=True) + jnp.exp(sink - m)
        p = p.astype(BF16)
        yield
        o_t = _dot(vt_keys, p)
        yield
        o_t = (o_t / denom).T
        o = jnp.concatenate([o_t[g * chunk:(g + 1) * chunk] for g in range(SWA_GROUP)],
                            axis=1)
        z = z_ref[rows, cols].astype(F32)
        o_ref[rows, cols] = (o * (z * _sigmoid(z))).astype(o_ref.dtype)

    units = [unit(h, c) for h in range(SWA_KV_HEADS) for c in range(n_chunks)]
    return None, [_Stream(units, max_active=SWA_IN_FLIGHT)]


def _swa_task(sinks, q, z, kv, past, chunk, qb):
    n_batch, length, _ = q.shape
    if past is None:
        per_win = qb // WINDOW
        prev = lambda col: pl.BlockSpec(
            (None, WINDOW, SWA_KVW), lambda b, i: (b, jnp.maximum(i * per_win - 1, 0), col))
        k_past, v_past, kp_spec, vp_spec = kv, kv, prev(0), prev(1)
    else:
        k_past, v_past = past
        kp_spec = vp_spec = pl.BlockSpec((None, WINDOW, SWA_KVW), lambda b, i: (b, 0, 0))
    cur = lambda col: pl.BlockSpec((None, qb, SWA_KVW), lambda b, i: (b, i, col))
    return _Task(
        stages=functools.partial(_swa_stages, chunk=chunk, n_chunks=qb // chunk,
                                 mask_missing_past=past is None),
        operands=(sinks, q, k_past, v_past, kv, kv, z),
        in_specs=(pl.BlockSpec(memory_space=pltpu.SMEM), _tile(qb, SWA_WIDTH), kp_spec, vp_spec,
                  cur(0), cur(1), _tile(qb, SWA_WIDTH)),
        out_specs=(_tile(qb, SWA_WIDTH),),
        out_shapes=(jax.ShapeDtypeStruct((n_batch, length, SWA_WIDTH), BF16),))


def _outproj_stages(x_ref, o1_ref, o2_ref, p_ref, gple_ref, gfin_ref, *refs, n_blocks):
    wo1, wo2, wg, wp = (refs[j * n_blocks:(j + 1) * n_blocks] for j in range(4))
    y_ref, h_ref = refs[-2:]

    def pieces():
        o1, o2 = o1_ref[...], o2_ref[...]
        for (cols, w1_ref, cs), (_, w2_ref, _) in zip(_block_pieces(wo1), _block_pieces(wo2)):
            h_ref[:, cols] = x_ref[:, cols] + (_dot(o1, w1_ref[:, cs]) + _dot(o2, w2_ref[:, cs]))
            yield
        h = h_ref[...]
        n = ((h * _rms_scale(h)) * gple_ref[...]).astype(BF16)
        p = p_ref[...].astype(BF16)
        for (cols, wg_ref, cs), (_, wp_ref, _) in zip(_block_pieces(wg), _block_pieces(wp)):
            gate = _sigmoid(_dot(n, wg_ref[:, cs]))
            h_ref[:, cols] = h_ref[:, cols] + gate * _dot(p, wp_ref[:, cs])
            yield
        h = h_ref[...]
        y_ref[...] = (h * _rms_scale(h)) * gfin_ref[...]

    return pieces(), []


def _outproj_task(x, o1, o2, p, batch0, tm, wo1, wo2, gple, wg, wp, gfin, y_prev=None):
    _, _, d = x.shape
    assert len(wo1) == len(wo2) == len(wg) == len(wp)
    consts = (gple, gfin, *wo1, *wo2, *wg, *wp)
    operands = (x, o1, o2, p, *consts)
    in_specs = (_tile(tm, d, batch0), _tile(tm, o1.shape[2]), _tile(tm, o2.shape[2]),
                _tile(tm, p.shape[2], batch0), *[_resident(c.shape) for c in consts])
    aliases = None
    if y_prev is not None:
        aliases = {len(operands): 0}
        operands += (y_prev,)
        in_specs += (pl.BlockSpec(memory_space=pl.ANY),)
    return _Task(stages=functools.partial(_outproj_stages, n_blocks=len(wo1)),
                 operands=operands, in_specs=in_specs,
                 out_specs=(_tile(tm, d, batch0),),
                 out_shapes=(jax.ShapeDtypeStruct(x.shape, F32),),
                 scratch=(pltpu.VMEM((tm, d), F32),), aliases=aliases)


def _split_w_in(w_in):
    sizes = (GLA_KW, GLA_KW, GLA_WIDTH, GLA_WIDTH, GLA_RANK, SWA_WIDTH, SWA_KVW, SWA_KVW, SWA_WIDTH)
    offs = [0]
    for s in sizes:
        offs.append(offs[-1] + s)
    col = lambda a, b: w_in[:, offs[a]:offs[b]].astype(BF16)
    q1, k1, v1, z1, lr, q2, kv2, z2 = col(0, 1), col(1, 2), col(2, 3), col(3, 4), col(4, 5), \
        col(5, 6), col(6, 8), col(8, 9)
    lr = jnp.pad(lr, ((0, 0), (0, LANES - GLA_RANK)))
    return tuple(_col_blocks(g) for g in (q1, k1, v1, z1, q2, kv2, z2, lr))


class _Weights(NamedTuple):
    g_mix: Any
    w_in: tuple
    w_gate: Any
    b_gate: Any
    g_gla_out: Any
    sinks: Any
    wo1: Any
    wo2: Any
    g_ple: Any
    w_ple_gate: Any
    w_ple_proj: Any
    g_final: Any


def _attn_tasks(w, proj, s0, state_batch0, past, chunk, tile):
    q1, k1, v1, z1, q2, kv2, z2, lr = proj
    return [_gla_task(q1, k1, v1, z1, lr, w.w_gate, w.b_gate, w.g_gla_out, s0, state_batch0,
                      chunk, tile),
            _swa_task(w.sinks, q2, z2, kv2, past, chunk, tile)]


def _prompt_path(w, x, p, s0):
    n_batch, length, _ = x.shape
    assert n_batch == 2, "a call holding all three stages would not fit the VMEM budget"
    proj, attn = {}, {}
    y = None
    for s in range(n_batch + 2):
        has_in, has_attn, has_out = s < n_batch, 1 <= s <= n_batch, s >= 2
        tile = TILE_FUSED if has_attn else TILE_MATMUL_ONLY
        tasks, names = [], []
        if has_attn:
            tasks += _attn_tasks(w, proj[s - 1], s0, s - 1, None, CHUNK, tile)
            names.append("attn")
        if has_in:
            tasks.append(_inproj_task(x, s, 1, tile, w.g_mix, w.w_in))
            names.append("inproj")
        if has_out:
            o1, o2, _ = attn[s - 2]
            tasks.append(_outproj_task(x, o1, o2, p, s - 2, tile, w.wo1, w.wo2, w.g_ple,
                                       w.w_ple_gate, w.w_ple_proj, w.g_final, y_prev=y))
            names.append("outproj")
        res = _run(tasks, (1, length // tile), "_".join(names))
        if has_attn:
            (o1, s_new), (o2,) = res[0], res[1]
            attn[s - 1] = (o1, o2, s_new)
            res = res[2:]
        if has_in:
            proj[s] = res[0]
            res = res[1:]
        if has_out:
            (y,) = res[0]
    states = jnp.concatenate([attn[b][2] for b in range(n_batch)], axis=0)
    kv = [proj[b][5] for b in range(n_batch)]
    return y, states, kv


def _sample_path(w, x, p, s0, past):
    n_batch, length, d = x.shape
    flat = lambda a: a.reshape(1, n_batch * length, a.shape[-1])
    per_batch = lambda a: a.reshape(n_batch, length, a.shape[-1])
    (proj,) = _run([_inproj_task(flat(x), 0, 1, n_batch * length, w.g_mix, w.w_in)], (1, 1),
                   "inproj_sample")
    proj = tuple(per_batch(a) for a in proj)
    (o1, s_new), (o2,) = _run(_attn_tasks(w, proj, s0, 0, past, min(CHUNK, length), length),
                              (n_batch, 1), "attn_sample")
    ((y,),) = _run([_outproj_task(flat(x), flat(o1), flat(o2), flat(p), 0, n_batch * length,
                                  w.wo1, w.wo2, w.g_ple, w.w_ple_gate, w.w_ple_proj, w.g_final)],
                   (1, 1), "outproj_sample")
    return y.reshape(n_batch, length, d), s_new, proj[5]


def kernel(x_prompt, x_sample, p_prompt, p_sample, state_gla, cache_swa_k, cache_swa_v, g_mix, w_in, w_gate_up, b_gate, g_gla_out, swa_sinks, w_out, g_ple, w_ple_gate, w_ple_proj, g_final):
    depth = w_in.shape[0]
    assert depth == 1, "the final RMSNorm is fused into the last layer's output kernel"
    bp, lp, _ = x_prompt.shape
    bs, ls, _ = x_sample.shape
    assert cache_swa_k.shape[2] == WINDOW
    i = 0
    row = lambda a: a.reshape(1, -1).astype(F32)
    w = _Weights(
        g_mix=row(g_mix[i]), w_in=_split_w_in(w_in[i]),
        w_gate=jnp.pad(w_gate_up[i].astype(BF16), ((0, LANES - GLA_RANK), (0, 0))),
        b_gate=row(b_gate[i]), g_gla_out=row(g_gla_out[i]),
        sinks=swa_sinks[i].reshape(-1).astype(F32),
        wo1=_col_blocks(w_out[i, :GLA_WIDTH].astype(BF16)),
        wo2=_col_blocks(w_out[i, GLA_WIDTH:].astype(BF16)),
        g_ple=row(g_ple[i]), w_ple_gate=_col_blocks(w_ple_gate[i].astype(BF16)),
        w_ple_proj=_col_blocks(w_ple_proj[i].astype(BF16)), g_final=row(g_final))

    s0_p = jnp.zeros((bp, GLA_HEADS, GLA_DK, GLA_DV), F32)
    y_p, s_p, kv_p = _prompt_path(w, x_prompt, p_prompt[i], s0_p)
    keep = min(WINDOW, lp)
    last = lambda col: jnp.concatenate(
        [a[:, lp - keep:, col * SWA_KVW:(col + 1) * SWA_KVW] for a in kv_p], axis=0
    ).reshape(bp, keep, SWA_KV_HEADS, SWA_HEAD_DIM)
    k_p, v_p = last(0), last(1)

    past = (cache_swa_k[i].reshape(bs, WINDOW, SWA_KVW), cache_swa_v[i].reshape(bs, WINDOW, SWA_KVW))
    y_s, s_s, kv_s = _sample_path(w, x_sample, p_sample[i], state_gla[i], past)
    k_s = kv_s[:, :, :SWA_KVW].reshape(bs, ls, SWA_KV_HEADS, SWA_HEAD_DIM)
    v_s = kv_s[:, :, SWA_KVW:].reshape(bs, ls, SWA_KV_HEADS, SWA_HEAD_DIM)
    return (y_p, y_s, s_p[None], k_p[None], v_p[None], s_s[None], k_s[None], v_s[None])
```

```python
import functools
from typing import Any, Callable, NamedTuple

import jax
import jax.numpy as jnp
from jax import lax
from jax.experimental import pallas as pl
from jax.experimental.pallas import tpu as pltpu

F32 = jnp.float32
BF16 = jnp.bfloat16

CHUNK = 64
RMS_EPS = 1e-6
GLA_HEADS = 4
GLA_DK = 128
GLA_DV = 256
GLA_KW = GLA_HEADS * GLA_DK
GLA_WIDTH = GLA_HEADS * GLA_DV
GLA_RANK = 16
GLA_TAU = 16.0
SWA_KV_HEADS = 4
SWA_GROUP = 4
SWA_HEAD_DIM = 64
SWA_WIDTH = SWA_KV_HEADS * SWA_GROUP * SWA_HEAD_DIM
SWA_KVW = SWA_KV_HEADS * SWA_HEAD_DIM
WINDOW = 128

LANES = 128
MXU_WIDTH = 256
WEIGHT_BLOCK = 512
VMEM_LIMIT_BYTES = 56 * 1024 * 1024
TILE_MATMUL_ONLY = 512
TILE_FUSED = 256
OUT_ROW_GROUP = 256
GLA_IN_FLIGHT = 2
SWA_IN_FLIGHT = 4
NORM_HEAD_START = 2

NT_DIMS = (((1,), (1,)), ((), ()))
TN_DIMS = (((0,), (0,)), ((), ()))


def _dot(a, b):
    return jnp.dot(a, b, preferred_element_type=F32)


def _dot_nt(a, b):
    return lax.dot_general(a, b, NT_DIMS, preferred_element_type=F32)


def _dot_tn(a, b):
    return lax.dot_general(a, b, TN_DIMS, preferred_element_type=F32)


def _rms_scale(x):
    return lax.rsqrt(jnp.mean(x * x, axis=-1, keepdims=True) + RMS_EPS)


LOG2E = 1.4426950408889634
LN2 = 0.6931471805599453


def _sigmoid(x):
    return 1.0 / (1.0 + jnp.exp2(x * -LOG2E))


def _silu(x):
    return x * _sigmoid(x)


def _resident(shape):
    zeros = (0,) * len(shape)
    return pl.BlockSpec(shape, lambda *_: zeros, pipeline_mode=pl.Buffered(1))


def _tile(tm, width, batch0=0):
    return pl.BlockSpec((None, tm, width), lambda b, i: (batch0 + b, i, 0))


class _Block(NamedTuple):
    array: Any
    shape: tuple
    index: tuple

    def spec(self):
        index = self.index
        return pl.BlockSpec(self.shape, lambda *_: index, pipeline_mode=pl.Buffered(1))


def _col_blocks(w, col0=0, width=None, row_block=0, n_rows=None):
    n_rows = n_rows or w.shape[0]
    width = width or w.shape[1] - col0
    blk = min(WEIGHT_BLOCK, width)
    assert col0 % blk == 0 and width % blk == 0
    return tuple(_Block(w, (n_rows, blk), (row_block, (col0 + c) // blk))
                 for c in range(0, width, blk))


def _block_pieces(block_refs):
    out, base = [], 0
    for ref in block_refs:
        width = ref.shape[1]
        for c in range(0, width, MXU_WIDTH):
            hi = min(c + MXU_WIDTH, width)
            out.append((slice(base + c, base + hi), ref, slice(c, hi)))
        base += width
    return out


class _Task(NamedTuple):
    stages: Callable[..., Any]
    operands: tuple
    in_specs: tuple
    out_specs: tuple
    out_shapes: tuple
    scratch: tuple = ()
    aliases: Any = None


_DONE = object()


class _Stream:
    def __init__(self, units, max_active):
        self.pending, self.active, self.max_active = list(units), [], max_active

    def busy(self):
        return bool(self.pending or self.active)

    def step(self):
        self.active = [g for g in self.active if next(g, _DONE) is not _DONE]
        if self.pending and len(self.active) < self.max_active:
            g = self.pending.pop(0)
            if next(g, _DONE) is not _DONE:
                self.active.append(g)


def _drive(heavy, light):
    heavy = [g for g in heavy if g is not None]
    while heavy or any(s.busy() for s in light):
        heavy = [g for g in heavy if next(g, _DONE) is not _DONE]
        for s in light:
            s.step()


def _run(tasks, grid, name):
    n_in = [len(t.operands) for t in tasks]
    n_out = [len(t.out_shapes) for t in tasks]
    n_scr = [len(t.scratch) for t in tasks]

    def body(*refs):
        ins = refs[:sum(n_in)]
        outs = refs[sum(n_in):sum(n_in) + sum(n_out)]
        scrs = refs[sum(n_in) + sum(n_out):]
        heavy, light = [], []
        i = o = s = 0
        for t, ni, no, ns in zip(tasks, n_in, n_out, n_scr):
            h, l = t.stages(*ins[i:i + ni], *outs[o:o + no], *scrs[s:s + ns])
            heavy.append(h)
            light += l
            i, o, s = i + ni, o + no, s + ns
        _drive(heavy, light)

    aliases = {}
    i = o = 0
    for t, ni, no in zip(tasks, n_in, n_out):
        for src, dst in (t.aliases or {}).items():
            aliases[i + src] = o + dst
        i, o = i + ni, o + no

    flat = pl.pallas_call(
        body,
        grid=grid,
        in_specs=[s for t in tasks for s in t.in_specs],
        out_specs=tuple(s for t in tasks for s in t.out_specs),
        out_shape=tuple(s for t in tasks for s in t.out_shapes),
        scratch_shapes=[s for t in tasks for s in t.scratch],
        input_output_aliases=aliases,
        compiler_params=pltpu.CompilerParams(
            dimension_semantics=("arbitrary", "arbitrary"), vmem_limit_bytes=VMEM_LIMIT_BYTES),
        name=name,
    )(*[a for t in tasks for a in t.operands])
    results, o = [], 0
    for no in n_out:
        results.append(tuple(flat[o:o + no]))
        o += no
    return results


def _inproj_stages(x_ref, g_ref, *refs, blocks_per_group):
    n_blocks = sum(blocks_per_group)
    w_refs, o_refs = refs[:n_blocks], refs[n_blocks:]

    def pieces():
        x = x_ref[...]
        n = ((x * _rms_scale(x)) * g_ref[...]).astype(BF16)
        for _ in range(NORM_HEAD_START):
            yield
        first = 0
        for n_blk, o_ref, is_gate in zip(blocks_per_group, o_refs, IN_IS_GATE):
            for cols, w_ref, cs in _block_pieces(w_refs[first:first + n_blk]):
                u = _dot(n, w_ref[:, cs])
                o_ref[:, cols] = (_silu(u) if is_gate else u).astype(o_ref.dtype)
                yield
            first += n_blk

    return pieces(), []


IN_DTYPES = (BF16, BF16, BF16, BF16, BF16, F32, BF16, BF16)
IN_IS_GATE = (False, False, False, True, False, False, True, False)


def _inproj_task(x, batch0, n_batch, tm, g_mix, groups):
    _, length, d = x.shape
    blocks = [w for g in groups for w in g]
    widths = [sum(w.shape[1] for w in g) for g in groups]
    return _Task(
        stages=functools.partial(_inproj_stages, blocks_per_group=tuple(len(g) for g in groups)),
        operands=(x, g_mix, *[w.array for w in blocks]),
        in_specs=(_tile(tm, d, batch0), _resident(g_mix.shape), *[w.spec() for w in blocks]),
        out_specs=tuple(_tile(tm, n) for n in widths),
        out_shapes=tuple(jax.ShapeDtypeStruct((n_batch, length, n), dt)
                         for n, dt in zip(widths, IN_DTYPES)))


def _gla_stages(q_ref, k_ref, v_ref, z_ref, lr_ref, wg_ref, bg_ref, gout_ref, s0_ref,
                o_ref, s_ref, *, chunk, n_chunks):
    @pl.when(pl.program_id(1) == 0)
    def _():
        s_ref[...] = s0_ref[...]

    row = lax.broadcasted_iota(jnp.int32, (chunk, chunk), 0)
    col = lax.broadcasted_iota(jnp.int32, (chunk, chunk), 1)
    causal = row >= col
    row2 = lax.broadcasted_iota(jnp.int32, (chunk, 2 * chunk), 0)
    col2 = lax.broadcasted_iota(jnp.int32, (chunk, 2 * chunk), 1)
    tril2 = jnp.where(row2 >= lax.rem(col2, chunk), 1.0, 0.0).astype(BF16)
    eye = (lax.broadcasted_iota(jnp.int32, (GLA_DK, GLA_DK), 0)
           == lax.broadcasted_iota(jnp.int32, (GLA_DK, GLA_DK), 1))
    scale = GLA_DK ** -0.5
    heads = [(slice(h * GLA_DK, (h + 1) * GLA_DK), slice(h * GLA_DV, (h + 1) * GLA_DV))
             for h in range(GLA_HEADS)]
    transpose_ahead = True

    def chunk_unit(c):
        r = slice(c * chunk, (c + 1) * chunk)
        gpre = _dot(lr_ref[r, :], wg_ref[...])
        yield
        gpre = gpre + bg_ref[...]
        soft = jnp.log2(1.0 + jnp.exp2(jnp.abs(gpre) * -LOG2E))
        g = jnp.minimum(gpre, 0.0) * (1.0 / GLA_TAU) - soft * (LN2 / GLA_TAU)
        g_hi = g.astype(BF16)
        g_lo = (g - g_hi.astype(F32)).astype(BF16)
        g2 = jnp.concatenate([g_hi, g_lo], axis=0)
        yield
        b = _dot(tril2, g2)
        yield
        b_last = b[chunk - 1:chunk, :]
        q = q_ref[r, :].astype(F32)
        k = k_ref[r, :].astype(F32)
        qe = ((q * scale) * jnp.exp(b)).astype(BF16)
        ke = (k * jnp.exp(-b)).astype(BF16)
        kd = k * jnp.exp(b_last - b)
        decay = jnp.exp(b_last)
        v = [v_ref[r, vs] for _, vs in heads]
        if transpose_ahead:
            kd_t = [kd[:, ks].T.astype(BF16) for ks, _ in heads]
        else:
            kd = kd.astype(BF16)
        yield
        a_raw = [_dot_nt(qe[:, ks], ke[:, ks]) for ks, _ in heads]
        if transpose_ahead:
            kv = [_dot(kd_t[h], v[h]) for h in range(GLA_HEADS)]
        else:
            kv = [_dot_tn(kd[:, ks], v[h]) for h, (ks, _) in enumerate(heads)]
        yield
        lhs, rhs = [], []
        for h, (ks, _) in enumerate(heads):
            a = jnp.where(causal, a_raw[h], 0.0).astype(BF16)
            s = s_ref[h]
            lhs.append(jnp.concatenate([qe[:, ks], a], axis=1))
            rhs.append(jnp.concatenate([s.astype(BF16), v[h]], axis=0))
            dcol = jnp.sum(jnp.where(eye, decay[:, ks], 0.0), axis=1, keepdims=True)
            s_ref[h] = dcol * s + kv[h]
        yield
        o = [_dot(lhs[h], rhs[h]) for h in range(GLA_HEADS)]
        yield
        for h, (_, vs) in enumerate(heads):
            oh = (o[h] * _rms_scale(o[h])) * gout_ref[...]
            o_ref[r, vs] = (oh * z_ref[r, vs].astype(F32)).astype(o_ref.dtype)

    return None, [_Stream([chunk_unit(c) for c in range(n_chunks)], GLA_IN_FLIGHT)]


def _gla_task(q, k, v, z, lr, wg, bg, gout, s0, state_batch0, chunk, tb):
    n_batch, length, _ = q.shape
    st_in = pl.BlockSpec((None, GLA_HEADS, GLA_DK, GLA_DV),
                         lambda b, i: (state_batch0 + b, 0, 0, 0))
    st_out = pl.BlockSpec((None, GLA_HEADS, GLA_DK, GLA_DV), lambda b, i: (b, 0, 0, 0))
    return _Task(
        stages=functools.partial(_gla_stages, chunk=chunk, n_chunks=tb // chunk),
        operands=(q, k, v, z, lr, wg, bg, gout, s0),
        in_specs=(_tile(tb, GLA_KW), _tile(tb, GLA_KW), _tile(tb, GLA_WIDTH), _tile(tb, GLA_WIDTH),
                  _tile(tb, LANES), _resident(wg.shape), _resident(bg.shape),
                  _resident(gout.shape), st_in),
        out_specs=(_tile(tb, GLA_WIDTH), st_out),
        out_shapes=(jax.ShapeDtypeStruct((n_batch, length, GLA_WIDTH), BF16),
                    jax.ShapeDtypeStruct((n_batch, GLA_HEADS, GLA_DK, GLA_DV), F32)))


def _swa_stages(sink_ref, q_ref, kp_ref, vp_ref, kc_ref, vc_ref, z_ref, o_ref,
                *, chunk, n_chunks, mask_missing_past):
    kvw = SWA_GROUP * SWA_HEAD_DIM
    n_keys = WINDOW + chunk
    gq = SWA_GROUP * chunk
    lane_group = lax.broadcasted_iota(jnp.int32, (chunk, kvw), 1) // SWA_HEAD_DIM
    gq_group = lax.broadcasted_iota(jnp.int32, (1, gq), 1) // chunk
    key_i = lax.broadcasted_iota(jnp.int32, (n_keys, gq), 0)
    qscale = SWA_HEAD_DIM ** -0.5
    first_pos = pl.program_id(1) * (chunk * n_chunks) - WINDOW
    shared = {}

    def head_operands(h):
        if not shared:
            shared["k"] = jnp.concatenate([kp_ref[...], kc_ref[...]], axis=0).astype(BF16)
            v = jnp.concatenate([vp_ref[...], vc_ref[...]], axis=0)
            shared["vt"] = v.T.astype(BF16)
        if h not in shared:
            hd = slice(h * SWA_HEAD_DIM, (h + 1) * SWA_HEAD_DIM)
            k_rep = jnp.concatenate([shared["k"][:, hd]] * SWA_GROUP, axis=1)
            vt = shared["vt"][hd, :]
            vt_at = {}
            for c in range(n_chunks):
                shift = (c * chunk) % LANES
                if shift not in vt_at:
                    vt_at[shift] = vt[:, shift:]
            sink = jnp.zeros((1, gq), F32)
            for g in range(SWA_GROUP):
                sink = jnp.where(gq_group == g, sink_ref[h * SWA_GROUP + g], sink)
            shared[h] = (k_rep, vt_at, sink)
        return shared[h]

    def unit(h, c):
        k_rep, vt_at, sink = head_operands(h)
        rows = slice(c * chunk, (c + 1) * chunk)
        keys = slice(c * chunk, c * chunk + n_keys)
        cols = slice(h * kvw, (h + 1) * kvw)
        q = q_ref[rows, cols]
        qs = jnp.concatenate(
            [jnp.where(lane_group == g, q, jnp.zeros_like(q)) for g in range(SWA_GROUP)],
            axis=0)
        shift = (c * chunk) % LANES
        vt_keys = vt_at[shift][:, c * chunk - shift:c * chunk - shift + n_keys]
        yield
        s = _dot_nt(k_rep[keys], qs)
        yield
        if mask_missing_past and c * chunk < WINDOW:
            s = jnp.where(key_i + (first_pos + c * chunk) >= 0, s, jnp.finfo(F32).min)
        m = jnp.maximum(jnp.max(s, axis=0, keepdims=True) * qscale, sink) * LOG2E
        p = jnp.exp2(s * (qscale * LOG2E) - m)
        denom = jnp.sum(p, axis=0, keepdims=True) + jnp.exp2(sink * LOG2E - m)
        p = p.astype(BF16)
        yield
        o_t = _dot(vt_keys, p)
        yield
        o_t = (o_t / denom).T
        o = jnp.concatenate([o_t[g * chunk:(g + 1) * chunk] for g in range(SWA_GROUP)],
                            axis=1)
        o_ref[rows, cols] = (o * z_ref[rows, cols].astype(F32)).astype(o_ref.dtype)

    units = [unit(h, c) for h in range(SWA_KV_HEADS) for c in range(n_chunks)]
    return None, [_Stream(units, SWA_IN_FLIGHT)]


def _swa_task(sinks, q, z, kv, past, chunk, qb):
    n_batch, length, _ = q.shape
    if past is None:
        per_win = qb // WINDOW
        prev = lambda col: pl.BlockSpec(
            (None, WINDOW, SWA_KVW), lambda b, i: (b, jnp.maximum(i * per_win - 1, 0), col))
        k_past, v_past, kp_spec, vp_spec = kv, kv, prev(0), prev(1)
    else:
        k_past, v_past = past
        kp_spec = vp_spec = pl.BlockSpec((None, WINDOW, SWA_KVW), lambda b, i: (b, 0, 0))
    cur = lambda col: pl.BlockSpec((None, qb, SWA_KVW), lambda b, i: (b, i, col))
    return _Task(
        stages=functools.partial(_swa_stages, chunk=chunk, n_chunks=qb // chunk,
                                 mask_missing_past=past is None),
        operands=(sinks, q, k_past, v_past, kv, kv, z),
        in_specs=(pl.BlockSpec(memory_space=pltpu.SMEM), _tile(qb, SWA_WIDTH), kp_spec, vp_spec,
                  cur(0), cur(1), _tile(qb, SWA_WIDTH)),
        out_specs=(_tile(qb, SWA_WIDTH),),
        out_shapes=(jax.ShapeDtypeStruct((n_batch, length, SWA_WIDTH), BF16),))


def _outproj_stages(x_ref, o1_ref, o2_ref, p_ref, gple_ref, gfin_ref, *refs, n_blocks):
    wo1, wo2, wg, wp = (refs[j * n_blocks:(j + 1) * n_blocks] for j in range(4))
    y_ref, h_ref = refs[-2:]

    def pieces():
        o1, o2 = o1_ref[...], o2_ref[...]
        n = []
        for (cols, w1_ref, cs), (_, w2_ref, _) in zip(_block_pieces(wo1), _block_pieces(wo2)):
            h = x_ref[:, cols] + (_dot(o1, w1_ref[:, cs]) + _dot(o2, w2_ref[:, cs]))
            h_ref[:, cols] = h
            n.append((h * gple_ref[:, cols]).astype(BF16))
            yield
        n = jnp.concatenate(n, axis=1)
        r = _rms_scale(h_ref[...])
        p = p_ref[...].astype(BF16)
        tm = h_ref.shape[0]
        for rows in [slice(r0, r0 + OUT_ROW_GROUP) for r0 in range(0, tm, OUT_ROW_GROUP)]:
            for (cols, wg_ref, cs), (_, wp_ref, _) in zip(_block_pieces(wg), _block_pieces(wp)):
                gate = _sigmoid(_dot(n[rows], wg_ref[:, cs]) * r[rows])
                h_ref[rows, cols] = h_ref[rows, cols] + gate * _dot(p[rows], wp_ref[:, cs])
                yield
            h = h_ref[rows, :]
            y_ref[rows, :] = (h * _rms_scale(h)) * gfin_ref[...]

    return pieces(), []


def _outproj_task(x, o1, o2, p, batch0, tm, wo1, wo2, gple, wg, wp, gfin, y_prev=None):
    _, _, d = x.shape
    assert len(wo1) == len(wo2) == len(wg) == len(wp)
    blocks = (*wo1, *wo2, *wg, *wp)
    operands = (x, o1, o2, p, gple, gfin, *[w.array for w in blocks])
    in_specs = (_tile(tm, d, batch0), _tile(tm, o1.shape[2]), _tile(tm, o2.shape[2]),
                _tile(tm, p.shape[2], batch0), _resident(gple.shape), _resident(gfin.shape),
                *[w.spec() for w in blocks])
    aliases = None
    if y_prev is not None:
        aliases = {len(operands): 0}
        operands += (y_prev,)
        in_specs += (pl.BlockSpec(memory_space=pl.ANY),)
    return _Task(stages=functools.partial(_outproj_stages, n_blocks=len(wo1)),
                 operands=operands, in_specs=in_specs,
                 out_specs=(_tile(tm, d, batch0),),
                 out_shapes=(jax.ShapeDtypeStruct(x.shape, F32),),
                 scratch=(pltpu.VMEM((tm, d), F32),), aliases=aliases)


def _split_w_in(w_in):
    lr0 = 2 * GLA_KW + 2 * GLA_WIDTH
    w = jnp.concatenate(
        [w_in[:, :lr0], w_in[:, lr0 + GLA_RANK:], w_in[:, lr0:lr0 + GLA_RANK],
         jnp.zeros((w_in.shape[0], LANES - GLA_RANK), w_in.dtype)], axis=1).astype(BF16)
    widths = (GLA_KW, GLA_KW, GLA_WIDTH, GLA_WIDTH, SWA_WIDTH, 2 * SWA_KVW, SWA_WIDTH, LANES)
    groups, col0 = [], 0
    for width in widths:
        groups.append(_col_blocks(w, col0, width))
        col0 += width
    return tuple(groups)


class _Weights(NamedTuple):
    g_mix: Any
    w_in: tuple
    w_gate: Any
    b_gate: Any
    g_gla_out: Any
    sinks: Any
    wo1: Any
    wo2: Any
    g_ple: Any
    w_ple_gate: Any
    w_ple_proj: Any
    g_final: Any


def _attn_tasks(w, proj, s0, state_batch0, past, chunk, tile):
    q1, k1, v1, z1, q2, kv2, z2, lr = proj
    return [_gla_task(q1, k1, v1, z1, lr, w.w_gate, w.b_gate, w.g_gla_out, s0, state_batch0,
                      chunk, tile),
            _swa_task(w.sinks, q2, z2, kv2, past, chunk, tile)]


def _prompt_path(w, x, p, s0):
    n_batch, length, _ = x.shape
    assert n_batch == 2, "a call holding all three stages would not fit the VMEM budget"
    proj, attn = {}, {}
    y = None
    for s in range(n_batch + 2):
        has_in, has_attn, has_out = s < n_batch, 1 <= s <= n_batch, s >= 2
        tile = TILE_FUSED if has_attn else TILE_MATMUL_ONLY
        tasks, names = [], []
        if has_attn:
            tasks += _attn_tasks(w, proj[s - 1], s0, s - 1, None, CHUNK, tile)
            names.append("attn")
        if has_in:
            tasks.append(_inproj_task(x, s, 1, tile, w.g_mix, w.w_in))
            names.append("inproj")
        if has_out:
            o1, o2, _ = attn[s - 2]
            tasks.append(_outproj_task(x, o1, o2, p, s - 2, tile, w.wo1, w.wo2, w.g_ple,
                                       w.w_ple_gate, w.w_ple_proj, w.g_final, y_prev=y))
            names.append("outproj")
        res = _run(tasks, (1, length // tile), "_".join(names))
        if has_attn:
            (o1, s_new), (o2,) = res[0], res[1]
            attn[s - 1] = (o1, o2, s_new)
            res = res[2:]
        if has_in:
            proj[s] = res[0]
            res = res[1:]
        if has_out:
            (y,) = res[0]
    states = jnp.concatenate([attn[b][2] for b in range(n_batch)], axis=0)
    kv = [proj[b][5] for b in range(n_batch)]
    return y, states, kv


def _sample_path(w, x, p, s0, past):
    n_batch, length, d = x.shape
    flat = lambda a: a.reshape(1, n_batch * length, a.shape[-1])
    per_batch = lambda a: a.reshape(n_batch, length, a.shape[-1])
    (proj,) = _run([_inproj_task(flat(x), 0, 1, n_batch * length, w.g_mix, w.w_in)], (1, 1),
                   "inproj_sample")
    proj = tuple(per_batch(a) for a in proj)
    (o1, s_new), (o2,) = _run(_attn_tasks(w, proj, s0, 0, past, min(CHUNK, length), length),
                              (n_batch, 1), "attn_sample")
    ((y,),) = _run([_outproj_task(flat(x), flat(o1), flat(o2), flat(p), 0, n_batch * length,
                                  w.wo1, w.wo2, w.g_ple, w.w_ple_gate, w.w_ple_proj, w.g_final)],
                   (1, 1), "outproj_sample")
    return y.reshape(n_batch, length, d), s_new, proj[5]


def kernel(x_prompt, x_sample, p_prompt, p_sample, state_gla, cache_swa_k, cache_swa_v, g_mix, w_in, w_gate_up, b_gate, g_gla_out, swa_sinks, w_out, g_ple, w_ple_gate, w_ple_proj, g_final):
    depth = w_in.shape[0]
    assert depth == 1, "the final RMSNorm is fused into the last layer's output kernel"
    bp, lp, _ = x_prompt.shape
    bs, ls, _ = x_sample.shape
    assert cache_swa_k.shape[2] == WINDOW
    i = 0
    row = lambda a: a.reshape(1, -1).astype(F32)
    assert GLA_WIDTH == SWA_WIDTH
    w_out_bf16 = w_out[i].astype(BF16)
    w = _Weights(
        g_mix=row(g_mix[i]), w_in=_split_w_in(w_in[i]),
        w_gate=jnp.pad(w_gate_up[i].astype(BF16), ((0, LANES - GLA_RANK), (0, 0))),
        b_gate=row(b_gate[i]), g_gla_out=row(g_gla_out[i]),
        sinks=swa_sinks[i].reshape(-1).astype(F32),
        wo1=_col_blocks(w_out_bf16, row_block=0, n_rows=GLA_WIDTH),
        wo2=_col_blocks(w_out_bf16, row_block=1, n_rows=SWA_WIDTH),
        g_ple=row(g_ple[i]), w_ple_gate=_col_blocks(w_ple_gate[i].astype(BF16)),
        w_ple_proj=_col_blocks(w_ple_proj[i].astype(BF16)), g_final=row(g_final))

    s0_p = jnp.zeros((bp, GLA_HEADS, GLA_DK, GLA_DV), F32)
    y_p, s_p, kv_p = _prompt_path(w, x_prompt, p_prompt[i], s0_p)
    keep = min(WINDOW, lp)
    last = lambda col: jnp.concatenate(
        [a[:, lp - keep:, col * SWA_KVW:(col + 1) * SWA_KVW] for a in kv_p], axis=0
    ).reshape(bp, keep, SWA_KV_HEADS, SWA_HEAD_DIM)
    k_p, v_p = last(0), last(1)

    past = (cache_swa_k[i].reshape(bs, WINDOW, SWA_KVW), cache_swa_v[i].reshape(bs, WINDOW, SWA_KVW))
    y_s, s_s, kv_s = _sample_path(w, x_sample, p_sample[i], state_gla[i], past)
    k_s = kv_s[:, :, :SWA_KVW].reshape(bs, ls, SWA_KV_HEADS, SWA_HEAD_DIM)
    v_s = kv_s[:, :, SWA_KVW:].reshape(bs, ls, SWA_KV_HEADS, SWA_HEAD_DIM)
    return (y_p, y_s, s_p[None], k_p[None], v_p[None], s_s[None], k_s[None], v_s[None])
```

```python
import functools
from typing import Any, Callable, NamedTuple

import jax
import jax.numpy as jnp
from jax import lax
from jax.experimental import pallas as pl
from jax.experimental.pallas import tpu as pltpu

F32 = jnp.float32
BF16 = jnp.bfloat16

CHUNK = 64
RMS_EPS = 1e-6
GLA_HEADS = 4
GLA_DK = 128
GLA_DV = 256
GLA_KW = GLA_HEADS * GLA_DK
GLA_WIDTH = GLA_HEADS * GLA_DV
GLA_RANK = 16
GLA_TAU = 16.0
SWA_KV_HEADS = 4
SWA_GROUP = 4
SWA_HEAD_DIM = 64
SWA_WIDTH = SWA_KV_HEADS * SWA_GROUP * SWA_HEAD_DIM
SWA_KVW = SWA_KV_HEADS * SWA_HEAD_DIM
WINDOW = 128

LANES = 128
MXU_WIDTH = 256
WEIGHT_BLOCK = 512
VMEM_LIMIT_BYTES = 56 * 1024 * 1024
TILE_MATMUL_ONLY = 512
TILE_FUSED = 256
RELAYOUT_ROWS = 256
OUT_ROW_GROUP = 256
GLA_IN_FLIGHT = 2
SWA_IN_FLIGHT = 4
NORM_HEAD_START = 2

NT_DIMS = (((1,), (1,)), ((), ()))
TN_DIMS = (((0,), (0,)), ((), ()))


def _dot(a, b):
    return jnp.dot(a, b, preferred_element_type=F32)


def _dot_nt(a, b):
    return lax.dot_general(a, b, NT_DIMS, preferred_element_type=F32)


def _dot_tn(a, b):
    return lax.dot_general(a, b, TN_DIMS, preferred_element_type=F32)


def _rms_scale(x):
    return lax.rsqrt(jnp.mean(x * x, axis=-1, keepdims=True) + RMS_EPS)


LOG2E = 1.4426950408889634
LN2 = 0.6931471805599453


def _sigmoid(x):
    return 1.0 / (1.0 + jnp.exp2(x * -LOG2E))


def _silu(x):
    return x * _sigmoid(x)


def _resident(shape):
    zeros = (0,) * len(shape)
    return pl.BlockSpec(shape, lambda *_: zeros, pipeline_mode=pl.Buffered(1))


def _tile(tm, width, batch0=0):
    return pl.BlockSpec((None, tm, width), lambda b, i: (batch0 + b, i, 0))


class _Block(NamedTuple):
    array: Any
    shape: tuple
    index: tuple

    def spec(self):
        index = self.index
        return pl.BlockSpec(self.shape, lambda *_: index, pipeline_mode=pl.Buffered(1))


def _col_blocks(w, col0=0, width=None, row_block=0, n_rows=None):
    n_rows = n_rows or w.shape[0]
    width = width or w.shape[1] - col0
    blk = min(WEIGHT_BLOCK, width)
    assert col0 % blk == 0 and width % blk == 0
    return tuple(_Block(w, (n_rows, blk), (row_block, (col0 + c) // blk))
                 for c in range(0, width, blk))


def _block_pieces(block_refs):
    out, base = [], 0
    for ref in block_refs:
        width = ref.shape[1]
        for c in range(0, width, MXU_WIDTH):
            hi = min(c + MXU_WIDTH, width)
            out.append((slice(base + c, base + hi), ref, slice(c, hi)))
        base += width
    return out


class _Task(NamedTuple):
    stages: Callable[..., Any]
    operands: tuple
    in_specs: tuple
    out_specs: tuple
    out_shapes: tuple
    scratch: tuple = ()
    aliases: Any = None


_DONE = object()


class _Stream:
    def __init__(self, units, max_active):
        self.pending, self.active, self.max_active = list(units), [], max_active

    def busy(self):
        return bool(self.pending or self.active)

    def step(self):
        self.active = [g for g in self.active if next(g, _DONE) is not _DONE]
        if self.pending and len(self.active) < self.max_active:
            g = self.pending.pop(0)
            if next(g, _DONE) is not _DONE:
                self.active.append(g)


def _drive(heavy, light):
    heavy = [g for g in heavy if g is not None]
    while heavy or any(s.busy() for s in light):
        heavy = [g for g in heavy if next(g, _DONE) is not _DONE]
        for s in light:
            s.step()


def _run(tasks, grid, name):
    n_in = [len(t.operands) for t in tasks]
    n_out = [len(t.out_shapes) for t in tasks]
    n_scr = [len(t.scratch) for t in tasks]

    def body(*refs):
        ins = refs[:sum(n_in)]
        outs = refs[sum(n_in):sum(n_in) + sum(n_out)]
        scrs = refs[sum(n_in) + sum(n_out):]
        heavy, light = [], []
        i = o = s = 0
        for t, ni, no, ns in zip(tasks, n_in, n_out, n_scr):
            h, l = t.stages(*ins[i:i + ni], *outs[o:o + no], *scrs[s:s + ns])
            heavy.append(h)
            light += l
            i, o, s = i + ni, o + no, s + ns
        _drive(heavy, light)

    aliases = {}
    i = o = 0
    for t, ni, no in zip(tasks, n_in, n_out):
        for src, dst in (t.aliases or {}).items():
            aliases[i + src] = o + dst
        i, o = i + ni, o + no

    flat = pl.pallas_call(
        body,
        grid=grid,
        in_specs=[s for t in tasks for s in t.in_specs],
        out_specs=tuple(s for t in tasks for s in t.out_specs),
        out_shape=tuple(s for t in tasks for s in t.out_shapes),
        scratch_shapes=[s for t in tasks for s in t.scratch],
        input_output_aliases=aliases,
        compiler_params=pltpu.CompilerParams(
            dimension_semantics=("arbitrary", "arbitrary"), vmem_limit_bytes=VMEM_LIMIT_BYTES),
        name=name,
    )(*[a for t in tasks for a in t.operands])
    results, o = [], 0
    for no in n_out:
        results.append(tuple(flat[o:o + no]))
        o += no
    return results


def _inproj_stages(x_ref, g_ref, *refs, blocks_per_group):
    n_blocks = sum(blocks_per_group)
    w_refs, o_refs = refs[:n_blocks], refs[n_blocks:]

    def pieces():
        x = x_ref[...]
        n = ((x * _rms_scale(x)) * g_ref[...]).astype(BF16)
        for _ in range(NORM_HEAD_START):
            yield
        first = 0
        for n_blk, o_ref, is_gate in zip(blocks_per_group, o_refs, IN_IS_GATE):
            for cols, w_ref, cs in _block_pieces(w_refs[first:first + n_blk]):
                u = _dot(n, w_ref[:, cs])
                o_ref[:, cols] = (_silu(u) if is_gate else u).astype(o_ref.dtype)
                yield
            first += n_blk

    return pieces(), []


IN_DTYPES = (BF16, BF16, BF16, BF16, BF16, F32, BF16, BF16)
IN_IS_GATE = (False, False, False, True, False, False, True, False)


def _inproj_task(x, batch0, n_batch, tm, g_mix, groups):
    _, length, d = x.shape
    blocks = [w for g in groups for w in g]
    widths = [sum(w.shape[1] for w in g) for g in groups]
    return _Task(
        stages=functools.partial(_inproj_stages, blocks_per_group=tuple(len(g) for g in groups)),
        operands=(x, g_mix, *[w.array for w in blocks]),
        in_specs=(_tile(tm, d, batch0), _resident(g_mix.shape), *[w.spec() for w in blocks]),
        out_specs=tuple(_tile(tm, n) for n in widths),
        out_shapes=tuple(jax.ShapeDtypeStruct((n_batch, length, n), dt)
                         for n, dt in zip(widths, IN_DTYPES)))


def _gla_stages(q_ref, k_ref, v_ref, z_ref, lr_ref, wg_ref, bg_ref, gout_ref, s0_ref,
                o_ref, s_ref, *, chunk, n_chunks):
    @pl.when(pl.program_id(1) == 0)
    def _():
        s_ref[...] = s0_ref[...]

    row = lax.broadcasted_iota(jnp.int32, (chunk, chunk), 0)
    col = lax.broadcasted_iota(jnp.int32, (chunk, chunk), 1)
    causal = row >= col
    row2 = lax.broadcasted_iota(jnp.int32, (chunk, 2 * chunk), 0)
    col2 = lax.broadcasted_iota(jnp.int32, (chunk, 2 * chunk), 1)
    tril2 = jnp.where(row2 >= lax.rem(col2, chunk), 1.0, 0.0).astype(BF16)
    eye = (lax.broadcasted_iota(jnp.int32, (GLA_DK, GLA_DK), 0)
           == lax.broadcasted_iota(jnp.int32, (GLA_DK, GLA_DK), 1))
    scale = GLA_DK ** -0.5
    heads = [(slice(h * GLA_DK, (h + 1) * GLA_DK), slice(h * GLA_DV, (h + 1) * GLA_DV))
             for h in range(GLA_HEADS)]
    transpose_ahead = True

    def chunk_unit(c):
        r = slice(c * chunk, (c + 1) * chunk)
        gpre = _dot(lr_ref[r, :], wg_ref[...])
        yield
        gpre = gpre + bg_ref[...]
        soft = jnp.log2(1.0 + jnp.exp2(jnp.abs(gpre) * -LOG2E))
        g = jnp.minimum(gpre, 0.0) * (1.0 / GLA_TAU) - soft * (LN2 / GLA_TAU)
        g_hi = g.astype(BF16)
        g_lo = (g - g_hi.astype(F32)).astype(BF16)
        g2 = jnp.concatenate([g_hi, g_lo], axis=0)
        yield
        b = _dot(tril2, g2)
        yield
        b_last = b[chunk - 1:chunk, :]
        q = q_ref[r, :].astype(F32)
        k = k_ref[r, :].astype(F32)
        qe = ((q * scale) * jnp.exp(b)).astype(BF16)
        ke = (k * jnp.exp(-b)).astype(BF16)
        kd = k * jnp.exp(b_last - b)
        decay = jnp.exp(b_last)
        v = [v_ref[r, vs] for _, vs in heads]
        if transpose_ahead:
            kd_t = [kd[:, ks].T.astype(BF16) for ks, _ in heads]
        else:
            kd = kd.astype(BF16)
        yield
        a_raw = [_dot_nt(qe[:, ks], ke[:, ks]) for ks, _ in heads]
        if transpose_ahead:
            kv = [_dot(kd_t[h], v[h]) for h in range(GLA_HEADS)]
        else:
            kv = [_dot_tn(kd[:, ks], v[h]) for h, (ks, _) in enumerate(heads)]
        yield
        lhs, rhs = [], []
        for h, (ks, _) in enumerate(heads):
            a = jnp.where(causal, a_raw[h], 0.0).astype(BF16)
            s = s_ref[h]
            lhs.append(jnp.concatenate([qe[:, ks], a], axis=1))
            rhs.append(jnp.concatenate([s.astype(BF16), v[h]], axis=0))
            dcol = jnp.sum(jnp.where(eye, decay[:, ks], 0.0), axis=1, keepdims=True)
            s_ref[h] = dcol * s + kv[h]
        yield
        o = [_dot(lhs[h], rhs[h]) for h in range(GLA_HEADS)]
        yield
        for h, (_, vs) in enumerate(heads):
            oh = (o[h] * _rms_scale(o[h])) * gout_ref[...]
            o_ref[r, vs] = (oh * z_ref[r, vs].astype(F32)).astype(o_ref.dtype)

    return None, [_Stream([chunk_unit(c) for c in range(n_chunks)], GLA_IN_FLIGHT)]


def _gla_task(q, k, v, z, lr, wg, bg, gout, s0, state_batch0, chunk, tb):
    n_batch, length, _ = q.shape
    st_in = pl.BlockSpec((None, GLA_HEADS, GLA_DK, GLA_DV),
                         lambda b, i: (state_batch0 + b, 0, 0, 0))
    st_out = pl.BlockSpec((None, GLA_HEADS, GLA_DK, GLA_DV), lambda b, i: (b, 0, 0, 0))
    return _Task(
        stages=functools.partial(_gla_stages, chunk=chunk, n_chunks=tb // chunk),
        operands=(q, k, v, z, lr, wg, bg, gout, s0),
        in_specs=(_tile(tb, GLA_KW), _tile(tb, GLA_KW), _tile(tb, GLA_WIDTH), _tile(tb, GLA_WIDTH),
                  _tile(tb, LANES), _resident(wg.shape), _resident(bg.shape),
                  _resident(gout.shape), st_in),
        out_specs=(_tile(tb, GLA_WIDTH), st_out),
        out_shapes=(jax.ShapeDtypeStruct((n_batch, length, GLA_WIDTH), BF16),
                    jax.ShapeDtypeStruct((n_batch, GLA_HEADS, GLA_DK, GLA_DV), F32)))


def _swa_stages(sink_ref, q_ref, kp_ref, vp_ref, kc_ref, vc_ref, z_ref, o_ref,
                *, chunk, n_chunks, mask_missing_past):
    kvw = SWA_GROUP * SWA_HEAD_DIM
    n_keys = WINDOW + chunk
    gq = SWA_GROUP * chunk
    lane_group = lax.broadcasted_iota(jnp.int32, (chunk, kvw), 1) // SWA_HEAD_DIM
    gq_group = lax.broadcasted_iota(jnp.int32, (1, gq), 1) // chunk
    key_i = lax.broadcasted_iota(jnp.int32, (n_keys, gq), 0)
    qscale = SWA_HEAD_DIM ** -0.5
    first_pos = pl.program_id(1) * (chunk * n_chunks) - WINDOW
    shared = {}

    def head_operands(h):
        if not shared:
            shared["k"] = jnp.concatenate([kp_ref[...], kc_ref[...]], axis=0).astype(BF16)
            v = jnp.concatenate([vp_ref[...], vc_ref[...]], axis=0)
            shared["vt"] = v.T.astype(BF16)
        if h not in shared:
            hd = slice(h * SWA_HEAD_DIM, (h + 1) * SWA_HEAD_DIM)
            k_rep = jnp.concatenate([shared["k"][:, hd]] * SWA_GROUP, axis=1)
            vt = shared["vt"][hd, :]
            vt_at = {}
            for c in range(n_chunks):
                shift = (c * chunk) % LANES
                if shift not in vt_at:
                    vt_at[shift] = vt[:, shift:]
            sink = jnp.zeros((1, gq), F32)
            for g in range(SWA_GROUP):
                sink = jnp.where(gq_group == g, sink_ref[h * SWA_GROUP + g], sink)
            shared[h] = (k_rep, vt_at, sink)
        return shared[h]

    def unit(h, c):
        k_rep, vt_at, sink = head_operands(h)
        rows = slice(c * chunk, (c + 1) * chunk)
        keys = slice(c * chunk, c * chunk + n_keys)
        cols = slice(h * kvw, (h + 1) * kvw)
        q = q_ref[rows, cols]
        qs = jnp.concatenate(
            [jnp.where(lane_group == g, q, jnp.zeros_like(q)) for g in range(SWA_GROUP)],
            axis=0)
        shift = (c * chunk) % LANES
        vt_keys = vt_at[shift][:, c * chunk - shift:c * chunk - shift + n_keys]
        yield
        s = _dot_nt(k_rep[keys], qs)
        yield
        if mask_missing_past and c * chunk < WINDOW:
            s = jnp.where(key_i + (first_pos + c * chunk) >= 0, s, jnp.finfo(F32).min)
        m = jnp.maximum(jnp.max(s, axis=0, keepdims=True) * qscale, sink) * LOG2E
        p = jnp.exp2(s * (qscale * LOG2E) - m)
        denom = jnp.sum(p, axis=0, keepdims=True) + jnp.exp2(sink * LOG2E - m)
        p = p.astype(BF16)
        yield
        o_t = _dot(vt_keys, p)
        yield
        o_t = (o_t / denom).T
        o = jnp.concatenate([o_t[g * chunk:(g + 1) * chunk] for g in range(SWA_GROUP)],
                            axis=1)
        o_ref[rows, cols] = (o * z_ref[rows, cols].astype(F32)).astype(o_ref.dtype)

    units = [unit(h, c) for h in range(SWA_KV_HEADS) for c in range(n_chunks)]
    return None, [_Stream(units, SWA_IN_FLIGHT)]


def _swa_task(sinks, q, z, kv, past, chunk, qb):
    n_batch, length, _ = q.shape
    if past is None:
        per_win = qb // WINDOW
        prev = lambda col: pl.BlockSpec(
            (None, WINDOW, SWA_KVW), lambda b, i: (b, jnp.maximum(i * per_win - 1, 0), col))
        k_past, v_past, kp_spec, vp_spec = kv, kv, prev(0), prev(1)
    else:
        k_past, v_past = past
        kp_spec = vp_spec = pl.BlockSpec((None, WINDOW, SWA_KVW), lambda b, i: (b, 0, 0))
    cur = lambda col: pl.BlockSpec((None, qb, SWA_KVW), lambda b, i: (b, i, col))
    return _Task(
        stages=functools.partial(_swa_stages, chunk=chunk, n_chunks=qb // chunk,
                                 mask_missing_past=past is None),
        operands=(sinks, q, k_past, v_past, kv, kv, z),
        in_specs=(pl.BlockSpec(memory_space=pltpu.SMEM), _tile(qb, SWA_WIDTH), kp_spec, vp_spec,
                  cur(0), cur(1), _tile(qb, SWA_WIDTH)),
        out_specs=(_tile(qb, SWA_WIDTH),),
        out_shapes=(jax.ShapeDtypeStruct((n_batch, length, SWA_WIDTH), BF16),))


def _outproj_stages(x_ref, o1_ref, o2_ref, p_ref, gple_ref, gfin_ref, *refs, n_blocks):
    wo1, wo2, wg, wp = (refs[j * n_blocks:(j + 1) * n_blocks] for j in range(4))
    y_ref, h_ref = refs[-2:]

    def pieces():
        o1, o2 = o1_ref[...], o2_ref[...]
        n = []
        for (cols, w1_ref, cs), (_, w2_ref, _) in zip(_block_pieces(wo1), _block_pieces(wo2)):
            h = x_ref[:, cols] + (_dot(o1, w1_ref[:, cs]) + _dot(o2, w2_ref[:, cs]))
            h_ref[:, cols] = h
            n.append((h * gple_ref[:, cols]).astype(BF16))
            yield
        n = jnp.concatenate(n, axis=1)
        r = _rms_scale(h_ref[...])
        p = p_ref[...].astype(BF16)
        tm = h_ref.shape[0]
        for rows in [slice(r0, r0 + OUT_ROW_GROUP) for r0 in range(0, tm, OUT_ROW_GROUP)]:
            for (cols, wg_ref, cs), (_, wp_ref, _) in zip(_block_pieces(wg), _block_pieces(wp)):
                gate = _sigmoid(_dot(n[rows], wg_ref[:, cs]) * r[rows])
                h_ref[rows, cols] = h_ref[rows, cols] + gate * _dot(p[rows], wp_ref[:, cs])
                yield
            h = h_ref[rows, :]
            y_ref[rows, :] = (h * _rms_scale(h)) * gfin_ref[...]

    return pieces(), []


def _outproj_task(x, o1, o2, p, batch0, tm, wo1, wo2, gple, wg, wp, gfin, y_prev=None):
    _, _, d = x.shape
    assert len(wo1) == len(wo2) == len(wg) == len(wp)
    blocks = (*wo1, *wo2, *wg, *wp)
    operands = (x, o1, o2, p, gple, gfin, *[w.array for w in blocks])
    in_specs = (_tile(tm, d, batch0), _tile(tm, o1.shape[2]), _tile(tm, o2.shape[2]),
                _tile(tm, p.shape[2], batch0), _resident(gple.shape), _resident(gfin.shape),
                *[w.spec() for w in blocks])
    aliases = None
    if y_prev is not None:
        aliases = {len(operands): 0}
        operands += (y_prev,)
        in_specs += (pl.BlockSpec(memory_space=pl.ANY),)
    return _Task(stages=functools.partial(_outproj_stages, n_blocks=len(wo1)),
                 operands=operands, in_specs=in_specs,
                 out_specs=(_tile(tm, d, batch0),),
                 out_shapes=(jax.ShapeDtypeStruct(x.shape, F32),),
                 scratch=(pltpu.VMEM((tm, d), F32),), aliases=aliases)


def _split_w_in(w_in):
    lr0 = 2 * GLA_KW + 2 * GLA_WIDTH
    n_rows, n_in = w_in.shape
    n_out = n_in - GLA_RANK + LANES

    def relayout(w_ref, o_ref):
        w = w_ref[...]
        o_ref[:, :lr0] = w[:, :lr0].astype(BF16)
        o_ref[:, lr0:n_in - GLA_RANK] = w[:, lr0 + GLA_RANK:].astype(BF16)
        o_ref[:, n_in - GLA_RANK:] = jnp.concatenate(
            [w[:, lr0:lr0 + GLA_RANK], jnp.zeros((w.shape[0], LANES - GLA_RANK), w.dtype)],
            axis=1).astype(BF16)

    w = pl.pallas_call(
        relayout,
        grid=(n_rows // RELAYOUT_ROWS,),
        in_specs=[pl.BlockSpec((RELAYOUT_ROWS, n_in), lambda i: (i, 0))],
        out_specs=pl.BlockSpec((RELAYOUT_ROWS, n_out), lambda i: (i, 0)),
        out_shape=jax.ShapeDtypeStruct((n_rows, n_out), BF16),
        compiler_params=pltpu.CompilerParams(
            dimension_semantics=("arbitrary",), vmem_limit_bytes=VMEM_LIMIT_BYTES),
        name="w_in_relayout",
    )(w_in)
    widths = (GLA_KW, GLA_KW, GLA_WIDTH, GLA_WIDTH, SWA_WIDTH, 2 * SWA_KVW, SWA_WIDTH, LANES)
    groups, col0 = [], 0
    for width in widths:
        groups.append(_col_blocks(w, col0, width))
        col0 += width
    return tuple(groups)


class _Weights(NamedTuple):
    g_mix: Any
    w_in: tuple
    w_gate: Any
    b_gate: Any
    g_gla_out: Any
    sinks: Any
    wo1: Any
    wo2: Any
    g_ple: Any
    w_ple_gate: Any
    w_ple_proj: Any
    g_final: Any


def _attn_tasks(w, proj, s0, state_batch0, past, chunk, tile):
    q1, k1, v1, z1, q2, kv2, z2, lr = proj
    return [_gla_task(q1, k1, v1, z1, lr, w.w_gate, w.b_gate, w.g_gla_out, s0, state_batch0,
                      chunk, tile),
            _swa_task(w.sinks, q2, z2, kv2, past, chunk, tile)]


def _prompt_path(w, x, p, s0):
    n_batch, length, _ = x.shape
    assert n_batch == 2, "a call holding all three stages would not fit the VMEM budget"
    proj, attn = {}, {}
    y = None
    for s in range(n_batch + 2):
        has_in, has_attn, has_out = s < n_batch, 1 <= s <= n_batch, s >= 2
        tile = TILE_FUSED if has_attn else TILE_MATMUL_ONLY
        tasks, names = [], []
        if has_attn:
            tasks += _attn_tasks(w, proj[s - 1], s0, s - 1, None, CHUNK, tile)
            names.append("attn")
        if has_in:
            tasks.append(_inproj_task(x, s, 1, tile, w.g_mix, w.w_in))
            names.append("inproj")
        if has_out:
            o1, o2, _ = attn[s - 2]
            tasks.append(_outproj_task(x, o1, o2, p, s - 2, tile, w.wo1, w.wo2, w.g_ple,
                                       w.w_ple_gate, w.w_ple_proj, w.g_final, y_prev=y))
            names.append("outproj")
        res = _run(tasks, (1, length // tile), "_".join(names))
        if has_attn:
            (o1, s_new), (o2,) = res[0], res[1]
            attn[s - 1] = (o1, o2, s_new)
            res = res[2:]
        if has_in:
            proj[s] = res[0]
            res = res[1:]
        if has_out:
            (y,) = res[0]
    states = jnp.concatenate([attn[b][2] for b in range(n_batch)], axis=0)
    kv = [proj[b][5] for b in range(n_batch)]
    return y, states, kv


def _sample_path(w, x, p, s0, past):
    n_batch, length, d = x.shape
    flat = lambda a: a.reshape(1, n_batch * length, a.shape[-1])
    per_batch = lambda a: a.reshape(n_batch, length, a.shape[-1])
    (proj,) = _run([_inproj_task(flat(x), 0, 1, n_batch * length, w.g_mix, w.w_in)], (1, 1),
                   "inproj_sample")
    proj = tuple(per_batch(a) for a in proj)
    (o1, s_new), (o2,) = _run(_attn_tasks(w, proj, s0, 0, past, min(CHUNK, length), length),
                              (n_batch, 1), "attn_sample")
    ((y,),) = _run([_outproj_task(flat(x), flat(o1), flat(o2), flat(p), 0, n_batch * length,
                                  w.wo1, w.wo2, w.g_ple, w.w_ple_gate, w.w_ple_proj, w.g_final)],
                   (1, 1), "outproj_sample")
    return y.reshape(n_batch, length, d), s_new, proj[5]


def kernel(x_prompt, x_sample, p_prompt, p_sample, state_gla, cache_swa_k, cache_swa_v, g_mix, w_in, w_gate_up, b_gate, g_gla_out, swa_sinks, w_out, g_ple, w_ple_gate, w_ple_proj, g_final):
    depth = w_in.shape[0]
    assert depth == 1, "the final RMSNorm is fused into the last layer's output kernel"
    bp, lp, _ = x_prompt.shape
    bs, ls, _ = x_sample.shape
    assert cache_swa_k.shape[2] == WINDOW
    i = 0
    row = lambda a: a.reshape(1, -1).astype(F32)
    assert GLA_WIDTH == SWA_WIDTH
    w_out_bf16 = w_out[i].astype(BF16)
    w = _Weights(
        g_mix=row(g_mix[i]), w_in=_split_w_in(w_in[i]),
        w_gate=jnp.pad(w_gate_up[i].astype(BF16), ((0, LANES - GLA_RANK), (0, 0))),
        b_gate=row(b_gate[i]), g_gla_out=row(g_gla_out[i]),
        sinks=swa_sinks[i].reshape(-1).astype(F32),
        wo1=_col_blocks(w_out_bf16, row_block=0, n_rows=GLA_WIDTH),
        wo2=_col_blocks(w_out_bf16, row_block=1, n_rows=SWA_WIDTH),
        g_ple=row(g_ple[i]), w_ple_gate=_col_blocks(w_ple_gate[i].astype(BF16)),
        w_ple_proj=_col_blocks(w_ple_proj[i].astype(BF16)), g_final=row(g_final))

    s0_p = jnp.zeros((bp, GLA_HEADS, GLA_DK, GLA_DV), F32)
    y_p, s_p, kv_p = _prompt_path(w, x_prompt, p_prompt[i], s0_p)
    keep = min(WINDOW, lp)
    last = lambda col: jnp.concatenate(
        [a[:, lp - keep:, col * SWA_KVW:(col + 1) * SWA_KVW] for a in kv_p], axis=0
    ).reshape(bp, keep, SWA_KV_HEADS, SWA_HEAD_DIM)
    k_p, v_p = last(0), last(1)

    past = (cache_swa_k[i].reshape(bs, WINDOW, SWA_KVW), cache_swa_v[i].reshape(bs, WINDOW, SWA_KVW))
    y_s, s_s, kv_s = _sample_path(w, x_sample, p_sample[i], state_gla[i], past)
    k_s = kv_s[:, :, :SWA_KVW].reshape(bs, ls, SWA_KV_HEADS, SWA_HEAD_DIM)
    v_s = kv_s[:, :, SWA_KVW:].reshape(bs, ls, SWA_KV_HEADS, SWA_HEAD_DIM)
    return (y_p, y_s, s_p[None], k_p[None], v_p[None], s_s[None], k_s[None], v_s[None])
```

```python
import functools
from typing import Any, Callable, NamedTuple

import jax
import jax.numpy as jnp
from jax import lax
from jax.experimental import pallas as pl
from jax.experimental.pallas import tpu as pltpu

F32 = jnp.float32
BF16 = jnp.bfloat16

CHUNK = 64
RMS_EPS = 1e-6
GLA_HEADS = 4
GLA_DK = 128
GLA_DV = 256
GLA_KW = GLA_HEADS * GLA_DK
GLA_WIDTH = GLA_HEADS * GLA_DV
GLA_RANK = 16
GLA_TAU = 16.0
SWA_KV_HEADS = 4
SWA_GROUP = 4
SWA_HEAD_DIM = 64
SWA_WIDTH = SWA_KV_HEADS * SWA_GROUP * SWA_HEAD_DIM
SWA_KVW = SWA_KV_HEADS * SWA_HEAD_DIM
WINDOW = 128

LANES = 128
MXU_WIDTH = 256
WEIGHT_BLOCK = 512
VMEM_LIMIT_BYTES = 56 * 1024 * 1024
TILE_MATMUL_ONLY = 512
TILE_FUSED = 256
RELAYOUT_ROWS = 256
OUT_ROW_GROUP = 256
GLA_IN_FLIGHT = 2
SWA_IN_FLIGHT = 4
NORM_HEAD_START = 2

NT_DIMS = (((1,), (1,)), ((), ()))
TN_DIMS = (((0,), (0,)), ((), ()))


def _dot(a, b):
    return jnp.dot(a, b, preferred_element_type=F32)


def _dot_nt(a, b):
    return lax.dot_general(a, b, NT_DIMS, preferred_element_type=F32)


def _dot_tn(a, b):
    return lax.dot_general(a, b, TN_DIMS, preferred_element_type=F32)


def _rms_scale(x):
    return lax.rsqrt(jnp.mean(x * x, axis=-1, keepdims=True) + RMS_EPS)


LOG2E = 1.4426950408889634
LN2 = 0.6931471805599453


def _sigmoid(x):
    return 1.0 / (1.0 + jnp.exp2(x * -LOG2E))


def _silu(x):
    return x * _sigmoid(x)


def _resident(shape):
    zeros = (0,) * len(shape)
    return pl.BlockSpec(shape, lambda *_: zeros, pipeline_mode=pl.Buffered(1))


def _tile(tm, width, batch0=0):
    return pl.BlockSpec((None, tm, width), lambda b, i: (batch0 + b, i, 0))


class _Block(NamedTuple):
    array: Any
    shape: tuple
    index: tuple

    def spec(self):
        index = self.index
        return pl.BlockSpec(self.shape, lambda *_: index, pipeline_mode=pl.Buffered(1))


def _col_blocks(w, col0=0, width=None, row_block=0, n_rows=None):
    n_rows = n_rows or w.shape[0]
    width = width or w.shape[1] - col0
    blk = min(WEIGHT_BLOCK, width)
    assert col0 % blk == 0 and width % blk == 0
    return tuple(_Block(w, (n_rows, blk), (row_block, (col0 + c) // blk))
                 for c in range(0, width, blk))


def _block_pieces(block_refs):
    out, base = [], 0
    for ref in block_refs:
        width = ref.shape[1]
        for c in range(0, width, MXU_WIDTH):
            hi = min(c + MXU_WIDTH, width)
            out.append((slice(base + c, base + hi), ref, slice(c, hi)))
        base += width
    return out


class _Task(NamedTuple):
    stages: Callable[..., Any]
    operands: tuple
    in_specs: tuple
    out_specs: tuple
    out_shapes: tuple
    scratch: tuple = ()
    aliases: Any = None


_DONE = object()


class _Stream:
    def __init__(self, units, max_active):
        self.pending, self.active, self.max_active = list(units), [], max_active

    def busy(self):
        return bool(self.pending or self.active)

    def step(self):
        self.active = [g for g in self.active if next(g, _DONE) is not _DONE]
        if self.pending and len(self.active) < self.max_active:
            g = self.pending.pop(0)
            if next(g, _DONE) is not _DONE:
                self.active.append(g)


def _drive(heavy, light):
    heavy = [g for g in heavy if g is not None]
    while heavy or any(s.busy() for s in light):
        heavy = [g for g in heavy if next(g, _DONE) is not _DONE]
        for s in light:
            s.step()


def _run(tasks, grid, name):
    n_in = [len(t.operands) for t in tasks]
    n_out = [len(t.out_shapes) for t in tasks]
    n_scr = [len(t.scratch) for t in tasks]

    def body(*refs):
        ins = refs[:sum(n_in)]
        outs = refs[sum(n_in):sum(n_in) + sum(n_out)]
        scrs = refs[sum(n_in) + sum(n_out):]
        heavy, light = [], []
        i = o = s = 0
        for t, ni, no, ns in zip(tasks, n_in, n_out, n_scr):
            h, l = t.stages(*ins[i:i + ni], *outs[o:o + no], *scrs[s:s + ns])
            heavy.append(h)
            light += l
            i, o, s = i + ni, o + no, s + ns
        _drive(heavy, light)

    aliases = {}
    i = o = 0
    for t, ni, no in zip(tasks, n_in, n_out):
        for src, dst in (t.aliases or {}).items():
            aliases[i + src] = o + dst
        i, o = i + ni, o + no

    flat = pl.pallas_call(
        body,
        grid=grid,
        in_specs=[s for t in tasks for s in t.in_specs],
        out_specs=tuple(s for t in tasks for s in t.out_specs),
        out_shape=tuple(s for t in tasks for s in t.out_shapes),
        scratch_shapes=[s for t in tasks for s in t.scratch],
        input_output_aliases=aliases,
        compiler_params=pltpu.CompilerParams(
            dimension_semantics=("arbitrary", "arbitrary"), vmem_limit_bytes=VMEM_LIMIT_BYTES),
        name=name,
    )(*[a for t in tasks for a in t.operands])
    results, o = [], 0
    for no in n_out:
        results.append(tuple(flat[o:o + no]))
        o += no
    return results


def _inproj_stages(x_ref, g_ref, *refs, blocks_per_group):
    n_blocks = sum(blocks_per_group)
    w_refs, o_refs = refs[:n_blocks], refs[n_blocks:]

    def pieces():
        x = x_ref[...]
        n = ((x * _rms_scale(x)) * g_ref[...]).astype(BF16)
        for _ in range(NORM_HEAD_START):
            yield
        first = 0
        for n_blk, o_ref, is_gate in zip(blocks_per_group, o_refs, IN_IS_GATE):
            for cols, w_ref, cs in _block_pieces(w_refs[first:first + n_blk]):
                u = _dot(n, w_ref[:, cs])
                o_ref[:, cols] = (_silu(u) if is_gate else u).astype(o_ref.dtype)
                yield
            first += n_blk

    return pieces(), []


IN_DTYPES = (BF16, BF16, BF16, BF16, BF16, F32, BF16, BF16)
IN_IS_GATE = (False, False, False, True, False, False, True, False)


def _inproj_task(x, batch0, n_batch, tm, g_mix, groups):
    _, length, d = x.shape
    blocks = [w for g in groups for w in g]
    widths = [sum(w.shape[1] for w in g) for g in groups]
    return _Task(
        stages=functools.partial(_inproj_stages, blocks_per_group=tuple(len(g) for g in groups)),
        operands=(x, g_mix, *[w.array for w in blocks]),
        in_specs=(_tile(tm, d, batch0), _resident(g_mix.shape), *[w.spec() for w in blocks]),
        out_specs=tuple(_tile(tm, n) for n in widths),
        out_shapes=tuple(jax.ShapeDtypeStruct((n_batch, length, n), dt)
                         for n, dt in zip(widths, IN_DTYPES)))


def _gla_stages(q_ref, k_ref, v_ref, z_ref, lr_ref, wg_ref, bg_ref, gout_ref, s0_ref,
                o_ref, s_ref, *, chunk, n_chunks):
    @pl.when(pl.program_id(1) == 0)
    def _():
        s_ref[...] = s0_ref[...]

    row = lax.broadcasted_iota(jnp.int32, (chunk, chunk), 0)
    col = lax.broadcasted_iota(jnp.int32, (chunk, chunk), 1)
    causal = row >= col
    row2 = lax.broadcasted_iota(jnp.int32, (chunk, 2 * chunk), 0)
    col2 = lax.broadcasted_iota(jnp.int32, (chunk, 2 * chunk), 1)
    tril2 = jnp.where(row2 >= lax.rem(col2, chunk), 1.0, 0.0).astype(BF16)
    eye = (lax.broadcasted_iota(jnp.int32, (GLA_DK, GLA_DK), 0)
           == lax.broadcasted_iota(jnp.int32, (GLA_DK, GLA_DK), 1))
    scale = GLA_DK ** -0.5
    heads = [(slice(h * GLA_DK, (h + 1) * GLA_DK), slice(h * GLA_DV, (h + 1) * GLA_DV))
             for h in range(GLA_HEADS)]
    transpose_ahead = True

    def chunk_unit(c):
        r = slice(c * chunk, (c + 1) * chunk)
        gpre = _dot(lr_ref[r, :], wg_ref[...])
        yield
        gpre = gpre + bg_ref[...]
        soft = jnp.log2(1.0 + jnp.exp2(jnp.abs(gpre) * -LOG2E))
        g = jnp.minimum(gpre, 0.0) * (1.0 / GLA_TAU) - soft * (LN2 / GLA_TAU)
        g_hi = g.astype(BF16)
        g_lo = (g - g_hi.astype(F32)).astype(BF16)
        g2 = jnp.concatenate([g_hi, g_lo], axis=0)
        yield
        b = _dot(tril2, g2)
        yield
        b_last = b[chunk - 1:chunk, :]
        q = q_ref[r, :].astype(F32)
        k = k_ref[r, :].astype(F32)
        qe = ((q * scale) * jnp.exp(b)).astype(BF16)
        ke = (k * jnp.exp(-b)).astype(BF16)
        kd = k * jnp.exp(b_last - b)
        decay = jnp.exp(b_last)
        v = [v_ref[r, vs] for _, vs in heads]
        if transpose_ahead:
            kd_t = [kd[:, ks].T.astype(BF16) for ks, _ in heads]
        else:
            kd = kd.astype(BF16)
        yield
        a_raw = [_dot_nt(qe[:, ks], ke[:, ks]) for ks, _ in heads]
        if transpose_ahead:
            kv = [_dot(kd_t[h], v[h]) for h in range(GLA_HEADS)]
        else:
            kv = [_dot_tn(kd[:, ks], v[h]) for h, (ks, _) in enumerate(heads)]
        yield
        lhs, rhs = [], []
        for h, (ks, _) in enumerate(heads):
            a = jnp.where(causal, a_raw[h], 0.0).astype(BF16)
            s = s_ref[h]
            lhs.append(jnp.concatenate([qe[:, ks], a], axis=1))
            rhs.append(jnp.concatenate([s.astype(BF16), v[h]], axis=0))
            dcol = jnp.sum(jnp.where(eye, decay[:, ks], 0.0), axis=1, keepdims=True)
            s_ref[h] = dcol * s + kv[h]
        yield
        o = [_dot(lhs[h], rhs[h]) for h in range(GLA_HEADS)]
        yield
        for h, (_, vs) in enumerate(heads):
            oh = (o[h] * _rms_scale(o[h])) * gout_ref[...]
            o_ref[r, vs] = (oh * z_ref[r, vs].astype(F32)).astype(o_ref.dtype)

    return None, [_Stream([chunk_unit(c) for c in range(n_chunks)], GLA_IN_FLIGHT)]


def _gla_task(q, k, v, z, lr, wg, bg, gout, s0, state_batch0, chunk, tb):
    n_batch, length, _ = q.shape
    st_in = pl.BlockSpec((None, GLA_HEADS, GLA_DK, GLA_DV),
                         lambda b, i: (state_batch0 + b, 0, 0, 0))
    st_out = pl.BlockSpec((None, GLA_HEADS, GLA_DK, GLA_DV), lambda b, i: (b, 0, 0, 0))
    return _Task(
        stages=functools.partial(_gla_stages, chunk=chunk, n_chunks=tb // chunk),
        operands=(q, k, v, z, lr, wg, bg, gout, s0),
        in_specs=(_tile(tb, GLA_KW), _tile(tb, GLA_KW), _tile(tb, GLA_WIDTH), _tile(tb, GLA_WIDTH),
                  _tile(tb, LANES), _resident(wg.shape), _resident(bg.shape),
                  _resident(gout.shape), st_in),
        out_specs=(_tile(tb, GLA_WIDTH), st_out),
        out_shapes=(jax.ShapeDtypeStruct((n_batch, length, GLA_WIDTH), BF16),
                    jax.ShapeDtypeStruct((n_batch, GLA_HEADS, GLA_DK, GLA_DV), F32)))


def _swa_stages(sink_ref, q_ref, kp_ref, vp_ref, kc_ref, vc_ref, z_ref, o_ref,
                *, chunk, n_chunks, mask_missing_past):
    kvw = SWA_GROUP * SWA_HEAD_DIM
    n_keys = WINDOW + chunk
    gq = SWA_GROUP * chunk
    lane_group = lax.broadcasted_iota(jnp.int32, (chunk, kvw), 1) // SWA_HEAD_DIM
    gq_group = lax.broadcasted_iota(jnp.int32, (1, gq), 1) // chunk
    key_i = lax.broadcasted_iota(jnp.int32, (n_keys, gq), 0)
    qscale = SWA_HEAD_DIM ** -0.5
    first_pos = pl.program_id(1) * (chunk * n_chunks) - WINDOW
    shared = {}

    def head_operands(h):
        if not shared:
            shared["k"] = jnp.concatenate([kp_ref[...], kc_ref[...]], axis=0).astype(BF16)
            v = jnp.concatenate([vp_ref[...], vc_ref[...]], axis=0)
            shared["vt"] = v.T.astype(BF16)
        if h not in shared:
            hd = slice(h * SWA_HEAD_DIM, (h + 1) * SWA_HEAD_DIM)
            k_rep = jnp.concatenate([shared["k"][:, hd]] * SWA_GROUP, axis=1)
            vt = shared["vt"][hd, :]
            vt_at = {}
            for c in range(n_chunks):
                shift = (c * chunk) % LANES
                if shift not in vt_at:
                    vt_at[shift] = vt[:, shift:]
            sink = jnp.zeros((1, gq), F32)
            for g in range(SWA_GROUP):
                sink = jnp.where(gq_group == g, sink_ref[h * SWA_GROUP + g], sink)
            shared[h] = (k_rep, vt_at, sink)
        return shared[h]

    def unit(h, c):
        k_rep, vt_at, sink = head_operands(h)
        rows = slice(c * chunk, (c + 1) * chunk)
        keys = slice(c * chunk, c * chunk + n_keys)
        cols = slice(h * kvw, (h + 1) * kvw)
        q = q_ref[rows, cols]
        qs = jnp.concatenate(
            [jnp.where(lane_group == g, q, jnp.zeros_like(q)) for g in range(SWA_GROUP)],
            axis=0)
        shift = (c * chunk) % LANES
        vt_keys = vt_at[shift][:, c * chunk - shift:c * chunk - shift + n_keys]
        yield
        s = _dot_nt(k_rep[keys], qs)
        yield
        if mask_missing_past and c * chunk < WINDOW:
            s = jnp.where(key_i + (first_pos + c * chunk) >= 0, s, jnp.finfo(F32).min)
        m = jnp.maximum(jnp.max(s, axis=0, keepdims=True) * qscale, sink) * LOG2E
        p = jnp.exp2(s * (qscale * LOG2E) - m)
        denom = jnp.sum(p, axis=0, keepdims=True) + jnp.exp2(sink * LOG2E - m)
        p = p.astype(BF16)
        yield
        o_t = _dot(vt_keys, p)
        yield
        o_t = (o_t / denom).T
        o = jnp.concatenate([o_t[g * chunk:(g + 1) * chunk] for g in range(SWA_GROUP)],
                            axis=1)
        o_ref[rows, cols] = (o * z_ref[rows, cols].astype(F32)).astype(o_ref.dtype)

    units = [unit(h, c) for h in range(SWA_KV_HEADS) for c in range(n_chunks)]
    return None, [_Stream(units, SWA_IN_FLIGHT)]


def _swa_task(sinks, q, z, kv, past, chunk, qb):
    n_batch, length, _ = q.shape
    if past is None:
        per_win = qb // WINDOW
        prev = lambda col: pl.BlockSpec(
            (None, WINDOW, SWA_KVW), lambda b, i: (b, jnp.maximum(i * per_win - 1, 0), col))
        k_past, v_past, kp_spec, vp_spec = kv, kv, prev(0), prev(1)
    else:
        k_past, v_past = past
        kp_spec = vp_spec = pl.BlockSpec((None, WINDOW, SWA_KVW), lambda b, i: (b, 0, 0))
    cur = lambda col: pl.BlockSpec((None, qb, SWA_KVW), lambda b, i: (b, i, col))
    return _Task(
        stages=functools.partial(_swa_stages, chunk=chunk, n_chunks=qb // chunk,
                                 mask_missing_past=past is None),
        operands=(sinks, q, k_past, v_past, kv, kv, z),
        in_specs=(pl.BlockSpec(memory_space=pltpu.SMEM), _tile(qb, SWA_WIDTH), kp_spec, vp_spec,
                  cur(0), cur(1), _tile(qb, SWA_WIDTH)),
        out_specs=(_tile(qb, SWA_WIDTH),),
        out_shapes=(jax.ShapeDtypeStruct((n_batch, length, SWA_WIDTH), BF16),))


def _outproj_stages(x_ref, o1_ref, o2_ref, p_ref, gple_ref, gfin_ref, *refs, n_blocks):
    wo1, wo2, wg, wp = (refs[j * n_blocks:(j + 1) * n_blocks] for j in range(4))
    y_ref, h_ref = refs[-2:]

    def pieces():
        o1, o2 = o1_ref[...], o2_ref[...]
        n = []
        for (cols, w1_ref, cs), (_, w2_ref, _) in zip(_block_pieces(wo1), _block_pieces(wo2)):
            h = x_ref[:, cols] + (_dot(o1, w1_ref[:, cs]) + _dot(o2, w2_ref[:, cs]))
            h_ref[:, cols] = h
            n.append((h * gple_ref[:, cols]).astype(BF16))
            yield
        n = jnp.concatenate(n, axis=1)
        r = _rms_scale(h_ref[...])
        p = p_ref[...].astype(BF16)
        tm = h_ref.shape[0]
        for rows in [slice(r0, r0 + OUT_ROW_GROUP) for r0 in range(0, tm, OUT_ROW_GROUP)]:
            for (cols, wg_ref, cs), (_, wp_ref, _) in zip(_block_pieces(wg), _block_pieces(wp)):
                gate = _sigmoid(_dot(n[rows], wg_ref[:, cs]) * r[rows])
                h_ref[rows, cols] = h_ref[rows, cols] + gate * _dot(p[rows], wp_ref[:, cs])
                yield
            h = h_ref[rows, :]
            y_ref[rows, :] = (h * _rms_scale(h)) * gfin_ref[...]

    return pieces(), []


def _outproj_task(x, o1, o2, p, batch0, tm, wo1, wo2, gple, wg, wp, gfin, y_prev=None):
    _, _, d = x.shape
    assert len(wo1) == len(wo2) == len(wg) == len(wp)
    blocks = (*wo1, *wo2, *wg, *wp)
    operands = (x, o1, o2, p, gple, gfin, *[w.array for w in blocks])
    in_specs = (_tile(tm, d, batch0), _tile(tm, o1.shape[2]), _tile(tm, o2.shape[2]),
                _tile(tm, p.shape[2], batch0), _resident(gple.shape), _resident(gfin.shape),
                *[w.spec() for w in blocks])
    aliases = None
    if y_prev is not None:
        aliases = {len(operands): 0}
        operands += (y_prev,)
        in_specs += (pl.BlockSpec(memory_space=pl.ANY),)
    return _Task(stages=functools.partial(_outproj_stages, n_blocks=len(wo1)),
                 operands=operands, in_specs=in_specs,
                 out_specs=(_tile(tm, d, batch0),),
                 out_shapes=(jax.ShapeDtypeStruct(x.shape, F32),),
                 scratch=(pltpu.VMEM((tm, d), F32),), aliases=aliases)


def _split_w_in(w_in):
    lr0 = 2 * GLA_KW + 2 * GLA_WIDTH
    n_rows, n_in = w_in.shape
    n_main = n_in - GLA_RANK
    assert lr0 % WEIGHT_BLOCK == 0 and n_main % WEIGHT_BLOCK == 0

    def relayout(w_ref, o_ref):
        o_ref[...] = w_ref[...].T.astype(BF16)

    assert WEIGHT_BLOCK % GLA_RANK == 0
    src_col = lambda j: GLA_RANK * (j * (WEIGHT_BLOCK // GLA_RANK)
                                    + (j >= lr0 // WEIGHT_BLOCK).astype(jnp.int32))
    w = pl.pallas_call(
        relayout,
        grid=(n_main // WEIGHT_BLOCK,),
        in_specs=[pl.BlockSpec((pl.Element(WEIGHT_BLOCK), pl.Element(n_rows)),
                               lambda j: (src_col(j), 0))],
        out_specs=pl.BlockSpec((n_rows, WEIGHT_BLOCK), lambda j: (0, j)),
        out_shape=jax.ShapeDtypeStruct((n_rows, n_main), BF16),
        compiler_params=pltpu.CompilerParams(
            dimension_semantics=("arbitrary",), vmem_limit_bytes=VMEM_LIMIT_BYTES),
        name="w_in_relayout",
    )(w_in.T)

    def relayout_gate(w_ref, o_ref):
        pad = jnp.zeros((LANES - GLA_RANK, n_rows), F32)
        o_ref[...] = jnp.concatenate([w_ref[...], pad], axis=0).T.astype(BF16)

    w_lr = pl.pallas_call(
        relayout_gate,
        grid=(1,),
        in_specs=[pl.BlockSpec((pl.Element(GLA_RANK), pl.Element(n_rows)), lambda j: (lr0, 0))],
        out_specs=pl.BlockSpec((n_rows, LANES), lambda j: (0, 0)),
        out_shape=jax.ShapeDtypeStruct((n_rows, LANES), BF16),
        name="w_gate_relayout",
    )(w_in.T)
    widths = (GLA_KW, GLA_KW, GLA_WIDTH, GLA_WIDTH, SWA_WIDTH, 2 * SWA_KVW, SWA_WIDTH)
    groups, col0 = [], 0
    for width in widths:
        groups.append(_col_blocks(w, col0, width))
        col0 += width
    return (*groups, _col_blocks(w_lr))


class _Weights(NamedTuple):
    g_mix: Any
    w_in: tuple
    w_gate: Any
    b_gate: Any
    g_gla_out: Any
    sinks: Any
    wo1: Any
    wo2: Any
    g_ple: Any
    w_ple_gate: Any
    w_ple_proj: Any
    g_final: Any


def _attn_tasks(w, proj, s0, state_batch0, past, chunk, tile):
    q1, k1, v1, z1, q2, kv2, z2, lr = proj
    return [_gla_task(q1, k1, v1, z1, lr, w.w_gate, w.b_gate, w.g_gla_out, s0, state_batch0,
                      chunk, tile),
            _swa_task(w.sinks, q2, z2, kv2, past, chunk, tile)]


def _prompt_path(w, x, p, s0):
    n_batch, length, _ = x.shape
    assert n_batch == 2, "a call holding all three stages would not fit the VMEM budget"
    proj, attn = {}, {}
    y = None
    for s in range(n_batch + 2):
        has_in, has_attn, has_out = s < n_batch, 1 <= s <= n_batch, s >= 2
        tile = TILE_FUSED if has_attn else TILE_MATMUL_ONLY
        tasks, names = [], []
        if has_attn:
            tasks += _attn_tasks(w, proj[s - 1], s0, s - 1, None, CHUNK, tile)
            names.append("attn")
        if has_in:
            tasks.append(_inproj_task(x, s, 1, tile, w.g_mix, w.w_in))
            names.append("inproj")
        if has_out:
            o1, o2, _ = attn[s - 2]
            tasks.append(_outproj_task(x, o1, o2, p, s - 2, tile, w.wo1, w.wo2, w.g_ple,
                                       w.w_ple_gate, w.w_ple_proj, w.g_final, y_prev=y))
            names.append("outproj")
        res = _run(tasks, (1, length // tile), "_".join(names))
        if has_attn:
            (o1, s_new), (o2,) = res[0], res[1]
            attn[s - 1] = (o1, o2, s_new)
            res = res[2:]
        if has_in:
            proj[s] = res[0]
            res = res[1:]
        if has_out:
            (y,) = res[0]
    states = jnp.concatenate([attn[b][2] for b in range(n_batch)], axis=0)
    kv = [proj[b][5] for b in range(n_batch)]
    return y, states, kv


def _sample_path(w, x, p, s0, past):
    n_batch, length, d = x.shape
    flat = lambda a: a.reshape(1, n_batch * length, a.shape[-1])
    per_batch = lambda a: a.reshape(n_batch, length, a.shape[-1])
    (proj,) = _run([_inproj_task(flat(x), 0, 1, n_batch * length, w.g_mix, w.w_in)], (1, 1),
                   "inproj_sample")
    proj = tuple(per_batch(a) for a in proj)
    (o1, s_new), (o2,) = _run(_attn_tasks(w, proj, s0, 0, past, min(CHUNK, length), length),
                              (n_batch, 1), "attn_sample")
    ((y,),) = _run([_outproj_task(flat(x), flat(o1), flat(o2), flat(p), 0, n_batch * length,
                                  w.wo1, w.wo2, w.g_ple, w.w_ple_gate, w.w_ple_proj, w.g_final)],
                   (1, 1), "outproj_sample")
    return y.reshape(n_batch, length, d), s_new, proj[5]


def kernel(x_prompt, x_sample, p_prompt, p_sample, state_gla, cache_swa_k, cache_swa_v, g_mix, w_in, w_gate_up, b_gate, g_gla_out, swa_sinks, w_out, g_ple, w_ple_gate, w_ple_proj, g_final):
    depth = w_in.shape[0]
    assert depth == 1, "the final RMSNorm is fused into the last layer's output kernel"
    bp, lp, _ = x_prompt.shape
    bs, ls, _ = x_sample.shape
    assert cache_swa_k.shape[2] == WINDOW
    i = 0
    row = lambda a: a.reshape(1, -1).astype(F32)
    assert GLA_WIDTH == SWA_WIDTH
    w_out_bf16 = w_out[i].astype(BF16)
    w = _Weights(
        g_mix=row(g_mix[i]), w_in=_split_w_in(w_in[i]),
        w_gate=jnp.pad(w_gate_up[i].astype(BF16), ((0, LANES - GLA_RANK), (0, 0))),
        b_gate=row(b_gate[i]), g_gla_out=row(g_gla_out[i]),
        sinks=swa_sinks[i].reshape(-1).astype(F32),
        wo1=_col_blocks(w_out_bf16, row_block=0, n_rows=GLA_WIDTH),
        wo2=_col_blocks(w_out_bf16, row_block=1, n_rows=SWA_WIDTH),
        g_ple=row(g_ple[i]), w_ple_gate=_col_blocks(w_ple_gate[i].astype(BF16)),
        w_ple_proj=_col_blocks(w_ple_proj[i].astype(BF16)), g_final=row(g_final))

    s0_p = jnp.zeros((bp, GLA_HEADS, GLA_DK, GLA_DV), F32)
    y_p, s_p, kv_p = _prompt_path(w, x_prompt, p_prompt[i], s0_p)
    keep = min(WINDOW, lp)
    last = lambda col: jnp.concatenate(
        [a[:, lp - keep:, col * SWA_KVW:(col + 1) * SWA_KVW] for a in kv_p], axis=0
    ).reshape(bp, keep, SWA_KV_HEADS, SWA_HEAD_DIM)
    k_p, v_p = last(0), last(1)

    past = (cache_swa_k[i].reshape(bs, WINDOW, SWA_KVW), cache_swa_v[i].reshape(bs, WINDOW, SWA_KVW))
    y_s, s_s, kv_s = _sample_path(w, x_sample, p_sample[i], state_gla[i], past)
    k_s = kv_s[:, :, :SWA_KVW].reshape(bs, ls, SWA_KV_HEADS, SWA_HEAD_DIM)
    v_s = kv_s[:, :, SWA_KVW:].reshape(bs, ls, SWA_KV_HEADS, SWA_HEAD_DIM)
    return (y_p, y_s, s_p[None], k_p[None], v_p[None], s_s[None], k_s[None], v_s[None])
```

```python
import functools
from typing import Any, Callable, NamedTuple

import jax
import jax.numpy as jnp
from jax import lax
from jax.experimental import pallas as pl
from jax.experimental.pallas import tpu as pltpu

F32 = jnp.float32
BF16 = jnp.bfloat16

CHUNK = 64
RMS_EPS = 1e-6
GLA_HEADS = 4
GLA_DK = 128
GLA_DV = 256
GLA_KW = GLA_HEADS * GLA_DK
GLA_WIDTH = GLA_HEADS * GLA_DV
GLA_RANK = 16
GLA_TAU = 16.0
SWA_KV_HEADS = 4
SWA_GROUP = 4
SWA_HEAD_DIM = 64
SWA_WIDTH = SWA_KV_HEADS * SWA_GROUP * SWA_HEAD_DIM
SWA_KVW = SWA_KV_HEADS * SWA_HEAD_DIM
WINDOW = 128

LANES = 128
MXU_WIDTH = 256
WEIGHT_BLOCK = 512
VMEM_LIMIT_BYTES = 56 * 1024 * 1024
TILE_MATMUL_ONLY = 512
TILE_FUSED = 256
RELAYOUT_ROWS = 256
OUT_ROW_GROUP = 256
GLA_IN_FLIGHT = 3
SWA_IN_FLIGHT = 4
NORM_HEAD_START = 2

NT_DIMS = (((1,), (1,)), ((), ()))
TN_DIMS = (((0,), (0,)), ((), ()))


def _dot(a, b):
    return jnp.dot(a, b, preferred_element_type=F32)


def _dot_nt(a, b):
    return lax.dot_general(a, b, NT_DIMS, preferred_element_type=F32)


def _dot_tn(a, b):
    return lax.dot_general(a, b, TN_DIMS, preferred_element_type=F32)


def _rms_scale(x):
    return lax.rsqrt(jnp.mean(x * x, axis=-1, keepdims=True) + RMS_EPS)


LOG2E = 1.4426950408889634
LN2 = 0.6931471805599453


def _sigmoid(x):
    return 1.0 / (1.0 + jnp.exp2(x * -LOG2E))


def _silu(x):
    return x * _sigmoid(x)


def _resident(shape):
    zeros = (0,) * len(shape)
    return pl.BlockSpec(shape, lambda *_: zeros, pipeline_mode=pl.Buffered(1))


def _tile(tm, width, batch0=0):
    return pl.BlockSpec((None, tm, width), lambda b, i: (batch0 + b, i, 0))


class _Block(NamedTuple):
    array: Any
    shape: tuple
    index: tuple

    def spec(self):
        index = self.index
        return pl.BlockSpec(self.shape, lambda *_: index, pipeline_mode=pl.Buffered(1))


def _col_blocks(w, col0=0, width=None):
    width = width or w.shape[1] - col0
    blk = min(WEIGHT_BLOCK, width)
    assert col0 % blk == 0 and width % blk == 0
    return tuple(_Block(w, (w.shape[0], blk), (0, (col0 + c) // blk))
                 for c in range(0, width, blk))


def _block_pieces(block_refs):
    out, base = [], 0
    for ref in block_refs:
        width = ref.shape[1]
        for c in range(0, width, MXU_WIDTH):
            hi = min(c + MXU_WIDTH, width)
            out.append((slice(base + c, base + hi), ref, slice(c, hi)))
        base += width
    return out


class _Task(NamedTuple):
    stages: Callable[..., Any]
    operands: tuple
    in_specs: tuple
    out_specs: tuple
    out_shapes: tuple
    scratch: tuple = ()
    aliases: Any = None


_DONE = object()


class _Stream:
    def __init__(self, units, max_active):
        self.pending, self.active, self.max_active = list(units), [], max_active

    def busy(self):
        return bool(self.pending or self.active)

    def step(self):
        self.active = [g for g in self.active if next(g, _DONE) is not _DONE]
        if self.pending and len(self.active) < self.max_active:
            g = self.pending.pop(0)
            if next(g, _DONE) is not _DONE:
                self.active.append(g)


def _drive(heavy, light):
    heavy = [g for g in heavy if g is not None]
    while heavy or any(s.busy() for s in light):
        heavy = [g for g in heavy if next(g, _DONE) is not _DONE]
        for s in light:
            s.step()


def _run(tasks, grid, name):
    n_in = [len(t.operands) for t in tasks]
    n_out = [len(t.out_shapes) for t in tasks]
    n_scr = [len(t.scratch) for t in tasks]

    def body(*refs):
        ins = refs[:sum(n_in)]
        outs = refs[sum(n_in):sum(n_in) + sum(n_out)]
        scrs = refs[sum(n_in) + sum(n_out):]
        heavy, light = [], []
        i = o = s = 0
        for t, ni, no, ns in zip(tasks, n_in, n_out, n_scr):
            h, l = t.stages(*ins[i:i + ni], *outs[o:o + no], *scrs[s:s + ns])
            heavy.append(h)
            light += l
            i, o, s = i + ni, o + no, s + ns
        _drive(heavy, light)

    aliases = {}
    i = o = 0
    for t, ni, no in zip(tasks, n_in, n_out):
        for src, dst in (t.aliases or {}).items():
            aliases[i + src] = o + dst
        i, o = i + ni, o + no

    flat = pl.pallas_call(
        body,
        grid=grid,
        in_specs=[s for t in tasks for s in t.in_specs],
        out_specs=tuple(s for t in tasks for s in t.out_specs),
        out_shape=tuple(s for t in tasks for s in t.out_shapes),
        scratch_shapes=[s for t in tasks for s in t.scratch],
        input_output_aliases=aliases,
        compiler_params=pltpu.CompilerParams(
            dimension_semantics=("arbitrary", "arbitrary"), vmem_limit_bytes=VMEM_LIMIT_BYTES),
        name=name,
    )(*[a for t in tasks for a in t.operands])
    results, o = [], 0
    for no in n_out:
        results.append(tuple(flat[o:o + no]))
        o += no
    return results


def _inproj_stages(x_ref, g_ref, *refs, blocks_per_group):
    n_blocks = sum(blocks_per_group)
    w_refs, o_refs = refs[:n_blocks], refs[n_blocks:]

    def pieces():
        x = x_ref[...]
        n = ((x * _rms_scale(x)) * g_ref[...]).astype(BF16)
        for _ in range(NORM_HEAD_START):
            yield
        first = 0
        for n_blk, o_ref, is_gate in zip(blocks_per_group, o_refs, IN_IS_GATE):
            for cols, w_ref, cs in _block_pieces(w_refs[first:first + n_blk]):
                u = _dot(n, w_ref[:, cs])
                o_ref[:, cols] = (_silu(u) if is_gate else u).astype(o_ref.dtype)
                yield
            first += n_blk

    return pieces(), []


IN_DTYPES = (BF16, BF16, BF16, BF16, BF16, F32, BF16, BF16)
IN_IS_GATE = (False, False, False, True, False, False, True, False)


def _inproj_task(x, batch0, n_batch, tm, g_mix, groups):
    _, length, d = x.shape
    blocks = [w for g in groups for w in g]
    widths = [sum(w.shape[1] for w in g) for g in groups]
    return _Task(
        stages=functools.partial(_inproj_stages, blocks_per_group=tuple(len(g) for g in groups)),
        operands=(x, g_mix, *[w.array for w in blocks]),
        in_specs=(_tile(tm, d, batch0), _resident(g_mix.shape), *[w.spec() for w in blocks]),
        out_specs=tuple(_tile(tm, n) for n in widths),
        out_shapes=tuple(jax.ShapeDtypeStruct((n_batch, length, n), dt)
                         for n, dt in zip(widths, IN_DTYPES)))


def _gla_stages(q_ref, k_ref, v_ref, z_ref, lr_ref, wg_ref, bg_ref, gout_ref, s0_ref,
                o_ref, s_ref, *, chunk, n_chunks):
    @pl.when(pl.program_id(1) == 0)
    def _():
        s_ref[...] = s0_ref[...]

    row = lax.broadcasted_iota(jnp.int32, (chunk, chunk), 0)
    col = lax.broadcasted_iota(jnp.int32, (chunk, chunk), 1)
    causal = row >= col
    row2 = lax.broadcasted_iota(jnp.int32, (chunk, 2 * chunk), 0)
    col2 = lax.broadcasted_iota(jnp.int32, (chunk, 2 * chunk), 1)
    tril2 = jnp.where(row2 >= lax.rem(col2, chunk), 1.0, 0.0).astype(BF16)
    eye = (lax.broadcasted_iota(jnp.int32, (GLA_DK, GLA_DK), 0)
           == lax.broadcasted_iota(jnp.int32, (GLA_DK, GLA_DK), 1))
    scale = GLA_DK ** -0.5
    heads = [(slice(h * GLA_DK, (h + 1) * GLA_DK), slice(h * GLA_DV, (h + 1) * GLA_DV))
             for h in range(GLA_HEADS)]

    def chunk_unit(c):
        r = slice(c * chunk, (c + 1) * chunk)
        gpre = _dot(lr_ref[r, :], wg_ref[...])
        yield
        gpre = gpre + bg_ref[...]
        soft = jnp.log2(1.0 + jnp.exp2(jnp.abs(gpre) * -LOG2E))
        g = jnp.minimum(gpre, 0.0) * (1.0 / GLA_TAU) - soft * (LN2 / GLA_TAU)
        g_hi = g.astype(BF16)
        g_lo = (g - g_hi.astype(F32)).astype(BF16)
        g2 = jnp.concatenate([g_hi, g_lo], axis=0)
        yield
        b = _dot(tril2, g2)
        yield
        b_last = b[chunk - 1:chunk, :]
        q = q_ref[r, :].astype(F32)
        k = k_ref[r, :].astype(F32)
        qe = ((q * scale) * jnp.exp(b)).astype(BF16)
        ke = k * jnp.exp(-b)
        kd = k * jnp.exp(b_last - b)
        decay = jnp.exp(b_last)
        v = [v_ref[r, vs] for _, vs in heads]
        ke_t = [ke[:, ks].T.astype(BF16) for ks, _ in heads]
        kd_t = [kd[:, ks].T.astype(BF16) for ks, _ in heads]
        yield
        a_raw = [_dot(qe[:, ks], ke_t[h]) for h, (ks, _) in enumerate(heads)]
        kv = [_dot(kd_t[h], v[h]) for h in range(GLA_HEADS)]
        yield
        lhs, rhs = [], []
        for h, (ks, _) in enumerate(heads):
            a = jnp.where(causal, a_raw[h], 0.0).astype(BF16)
            s = s_ref[h]
            lhs.append(jnp.concatenate([qe[:, ks], a], axis=1))
            rhs.append(jnp.concatenate([s.astype(BF16), v[h]], axis=0))
            dcol = jnp.sum(jnp.where(eye, decay[:, ks], 0.0), axis=1, keepdims=True)
            s_ref[h] = dcol * s + kv[h]
        yield
        o = [_dot(lhs[h], rhs[h]) for h in range(GLA_HEADS)]
        yield
        for h, (_, vs) in enumerate(heads):
            oh = (o[h] * _rms_scale(o[h])) * gout_ref[...]
            o_ref[r, vs] = (oh * z_ref[r, vs].astype(F32)).astype(o_ref.dtype)

    return None, [_Stream([chunk_unit(c) for c in range(n_chunks)], GLA_IN_FLIGHT)]


def _gla_task(q, k, v, z, lr, wg, bg, gout, s0, state_batch0, chunk, tb):
    n_batch, length, _ = q.shape
    st_in = pl.BlockSpec((None, GLA_HEADS, GLA_DK, GLA_DV),
                         lambda b, i: (state_batch0 + b, 0, 0, 0))
    st_out = pl.BlockSpec((None, GLA_HEADS, GLA_DK, GLA_DV), lambda b, i: (b, 0, 0, 0))
    return _Task(
        stages=functools.partial(_gla_stages, chunk=chunk, n_chunks=tb // chunk),
        operands=(q, k, v, z, lr, wg, bg, gout, s0),
        in_specs=(_tile(tb, GLA_KW), _tile(tb, GLA_KW), _tile(tb, GLA_WIDTH), _tile(tb, GLA_WIDTH),
                  _tile(tb, LANES), _resident(wg.shape), _resident(bg.shape),
                  _resident(gout.shape), st_in),
        out_specs=(_tile(tb, GLA_WIDTH), st_out),
        out_shapes=(jax.ShapeDtypeStruct((n_batch, length, GLA_WIDTH), BF16),
                    jax.ShapeDtypeStruct((n_batch, GLA_HEADS, GLA_DK, GLA_DV), F32)))


def _swa_stages(sink_ref, q_ref, kp_ref, vp_ref, kc_ref, vc_ref, z_ref, o_ref,
                *, chunk, n_chunks, mask_missing_past):
    kvw = SWA_GROUP * SWA_HEAD_DIM
    n_keys = WINDOW + chunk
    gq = SWA_GROUP * chunk
    gq_group = lax.broadcasted_iota(jnp.int32, (1, gq), 1) // chunk
    key_i = lax.broadcasted_iota(jnp.int32, (n_keys, gq), 0)
    qscale = SWA_HEAD_DIM ** -0.5
    first_pos = pl.program_id(1) * (chunk * n_chunks) - WINDOW
    shared = {}

    def head_operands(h):
        if not shared:
            shared["k"] = jnp.concatenate([kp_ref[...], kc_ref[...]], axis=0).astype(BF16)
            v = jnp.concatenate([vp_ref[...], vc_ref[...]], axis=0)
            shared["vt"] = v.T.astype(BF16)
        if h not in shared:
            hd = slice(h * SWA_HEAD_DIM, (h + 1) * SWA_HEAD_DIM)
            k_h = shared["k"][:, hd]
            vt = shared["vt"][hd, :]
            vt_at = {}
            for c in range(n_chunks):
                shift = (c * chunk) % LANES
                if shift not in vt_at:
                    vt_at[shift] = vt[:, shift:]
            sink = jnp.zeros((1, gq), F32)
            for g in range(SWA_GROUP):
                sink = jnp.where(gq_group == g, sink_ref[h * SWA_GROUP + g], sink)
            shared[h] = (k_h, vt_at, sink)
        return shared[h]

    def unit(h, c):
        k_h, vt_at, sink = head_operands(h)
        rows = slice(c * chunk, (c + 1) * chunk)
        keys = slice(c * chunk, c * chunk + n_keys)
        cols = slice(h * kvw, (h + 1) * kvw)
        qt = q_ref[rows, cols].astype(F32).T
        qg = jnp.concatenate(
            [qt[g * SWA_HEAD_DIM:(g + 1) * SWA_HEAD_DIM] for g in range(SWA_GROUP)],
            axis=1).astype(BF16)
        shift = (c * chunk) % LANES
        vt_keys = vt_at[shift][:, c * chunk - shift:c * chunk - shift + n_keys]
        yield
        s = _dot(k_h[keys], qg)
        yield
        if mask_missing_past and c * chunk < WINDOW:
            s = jnp.where(key_i + (first_pos + c * chunk) >= 0, s, jnp.finfo(F32).min)
        m = jnp.maximum(jnp.max(s, axis=0, keepdims=True) * qscale, sink) * LOG2E
        p = jnp.exp2(s * (qscale * LOG2E) - m)
        denom = jnp.sum(p, axis=0, keepdims=True) + jnp.exp2(sink * LOG2E - m)
        p = p.astype(BF16)
        yield
        o_t = _dot(vt_keys, p)
        yield
        o_t = (o_t / denom).T
        o = jnp.concatenate([o_t[g * chunk:(g + 1) * chunk] for g in range(SWA_GROUP)],
                            axis=1)
        o_ref[rows, cols] = (o * z_ref[rows, cols].astype(F32)).astype(o_ref.dtype)

    units = [unit(h, c) for h in range(SWA_KV_HEADS) for c in range(n_chunks)]
    return None, [_Stream(units, SWA_IN_FLIGHT)]


def _swa_task(sinks, q, z, kv, past, chunk, qb):
    n_batch, length, _ = q.shape
    if past is None:
        per_win = qb // WINDOW
        prev = lambda col: pl.BlockSpec(
            (None, WINDOW, SWA_KVW), lambda b, i: (b, jnp.maximum(i * per_win - 1, 0), col))
        k_past, v_past, kp_spec, vp_spec = kv, kv, prev(0), prev(1)
    else:
        k_past, v_past = past
        kp_spec = vp_spec = pl.BlockSpec((None, WINDOW, SWA_KVW), lambda b, i: (b, 0, 0))
    cur = lambda col: pl.BlockSpec((None, qb, SWA_KVW), lambda b, i: (b, i, col))
    return _Task(
        stages=functools.partial(_swa_stages, chunk=chunk, n_chunks=qb // chunk,
                                 mask_missing_past=past is None),
        operands=(sinks, q, k_past, v_past, kv, kv, z),
        in_specs=(pl.BlockSpec(memory_space=pltpu.SMEM), _tile(qb, SWA_WIDTH), kp_spec, vp_spec,
                  cur(0), cur(1), _tile(qb, SWA_WIDTH)),
        out_specs=(_tile(qb, SWA_WIDTH),),
        out_shapes=(jax.ShapeDtypeStruct((n_batch, length, SWA_WIDTH), BF16),))


def _outproj_stages(x_ref, o1_ref, o2_ref, p_ref, gple_ref, gfin_ref, *refs, n_blocks):
    wo, wg, wp = (refs[j * n_blocks:(j + 1) * n_blocks] for j in range(3))
    y_ref, h_ref = refs[-2:]

    def pieces():
        o = jnp.concatenate([o1_ref[...], o2_ref[...]], axis=1)
        n = []
        for cols, w_ref, cs in _block_pieces(wo):
            h = x_ref[:, cols] + _dot(o, w_ref[:, cs])
            h_ref[:, cols] = h
            n.append((h * gple_ref[:, cols]).astype(BF16))
            yield
        n = jnp.concatenate(n, axis=1)
        r = _rms_scale(h_ref[...]) * -LOG2E
        p = p_ref[...].astype(BF16)
        tm = h_ref.shape[0]
        for rows in [slice(r0, r0 + OUT_ROW_GROUP) for r0 in range(0, tm, OUT_ROW_GROUP)]:
            for (cols, wg_ref, cs), (_, wp_ref, _) in zip(_block_pieces(wg), _block_pieces(wp)):
                gate = 1.0 / (1.0 + jnp.exp2(_dot(n[rows], wg_ref[:, cs]) * r[rows]))
                h_ref[rows, cols] = h_ref[rows, cols] + gate * _dot(p[rows], wp_ref[:, cs])
                yield
            h = h_ref[rows, :]
            y_ref[rows, :] = (h * _rms_scale(h)) * gfin_ref[...]

    return pieces(), []


def _outproj_task(x, o1, o2, p, batch0, tm, wo, gple, wg, wp, gfin, y_prev=None):
    _, _, d = x.shape
    assert len(wo) == len(wg) == len(wp)
    blocks = (*wo, *wg, *wp)
    operands = (x, o1, o2, p, gple, gfin, *[w.array for w in blocks])
    in_specs = (_tile(tm, d, batch0), _tile(tm, o1.shape[2]), _tile(tm, o2.shape[2]),
                _tile(tm, p.shape[2], batch0), _resident(gple.shape), _resident(gfin.shape),
                *[w.spec() for w in blocks])
    aliases = None
    if y_prev is not None:
        aliases = {len(operands): 0}
        operands += (y_prev,)
        in_specs += (pl.BlockSpec(memory_space=pl.ANY),)
    return _Task(stages=functools.partial(_outproj_stages, n_blocks=len(wo)),
                 operands=operands, in_specs=in_specs,
                 out_specs=(_tile(tm, d, batch0),),
                 out_shapes=(jax.ShapeDtypeStruct(x.shape, F32),),
                 scratch=(pltpu.VMEM((tm, d), F32),), aliases=aliases)


def _split_w_in(w_in):
    lr0 = 2 * GLA_KW + 2 * GLA_WIDTH
    n_rows, n_in = w_in.shape
    n_main = n_in - GLA_RANK
    assert lr0 % WEIGHT_BLOCK == 0 and n_main % WEIGHT_BLOCK == 0

    def relayout(w_ref, o_ref):
        o_ref[...] = w_ref[...].T.astype(BF16)

    assert WEIGHT_BLOCK % GLA_RANK == 0
    src_col = lambda j: GLA_RANK * (j * (WEIGHT_BLOCK // GLA_RANK)
                                    + (j >= lr0 // WEIGHT_BLOCK).astype(jnp.int32))
    w = pl.pallas_call(
        relayout,
        grid=(n_main // WEIGHT_BLOCK,),
        in_specs=[pl.BlockSpec((pl.Element(WEIGHT_BLOCK), pl.Element(n_rows)),
                               lambda j: (src_col(j), 0))],
        out_specs=pl.BlockSpec((n_rows, WEIGHT_BLOCK), lambda j: (0, j)),
        out_shape=jax.ShapeDtypeStruct((n_rows, n_main), BF16),
        compiler_params=pltpu.CompilerParams(
            dimension_semantics=("arbitrary",), vmem_limit_bytes=VMEM_LIMIT_BYTES),
        name="w_in_relayout",
    )(w_in.T)

    def relayout_gate(w_ref, o_ref):
        pad = jnp.zeros((LANES - GLA_RANK, n_rows), F32)
        o_ref[...] = jnp.concatenate([w_ref[...], pad], axis=0).T.astype(BF16)

    w_lr = pl.pallas_call(
        relayout_gate,
        grid=(1,),
        in_specs=[pl.BlockSpec((pl.Element(GLA_RANK), pl.Element(n_rows)), lambda j: (lr0, 0))],
        out_specs=pl.BlockSpec((n_rows, LANES), lambda j: (0, 0)),
        out_shape=jax.ShapeDtypeStruct((n_rows, LANES), BF16),
        name="w_gate_relayout",
    )(w_in.T)
    widths = (GLA_KW, GLA_KW, GLA_WIDTH, GLA_WIDTH, SWA_WIDTH, 2 * SWA_KVW, SWA_WIDTH)
    groups, col0 = [], 0
    for width in widths:
        groups.append(_col_blocks(w, col0, width))
        col0 += width
    return (*groups, _col_blocks(w_lr))


class _Weights(NamedTuple):
    g_mix: Any
    w_in: tuple
    w_gate: Any
    b_gate: Any
    g_gla_out: Any
    sinks: Any
    w_out: Any
    g_ple: Any
    w_ple_gate: Any
    w_ple_proj: Any
    g_final: Any


def _attn_tasks(w, proj, s0, state_batch0, past, chunk, tile):
    q1, k1, v1, z1, q2, kv2, z2, lr = proj
    return [_gla_task(q1, k1, v1, z1, lr, w.w_gate, w.b_gate, w.g_gla_out, s0, state_batch0,
                      chunk, tile),
            _swa_task(w.sinks, q2, z2, kv2, past, chunk, tile)]


def _prompt_path(w, x, p, s0):
    n_batch, length, _ = x.shape
    assert n_batch == 2, "a call holding all three stages would not fit the VMEM budget"
    proj, attn = {}, {}
    y = None
    for s in range(n_batch + 2):
        has_in, has_attn, has_out = s < n_batch, 1 <= s <= n_batch, s >= 2
        tile = TILE_FUSED if has_attn else TILE_MATMUL_ONLY
        tasks, names = [], []
        if has_attn:
            tasks += _attn_tasks(w, proj[s - 1], s0, s - 1, None, CHUNK, tile)
            names.append("attn")
        if has_in:
            tasks.append(_inproj_task(x, s, 1, tile, w.g_mix, w.w_in))
            names.append("inproj")
        if has_out:
            o1, o2, _ = attn[s - 2]
            tasks.append(_outproj_task(x, o1, o2, p, s - 2, tile, w.w_out, w.g_ple,
                                       w.w_ple_gate, w.w_ple_proj, w.g_final, y_prev=y))
            names.append("outproj")
        res = _run(tasks, (1, length // tile), "_".join(names))
        if has_attn:
            (o1, s_new), (o2,) = res[0], res[1]
            attn[s - 1] = (o1, o2, s_new)
            res = res[2:]
        if has_in:
            proj[s] = res[0]
            res = res[1:]
        if has_out:
            (y,) = res[0]
    states = jnp.concatenate([attn[b][2] for b in range(n_batch)], axis=0)
    kv = [proj[b][5] for b in range(n_batch)]
    return y, states, kv


def _sample_path(w, x, p, s0, past):
    n_batch, length, d = x.shape
    flat = lambda a: a.reshape(1, n_batch * length, a.shape[-1])
    per_batch = lambda a: a.reshape(n_batch, length, a.shape[-1])
    (proj,) = _run([_inproj_task(flat(x), 0, 1, n_batch * length, w.g_mix, w.w_in)], (1, 1),
                   "inproj_sample")
    proj = tuple(per_batch(a) for a in proj)
    (o1, s_new), (o2,) = _run(_attn_tasks(w, proj, s0, 0, past, min(CHUNK, length), length),
                              (n_batch, 1), "attn_sample")
    ((y,),) = _run([_outproj_task(flat(x), flat(o1), flat(o2), flat(p), 0, n_batch * length,
                                  w.w_out, w.g_ple, w.w_ple_gate, w.w_ple_proj, w.g_final)],
                   (1, 1), "outproj_sample")
    return y.reshape(n_batch, length, d), s_new, proj[5]


def kernel(x_prompt, x_sample, p_prompt, p_sample, state_gla, cache_swa_k, cache_swa_v, g_mix, w_in, w_gate_up, b_gate, g_gla_out, swa_sinks, w_out, g_ple, w_ple_gate, w_ple_proj, g_final):
    depth = w_in.shape[0]
    assert depth == 1, "the final RMSNorm is fused into the last layer's output kernel"
    bp, lp, _ = x_prompt.shape
    bs, ls, _ = x_sample.shape
    assert cache_swa_k.shape[2] == WINDOW
    i = 0
    row = lambda a: a.reshape(1, -1).astype(F32)
    w = _Weights(
        g_mix=row(g_mix[i]), w_in=_split_w_in(w_in[i]),
        w_gate=jnp.pad(w_gate_up[i].astype(BF16), ((0, LANES - GLA_RANK), (0, 0))),
        b_gate=row(b_gate[i]), g_gla_out=row(g_gla_out[i]),
        sinks=swa_sinks[i].reshape(-1).astype(F32),
        w_out=_col_blocks(w_out[i].astype(BF16)),
        g_ple=row(g_ple[i]), w_ple_gate=_col_blocks(w_ple_gate[i].astype(BF16)),
        w_ple_proj=_col_blocks(w_ple_proj[i].astype(BF16)), g_final=row(g_final))

    s0_p = jnp.zeros((bp, GLA_HEADS, GLA_DK, GLA_DV), F32)
    y_p, s_p, kv_p = _prompt_path(w, x_prompt, p_prompt[i], s0_p)
    keep = min(WINDOW, lp)
    last = lambda col: jnp.concatenate(
        [a[:, lp - keep:, col * SWA_KVW:(col + 1) * SWA_KVW] for a in kv_p], axis=0
    ).reshape(bp, keep, SWA_KV_HEADS, SWA_HEAD_DIM)
    k_p, v_p = last(0), last(1)

    past = (cache_swa_k[i].reshape(bs, WINDOW, SWA_KVW), cache_swa_v[i].reshape(bs, WINDOW, SWA_KVW))
    y_s, s_s, kv_s = _sample_path(w, x_sample, p_sample[i], state_gla[i], past)
    k_s = kv_s[:, :, :SWA_KVW].reshape(bs, ls, SWA_KV_HEADS, SWA_HEAD_DIM)
    v_s = kv_s[:, :, SWA_KVW:].reshape(bs, ls, SWA_KV_HEADS, SWA_HEAD_DIM)
    return (y_p, y_s, s_p[None], k_p[None], v_p[None], s_s[None], k_s[None], v_s[None])
```

```python
import functools
from typing import Any, Callable, NamedTuple

import jax
import jax.numpy as jnp
from jax import lax
from jax.experimental import pallas as pl
from jax.experimental.pallas import tpu as pltpu

F32 = jnp.float32
BF16 = jnp.bfloat16

CHUNK = 64
RMS_EPS = 1e-6
GLA_HEADS = 4
GLA_DK = 128
GLA_DV = 256
GLA_KW = GLA_HEADS * GLA_DK
GLA_WIDTH = GLA_HEADS * GLA_DV
GLA_RANK = 16
GLA_TAU = 16.0
SWA_KV_HEADS = 4
SWA_GROUP = 4
SWA_HEAD_DIM = 64
SWA_WIDTH = SWA_KV_HEADS * SWA_GROUP * SWA_HEAD_DIM
SWA_KVW = SWA_KV_HEADS * SWA_HEAD_DIM
WINDOW = 128

LANES = 128
MXU_WIDTH = 256
WEIGHT_BLOCK = 512
VMEM_LIMIT_BYTES = 56 * 1024 * 1024
TILE_MATMUL_ONLY = 512
TILE_FUSED = 256
OUT_ROW_GROUP = 256
GLA_IN_FLIGHT = 3
SWA_IN_FLIGHT = 4
NORM_HEAD_START = 2

LOG2E = 1.4426950408889634
LN2 = 0.6931471805599453


def _dot(a, b):
    return jnp.dot(a, b, preferred_element_type=F32)


def _rms_scale(x):
    return lax.rsqrt(jnp.mean(x * x, axis=-1, keepdims=True) + RMS_EPS)


def _sigmoid(x):
    return 1.0 / (1.0 + jnp.exp2(x * -LOG2E))


def _silu(x):
    return x * _sigmoid(x)


def _resident(shape):
    zeros = (0,) * len(shape)
    return pl.BlockSpec(shape, lambda *_: zeros, pipeline_mode=pl.Buffered(1))


def _tile(tm, width, batch0=0):
    return pl.BlockSpec((None, tm, width), lambda b, i: (batch0 + b, i, 0))


class _Block(NamedTuple):
    array: Any
    shape: tuple
    index: tuple

    def spec(self):
        index = self.index
        return pl.BlockSpec(self.shape, lambda *_: index, pipeline_mode=pl.Buffered(1))


def _col_blocks(w, col0=0, width=None):
    width = width or w.shape[1] - col0
    blk = min(WEIGHT_BLOCK, width)
    assert col0 % blk == 0 and width % blk == 0
    return tuple(_Block(w, (w.shape[0], blk), (0, (col0 + c) // blk))
                 for c in range(0, width, blk))


def _block_pieces(block_refs):
    out, base = [], 0
    for ref in block_refs:
        width = ref.shape[1]
        for c in range(0, width, MXU_WIDTH):
            hi = min(c + MXU_WIDTH, width)
            out.append((slice(base + c, base + hi), ref, slice(c, hi)))
        base += width
    return out


class _Task(NamedTuple):
    stages: Callable[..., Any]
    operands: tuple
    in_specs: tuple
    out_specs: tuple
    out_shapes: tuple
    scratch: tuple = ()
    aliases: Any = None


_DONE = object()


class _Stream:
    def __init__(self, units, max_active):
        self.pending, self.active, self.max_active = list(units), [], max_active

    def busy(self):
        return bool(self.pending or self.active)

    def step(self):
        self.active = [g for g in self.active if next(g, _DONE) is not _DONE]
        if self.pending and len(self.active) < self.max_active:
            g = self.pending.pop(0)
            if next(g, _DONE) is not _DONE:
                self.active.append(g)


def _drive(heavy, light):
    heavy = [g for g in heavy if g is not None]
    while heavy or any(s.busy() for s in light):
        heavy = [g for g in heavy if next(g, _DONE) is not _DONE]
        for s in light:
            s.step()


def _run(tasks, grid, name):
    n_in = [len(t.operands) for t in tasks]
    n_out = [len(t.out_shapes) for t in tasks]
    n_scr = [len(t.scratch) for t in tasks]

    def body(*refs):
        ins = refs[:sum(n_in)]
        outs = refs[sum(n_in):sum(n_in) + sum(n_out)]
        scrs = refs[sum(n_in) + sum(n_out):]
        heavy, light = [], []
        i = o = s = 0
        for t, ni, no, ns in zip(tasks, n_in, n_out, n_scr):
            h, l = t.stages(*ins[i:i + ni], *outs[o:o + no], *scrs[s:s + ns])
            heavy.append(h)
            light += l
            i, o, s = i + ni, o + no, s + ns
        _drive(heavy, light)

    aliases = {}
    i = o = 0
    for t, ni, no in zip(tasks, n_in, n_out):
        for src, dst in (t.aliases or {}).items():
            aliases[i + src] = o + dst
        i, o = i + ni, o + no

    flat = pl.pallas_call(
        body,
        grid=grid,
        in_specs=[s for t in tasks for s in t.in_specs],
        out_specs=tuple(s for t in tasks for s in t.out_specs),
        out_shape=tuple(s for t in tasks for s in t.out_shapes),
        scratch_shapes=[s for t in tasks for s in t.scratch],
        input_output_aliases=aliases,
        compiler_params=pltpu.CompilerParams(
            dimension_semantics=("arbitrary", "arbitrary"), vmem_limit_bytes=VMEM_LIMIT_BYTES),
        name=name,
    )(*[a for t in tasks for a in t.operands])
    results, o = [], 0
    for no in n_out:
        results.append(tuple(flat[o:o + no]))
        o += no
    return results


def _inproj_stages(x_ref, g_ref, *refs, blocks_per_group):
    n_blocks = sum(blocks_per_group)
    w_refs, o_refs = refs[:n_blocks], refs[n_blocks:]

    def pieces():
        x = x_ref[...]
        n = ((x * _rms_scale(x)) * g_ref[...]).astype(BF16)
        for _ in range(NORM_HEAD_START):
            yield
        first = 0
        for n_blk, o_ref, is_gate in zip(blocks_per_group, o_refs, IN_IS_GATE):
            for cols, w_ref, cs in _block_pieces(w_refs[first:first + n_blk]):
                u = _dot(n, w_ref[:, cs])
                o_ref[:, cols] = (_silu(u) if is_gate else u).astype(o_ref.dtype)
                yield
            first += n_blk

    return pieces(), []


IN_DTYPES = (BF16, BF16, BF16, BF16, BF16, F32, BF16, BF16)
IN_IS_GATE = (False, False, False, True, False, False, True, False)


def _inproj_task(x, batch0, n_batch, tm, g_mix, groups):
    _, length, d = x.shape
    blocks = [w for g in groups for w in g]
    widths = [sum(w.shape[1] for w in g) for g in groups]
    return _Task(
        stages=functools.partial(_inproj_stages, blocks_per_group=tuple(len(g) for g in groups)),
        operands=(x, g_mix, *[w.array for w in blocks]),
        in_specs=(_tile(tm, d, batch0), _resident(g_mix.shape), *[w.spec() for w in blocks]),
        out_specs=tuple(_tile(tm, n) for n in widths),
        out_shapes=tuple(jax.ShapeDtypeStruct((n_batch, length, n), dt)
                         for n, dt in zip(widths, IN_DTYPES)))


def _gla_stages(q_ref, k_ref, v_ref, z_ref, lr_ref, wg_ref, bg_ref, gout_ref, s0_ref,
                o_ref, s_ref, *, chunk, n_chunks):
    @pl.when(pl.program_id(1) == 0)
    def _():
        s_ref[...] = s0_ref[...]

    row = lax.broadcasted_iota(jnp.int32, (chunk, chunk), 0)
    col = lax.broadcasted_iota(jnp.int32, (chunk, chunk), 1)
    causal = row >= col
    row2 = lax.broadcasted_iota(jnp.int32, (chunk, 2 * chunk), 0)
    col2 = lax.broadcasted_iota(jnp.int32, (chunk, 2 * chunk), 1)
    tril2 = jnp.where(row2 >= lax.rem(col2, chunk), 1.0, 0.0).astype(BF16)
    eye = (lax.broadcasted_iota(jnp.int32, (GLA_DK, GLA_DK), 0)
           == lax.broadcasted_iota(jnp.int32, (GLA_DK, GLA_DK), 1))
    scale = GLA_DK ** -0.5
    heads = [(slice(h * GLA_DK, (h + 1) * GLA_DK), slice(h * GLA_DV, (h + 1) * GLA_DV))
             for h in range(GLA_HEADS)]

    def chunk_unit(c):
        r = slice(c * chunk, (c + 1) * chunk)
        gpre = _dot(lr_ref[r, :], wg_ref[...])
        yield
        gpre = gpre + bg_ref[...]
        soft = jnp.log2(1.0 + jnp.exp2(jnp.abs(gpre) * -LOG2E))
        g = jnp.minimum(gpre, 0.0) * (1.0 / GLA_TAU) - soft * (LN2 / GLA_TAU)
        g_hi = g.astype(BF16)
        g_lo = (g - g_hi.astype(F32)).astype(BF16)
        g2 = jnp.concatenate([g_hi, g_lo], axis=0)
        yield
        b = _dot(tril2, g2)
        yield
        b_last = b[chunk - 1:chunk, :]
        q = q_ref[r, :].astype(F32)
        k = k_ref[r, :].astype(F32)
        qe = ((q * scale) * jnp.exp(b)).astype(BF16)
        ke = k * jnp.exp(-b)
        kd = k * jnp.exp(b_last - b)
        decay = jnp.exp(b_last)
        v = [v_ref[r, vs] for _, vs in heads]
        ke_t = [ke[:, ks].T.astype(BF16) for ks, _ in heads]
        kd_t = [kd[:, ks].T.astype(BF16) for ks, _ in heads]
        yield
        a_raw = [_dot(qe[:, ks], ke_t[h]) for h, (ks, _) in enumerate(heads)]
        kv = [_dot(kd_t[h], v[h]) for h in range(GLA_HEADS)]
        yield
        lhs, rhs = [], []
        for h, (ks, _) in enumerate(heads):
            a = jnp.where(causal, a_raw[h], 0.0).astype(BF16)
            s = s_ref[h]
            lhs.append(jnp.concatenate([qe[:, ks], a], axis=1))
            rhs.append(jnp.concatenate([s.astype(BF16), v[h]], axis=0))
            dcol = jnp.sum(jnp.where(eye, decay[:, ks], 0.0), axis=1, keepdims=True)
            s_ref[h] = dcol * s + kv[h]
        yield
        o = [_dot(lhs[h], rhs[h]) for h in range(GLA_HEADS)]
        yield
        for h, (_, vs) in enumerate(heads):
            oh = (o[h] * _rms_scale(o[h])) * gout_ref[...]
            o_ref[r, vs] = (oh * z_ref[r, vs].astype(F32)).astype(o_ref.dtype)

    return None, [_Stream([chunk_unit(c) for c in range(n_chunks)], GLA_IN_FLIGHT)]


def _gla_task(q, k, v, z, lr, wg, bg, gout, s0, state_batch0, chunk, tb):
    n_batch, length, _ = q.shape
    st_in = pl.BlockSpec((None, GLA_HEADS, GLA_DK, GLA_DV),
                         lambda b, i: (state_batch0 + b, 0, 0, 0))
    st_out = pl.BlockSpec((None, GLA_HEADS, GLA_DK, GLA_DV), lambda b, i: (b, 0, 0, 0))
    return _Task(
        stages=functools.partial(_gla_stages, chunk=chunk, n_chunks=tb // chunk),
        operands=(q, k, v, z, lr, wg, bg, gout, s0),
        in_specs=(_tile(tb, GLA_KW), _tile(tb, GLA_KW), _tile(tb, GLA_WIDTH), _tile(tb, GLA_WIDTH),
                  _tile(tb, LANES), _resident(wg.shape), _resident(bg.shape),
                  _resident(gout.shape), st_in),
        out_specs=(_tile(tb, GLA_WIDTH), st_out),
        out_shapes=(jax.ShapeDtypeStruct((n_batch, length, GLA_WIDTH), BF16),
                    jax.ShapeDtypeStruct((n_batch, GLA_HEADS, GLA_DK, GLA_DV), F32)))


def _swa_stages(sink_ref, q_ref, kp_ref, vp_ref, kc_ref, vc_ref, z_ref, o_ref,
                *, chunk, n_chunks, mask_missing_past):
    kvw = SWA_GROUP * SWA_HEAD_DIM
    n_keys = WINDOW + chunk
    gq = SWA_GROUP * chunk
    gq_group = lax.broadcasted_iota(jnp.int32, (1, gq), 1) // chunk
    key_i = lax.broadcasted_iota(jnp.int32, (n_keys, gq), 0)
    qscale = SWA_HEAD_DIM ** -0.5
    first_pos = pl.program_id(1) * (chunk * n_chunks) - WINDOW
    shared = {}

    def head_operands(h):
        if not shared:
            shared["k"] = jnp.concatenate([kp_ref[...], kc_ref[...]], axis=0).astype(BF16)
            v = jnp.concatenate([vp_ref[...], vc_ref[...]], axis=0)
            shared["vt"] = v.T.astype(BF16)
        if h not in shared:
            hd = slice(h * SWA_HEAD_DIM, (h + 1) * SWA_HEAD_DIM)
            k_h = shared["k"][:, hd]
            vt = shared["vt"][hd, :]
            vt_at = {}
            for c in range(n_chunks):
                shift = (c * chunk) % LANES
                if shift not in vt_at:
                    vt_at[shift] = vt[:, shift:]
            sink = jnp.zeros((1, gq), F32)
            for g in range(SWA_GROUP):
                sink = jnp.where(gq_group == g, sink_ref[h * SWA_GROUP + g], sink)
            shared[h] = (k_h, vt_at, sink)
        return shared[h]

    def unit(h, c):
        k_h, vt_at, sink = head_operands(h)
        rows = slice(c * chunk, (c + 1) * chunk)
        keys = slice(c * chunk, c * chunk + n_keys)
        cols = slice(h * kvw, (h + 1) * kvw)
        qt = q_ref[rows, cols].astype(F32).T
        qg = jnp.concatenate(
            [qt[g * SWA_HEAD_DIM:(g + 1) * SWA_HEAD_DIM] for g in range(SWA_GROUP)],
            axis=1).astype(BF16)
        shift = (c * chunk) % LANES
        vt_keys = vt_at[shift][:, c * chunk - shift:c * chunk - shift + n_keys]
        yield
        s = _dot(k_h[keys], qg)
        yield
        if mask_missing_past and c * chunk < WINDOW:
            s = jnp.where(key_i + (first_pos + c * chunk) >= 0, s, jnp.finfo(F32).min)
        m = jnp.maximum(jnp.max(s, axis=0, keepdims=True) * qscale, sink) * LOG2E
        p = jnp.exp2(s * (qscale * LOG2E) - m)
        denom = jnp.sum(p, axis=0, keepdims=True) + jnp.exp2(sink * LOG2E - m)
        p = p.astype(BF16)
        yield
        o_t = _dot(vt_keys, p)
        yield
        o_t = (o_t / denom).T
        o = jnp.concatenate([o_t[g * chunk:(g + 1) * chunk] for g in range(SWA_GROUP)],
                            axis=1)
        o_ref[rows, cols] = (o * z_ref[rows, cols].astype(F32)).astype(o_ref.dtype)

    units = [unit(h, c) for h in range(SWA_KV_HEADS) for c in range(n_chunks)]
    return None, [_Stream(units, SWA_IN_FLIGHT)]


def _swa_task(sinks, q, z, kv, past, chunk, qb):
    n_batch, length, _ = q.shape
    if past is None:
        per_win = qb // WINDOW
        prev = lambda col: pl.BlockSpec(
            (None, WINDOW, SWA_KVW), lambda b, i: (b, jnp.maximum(i * per_win - 1, 0), col))
        k_past, v_past, kp_spec, vp_spec = kv, kv, prev(0), prev(1)
    else:
        k_past, v_past = past
        kp_spec = vp_spec = pl.BlockSpec((None, WINDOW, SWA_KVW), lambda b, i: (b, 0, 0))
    cur = lambda col: pl.BlockSpec((None, qb, SWA_KVW), lambda b, i: (b, i, col))
    return _Task(
        stages=functools.partial(_swa_stages, chunk=chunk, n_chunks=qb // chunk,
                                 mask_missing_past=past is None),
        operands=(sinks, q, k_past, v_past, kv, kv, z),
        in_specs=(pl.BlockSpec(memory_space=pltpu.SMEM), _tile(qb, SWA_WIDTH), kp_spec, vp_spec,
                  cur(0), cur(1), _tile(qb, SWA_WIDTH)),
        out_specs=(_tile(qb, SWA_WIDTH),),
        out_shapes=(jax.ShapeDtypeStruct((n_batch, length, SWA_WIDTH), BF16),))


def _outproj_stages(x_ref, o1_ref, o2_ref, p_ref, gple_ref, gfin_ref, *refs, n_blocks):
    wo, wg, wp = (refs[j * n_blocks:(j + 1) * n_blocks] for j in range(3))
    y_ref = refs[-1]
    h_ref = y_ref

    def pieces():
        o = jnp.concatenate([o1_ref[...], o2_ref[...]], axis=1)
        n = []
        for cols, w_ref, cs in _block_pieces(wo):
            h = x_ref[:, cols] + _dot(o, w_ref[:, cs])
            h_ref[:, cols] = h
            n.append((h * gple_ref[:, cols]).astype(BF16))
            yield
        n = jnp.concatenate(n, axis=1)
        r = _rms_scale(h_ref[...]) * -LOG2E
        p = p_ref[...].astype(BF16)
        tm = h_ref.shape[0]
        for rows in [slice(r0, r0 + OUT_ROW_GROUP) for r0 in range(0, tm, OUT_ROW_GROUP)]:
            for (cols, wg_ref, cs), (_, wp_ref, _) in zip(_block_pieces(wg), _block_pieces(wp)):
                gate = 1.0 / (1.0 + jnp.exp2(_dot(n[rows], wg_ref[:, cs]) * r[rows]))
                h_ref[rows, cols] = h_ref[rows, cols] + gate * _dot(p[rows], wp_ref[:, cs])
                yield
            h = h_ref[rows, :]
            y_ref[rows, :] = (h * _rms_scale(h)) * gfin_ref[...]

    return pieces(), []


def _outproj_task(x, o1, o2, p, batch0, tm, wo, gple, wg, wp, gfin, y_prev=None):
    _, _, d = x.shape
    assert len(wo) == len(wg) == len(wp)
    blocks = (*wo, *wg, *wp)
    operands = (x, o1, o2, p, gple, gfin, *[w.array for w in blocks])
    in_specs = (_tile(tm, d, batch0), _tile(tm, o1.shape[2]), _tile(tm, o2.shape[2]),
                _tile(tm, p.shape[2], batch0), _resident(gple.shape), _resident(gfin.shape),
                *[w.spec() for w in blocks])
    aliases = None
    if y_prev is not None:
        aliases = {len(operands): 0}
        operands += (y_prev,)
        in_specs += (pl.BlockSpec(memory_space=pl.ANY),)
    return _Task(stages=functools.partial(_outproj_stages, n_blocks=len(wo)),
                 operands=operands, in_specs=in_specs,
                 out_specs=(_tile(tm, d, batch0),),
                 out_shapes=(jax.ShapeDtypeStruct(x.shape, F32),),
                 aliases=aliases)


def _split_w_in(w_in):
    lr0 = 2 * GLA_KW + 2 * GLA_WIDTH
    n_rows, n_in = w_in.shape
    n_main = n_in - GLA_RANK
    assert lr0 % WEIGHT_BLOCK == 0 and n_main % WEIGHT_BLOCK == 0

    def relayout(w_ref, o_ref):
        o_ref[...] = w_ref[...].T.astype(BF16)

    assert WEIGHT_BLOCK % GLA_RANK == 0
    src_col = lambda j: GLA_RANK * (j * (WEIGHT_BLOCK // GLA_RANK)
                                    + (j >= lr0 // WEIGHT_BLOCK).astype(jnp.int32))
    w = pl.pallas_call(
        relayout,
        grid=(n_main // WEIGHT_BLOCK,),
        in_specs=[pl.BlockSpec((pl.Element(WEIGHT_BLOCK), pl.Element(n_rows)),
                               lambda j: (src_col(j), 0))],
        out_specs=pl.BlockSpec((n_rows, WEIGHT_BLOCK), lambda j: (0, j)),
        out_shape=jax.ShapeDtypeStruct((n_rows, n_main), BF16),
        compiler_params=pltpu.CompilerParams(
            dimension_semantics=("arbitrary",), vmem_limit_bytes=VMEM_LIMIT_BYTES),
        name="w_in_relayout",
    )(w_in.T)

    def relayout_gate(w_ref, o_ref):
        pad = jnp.zeros((LANES - GLA_RANK, n_rows), F32)
        o_ref[...] = jnp.concatenate([w_ref[...], pad], axis=0).T.astype(BF16)

    w_lr = pl.pallas_call(
        relayout_gate,
        grid=(1,),
        in_specs=[pl.BlockSpec((pl.Element(GLA_RANK), pl.Element(n_rows)), lambda j: (lr0, 0))],
        out_specs=pl.BlockSpec((n_rows, LANES), lambda j: (0, 0)),
        out_shape=jax.ShapeDtypeStruct((n_rows, LANES), BF16),
        name="w_gate_relayout",
    )(w_in.T)
    widths = (GLA_KW, GLA_KW, GLA_WIDTH, GLA_WIDTH, SWA_WIDTH, 2 * SWA_KVW, SWA_WIDTH)
    groups, col0 = [], 0
    for width in widths:
        groups.append(_col_blocks(w, col0, width))
        col0 += width
    return (*groups, _col_blocks(w_lr))


class _Weights(NamedTuple):
    g_mix: Any
    w_in: tuple
    w_gate: Any
    b_gate: Any
    g_gla_out: Any
    sinks: Any
    w_out: Any
    g_ple: Any
    w_ple_gate: Any
    w_ple_proj: Any
    g_final: Any


def _attn_tasks(w, proj, s0, state_batch0, past, chunk, tile):
    q1, k1, v1, z1, q2, kv2, z2, lr = proj
    return [_gla_task(q1, k1, v1, z1, lr, w.w_gate, w.b_gate, w.g_gla_out, s0, state_batch0,
                      chunk, tile),
            _swa_task(w.sinks, q2, z2, kv2, past, chunk, tile)]


def _prompt_path(w, x, p, s0):
    n_batch, length, _ = x.shape
    assert n_batch == 2, "a call holding all three stages would not fit the VMEM budget"
    proj, attn = {}, {}
    y = None
    for s in range(n_batch + 2):
        has_in, has_attn, has_out = s < n_batch, 1 <= s <= n_batch, s >= 2
        tile = TILE_FUSED if has_attn else TILE_MATMUL_ONLY
        tasks, names = [], []
        if has_attn:
            tasks += _attn_tasks(w, proj[s - 1], s0, s - 1, None, CHUNK, tile)
            names.append("attn")
        if has_in:
            tasks.append(_inproj_task(x, s, 1, tile, w.g_mix, w.w_in))
            names.append("inproj")
        if has_out:
            o1, o2, _ = attn[s - 2]
            tasks.append(_outproj_task(x, o1, o2, p, s - 2, tile, w.w_out, w.g_ple,
                                       w.w_ple_gate, w.w_ple_proj, w.g_final, y_prev=y))
            names.append("outproj")
        res = _run(tasks, (1, length // tile), "_".join(names))
        if has_attn:
            (o1, s_new), (o2,) = res[0], res[1]
            attn[s - 1] = (o1, o2, s_new)
            res = res[2:]
        if has_in:
            proj[s] = res[0]
            res = res[1:]
        if has_out:
            (y,) = res[0]
    states = jnp.concatenate([attn[b][2] for b in range(n_batch)], axis=0)
    kv = [proj[b][5] for b in range(n_batch)]
    return y, states, kv


def _sample_path(w, x, p, s0, past):
    n_batch, length, d = x.shape
    flat = lambda a: a.reshape(1, n_batch * length, a.shape[-1])
    per_batch = lambda a: a.reshape(n_batch, length, a.shape[-1])
    (proj,) = _run([_inproj_task(flat(x), 0, 1, n_batch * length, w.g_mix, w.w_in)], (1, 1),
                   "inproj_sample")
    proj = tuple(per_batch(a) for a in proj)
    (o1, s_new), (o2,) = _run(_attn_tasks(w, proj, s0, 0, past, min(CHUNK, length), length),
                              (n_batch, 1), "attn_sample")
    ((y,),) = _run([_outproj_task(flat(x), flat(o1), flat(o2), flat(p), 0, n_batch * length,
                                  w.w_out, w.g_ple, w.w_ple_gate, w.w_ple_proj, w.g_final)],
                   (1, 1), "outproj_sample")
    return y.reshape(n_batch, length, d), s_new, proj[5]


def kernel(x_prompt, x_sample, p_prompt, p_sample, state_gla, cache_swa_k, cache_swa_v, g_mix, w_in, w_gate_up, b_gate, g_gla_out, swa_sinks, w_out, g_ple, w_ple_gate, w_ple_proj, g_final):
    depth = w_in.shape[0]
    assert depth == 1, "the final RMSNorm is fused into the last layer's output kernel"
    bp, lp, _ = x_prompt.shape
    bs, ls, _ = x_sample.shape
    assert cache_swa_k.shape[2] == WINDOW
    i = 0
    row = lambda a: a.reshape(1, -1).astype(F32)
    w = _Weights(
        g_mix=row(g_mix[i]), w_in=_split_w_in(w_in[i]),
        w_gate=jnp.pad(w_gate_up[i].astype(BF16), ((0, LANES - GLA_RANK), (0, 0))),
        b_gate=row(b_gate[i]), g_gla_out=row(g_gla_out[i]),
        sinks=swa_sinks[i].reshape(-1).astype(F32),
        w_out=_col_blocks(w_out[i].astype(BF16)),
        g_ple=row(g_ple[i]), w_ple_gate=_col_blocks(w_ple_gate[i].astype(BF16)),
        w_ple_proj=_col_blocks(w_ple_proj[i].astype(BF16)), g_final=row(g_final))

    s0_p = jnp.zeros((bp, GLA_HEADS, GLA_DK, GLA_DV), F32)
    y_p, s_p, kv_p = _prompt_path(w, x_prompt, p_prompt[i], s0_p)
    keep = min(WINDOW, lp)
    last = lambda col: jnp.concatenate(
        [a[:, lp - keep:, col * SWA_KVW:(col + 1) * SWA_KVW] for a in kv_p], axis=0
    ).reshape(bp, keep, SWA_KV_HEADS, SWA_HEAD_DIM)
    k_p, v_p = last(0), last(1)

    past = (cache_swa_k[i].reshape(bs, WINDOW, SWA_KVW), cache_swa_v[i].reshape(bs, WINDOW, SWA_KVW))
    y_s, s_s, kv_s = _sample_path(w, x_sample, p_sample[i], state_gla[i], past)
    k_s = kv_s[:, :, :SWA_KVW].reshape(bs, ls, SWA_KV_HEADS, SWA_HEAD_DIM)
    v_s = kv_s[:, :, SWA_KVW:].reshape(bs, ls, SWA_KV_HEADS, SWA_HEAD_DIM)
    return (y_p, y_s, s_p[None], k_p[None], v_p[None], s_s[None], k_s[None], v_s[None])
```

```python
import functools
from typing import Any, Callable, NamedTuple

import jax
import jax.numpy as jnp
from jax import lax
from jax.experimental import pallas as pl
from jax.experimental.pallas import tpu as pltpu

F32 = jnp.float32
BF16 = jnp.bfloat16

CHUNK = 64
RMS_EPS = 1e-6
GLA_HEADS = 4
GLA_DK = 128
GLA_DV = 256
GLA_KW = GLA_HEADS * GLA_DK
GLA_WIDTH = GLA_HEADS * GLA_DV
GLA_RANK = 16
GLA_TAU = 16.0
SWA_KV_HEADS = 4
SWA_GROUP = 4
SWA_HEAD_DIM = 64
SWA_WIDTH = SWA_KV_HEADS * SWA_GROUP * SWA_HEAD_DIM
SWA_KVW = SWA_KV_HEADS * SWA_HEAD_DIM
WINDOW = 128

LANES = 128
MXU_WIDTH = 256
WEIGHT_BLOCK = 512
VMEM_LIMIT_BYTES = 56 * 1024 * 1024
TILE_MATMUL_ONLY = 512
TILE_FUSED = 256
OUT_ROW_GROUP = 256
GLA_IN_FLIGHT = 3
SWA_IN_FLIGHT = 4
NORM_HEAD_START = 2

LOG2E = 1.4426950408889634
LN2 = 0.6931471805599453


def _dot(a, b):
    return jnp.dot(a, b, preferred_element_type=F32)


def _rms_scale(x):
    return lax.rsqrt(jnp.mean(x * x, axis=-1, keepdims=True) + RMS_EPS)


def _sigmoid(x):
    return 1.0 / (1.0 + jnp.exp2(x * -LOG2E))


def _silu(x):
    return x * _sigmoid(x)


def _resident(shape):
    zeros = (0,) * len(shape)
    return pl.BlockSpec(shape, lambda *_: zeros, pipeline_mode=pl.Buffered(1))


def _tile(tm, width, batch0=0):
    return pl.BlockSpec((None, tm, width), lambda b, i: (batch0 + b, i, 0))


class _Block(NamedTuple):
    array: Any
    shape: tuple
    index: tuple

    def spec(self):
        index = self.index
        return pl.BlockSpec(self.shape, lambda *_: index, pipeline_mode=pl.Buffered(1))


def _col_blocks(w, col0=0, width=None):
    width = width or w.shape[1] - col0
    blk = min(WEIGHT_BLOCK, width)
    assert col0 % blk == 0 and width % blk == 0
    return tuple(_Block(w, (w.shape[0], blk), (0, (col0 + c) // blk))
                 for c in range(0, width, blk))


def _block_pieces(block_refs):
    out, base = [], 0
    for ref in block_refs:
        width = ref.shape[1]
        for c in range(0, width, MXU_WIDTH):
            hi = min(c + MXU_WIDTH, width)
            out.append((slice(base + c, base + hi), ref, slice(c, hi)))
        base += width
    return out


class _Task(NamedTuple):
    stages: Callable[..., Any]
    operands: tuple
    in_specs: tuple
    out_specs: tuple
    out_shapes: tuple
    scratch: tuple = ()
    aliases: Any = None


_DONE = object()


class _Stream:
    def __init__(self, units, max_active):
        self.pending, self.active, self.max_active = list(units), [], max_active

    def busy(self):
        return bool(self.pending or self.active)

    def step(self):
        self.active = [g for g in self.active if next(g, _DONE) is not _DONE]
        if self.pending and len(self.active) < self.max_active:
            g = self.pending.pop(0)
            if next(g, _DONE) is not _DONE:
                self.active.append(g)


def _drive(heavy, light):
    heavy = [g for g in heavy if g is not None]
    while heavy or any(s.busy() for s in light):
        heavy = [g for g in heavy if next(g, _DONE) is not _DONE]
        for s in light:
            s.step()


def _run(tasks, grid, name):
    n_in = [len(t.operands) for t in tasks]
    n_out = [len(t.out_shapes) for t in tasks]
    n_scr = [len(t.scratch) for t in tasks]

    def body(*refs):
        ins = refs[:sum(n_in)]
        outs = refs[sum(n_in):sum(n_in) + sum(n_out)]
        scrs = refs[sum(n_in) + sum(n_out):]
        heavy, light = [], []
        i = o = s = 0
        for t, ni, no, ns in zip(tasks, n_in, n_out, n_scr):
            h, l = t.stages(*ins[i:i + ni], *outs[o:o + no], *scrs[s:s + ns])
            heavy.append(h)
            light += l
            i, o, s = i + ni, o + no, s + ns
        _drive(heavy, light)

    aliases = {}
    i = o = 0
    for t, ni, no in zip(tasks, n_in, n_out):
        for src, dst in (t.aliases or {}).items():
            aliases[i + src] = o + dst
        i, o = i + ni, o + no

    flat = pl.pallas_call(
        body,
        grid=grid,
        in_specs=[s for t in tasks for s in t.in_specs],
        out_specs=tuple(s for t in tasks for s in t.out_specs),
        out_shape=tuple(s for t in tasks for s in t.out_shapes),
        scratch_shapes=[s for t in tasks for s in t.scratch],
        input_output_aliases=aliases,
        compiler_params=pltpu.CompilerParams(
            dimension_semantics=("arbitrary", "arbitrary"), vmem_limit_bytes=VMEM_LIMIT_BYTES),
        name=name,
    )(*[a for t in tasks for a in t.operands])
    results, o = [], 0
    for no in n_out:
        results.append(tuple(flat[o:o + no]))
        o += no
    return results


def _inproj_stages(x_ref, g_ref, *refs, blocks_per_group):
    n_blocks = sum(blocks_per_group)
    w_refs, o_refs = refs[:n_blocks], refs[n_blocks:]

    def pieces():
        x = x_ref[...]
        n = ((x * _rms_scale(x)) * g_ref[...]).astype(BF16)
        for _ in range(NORM_HEAD_START):
            yield
        first = 0
        for n_blk, o_ref, is_gate in zip(blocks_per_group, o_refs, IN_IS_GATE):
            for cols, w_ref, cs in _block_pieces(w_refs[first:first + n_blk]):
                u = _dot(n, w_ref[:, cs])
                o_ref[:, cols] = (_silu(u) if is_gate else u).astype(o_ref.dtype)
                yield
            first += n_blk

    return pieces(), []


IN_DTYPES = (BF16, BF16, BF16, BF16, BF16, F32, BF16, BF16)
IN_IS_GATE = (False, False, False, True, False, False, True, False)


def _inproj_task(x, batch0, n_batch, tm, g_mix, groups):
    _, length, d = x.shape
    blocks = [w for g in groups for w in g]
    widths = [sum(w.shape[1] for w in g) for g in groups]
    return _Task(
        stages=functools.partial(_inproj_stages, blocks_per_group=tuple(len(g) for g in groups)),
        operands=(x, g_mix, *[w.array for w in blocks]),
        in_specs=(_tile(tm, d, batch0), _resident(g_mix.shape), *[w.spec() for w in blocks]),
        out_specs=tuple(_tile(tm, n) for n in widths),
        out_shapes=tuple(jax.ShapeDtypeStruct((n_batch, length, n), dt)
                         for n, dt in zip(widths, IN_DTYPES)))


def _gla_stages(q_ref, k_ref, v_ref, z_ref, lr_ref, wg_ref, bg_ref, gout_ref, s0_ref,
                o_ref, s_ref, *, chunk, n_chunks):
    @pl.when(pl.program_id(1) == 0)
    def _():
        s_ref[...] = s0_ref[...]

    row = lax.broadcasted_iota(jnp.int32, (chunk, chunk), 0)
    col = lax.broadcasted_iota(jnp.int32, (chunk, chunk), 1)
    causal = row >= col
    row2 = lax.broadcasted_iota(jnp.int32, (chunk, 2 * chunk), 0)
    col2 = lax.broadcasted_iota(jnp.int32, (chunk, 2 * chunk), 1)
    tril2 = jnp.where(row2 >= lax.rem(col2, chunk), 1.0, 0.0).astype(BF16)
    eye = (lax.broadcasted_iota(jnp.int32, (GLA_DK, GLA_DK), 0)
           == lax.broadcasted_iota(jnp.int32, (GLA_DK, GLA_DK), 1))
    scale = GLA_DK ** -0.5
    heads = [(slice(h * GLA_DK, (h + 1) * GLA_DK), slice(h * GLA_DV, (h + 1) * GLA_DV))
             for h in range(GLA_HEADS)]

    def chunk_unit(c):
        r = slice(c * chunk, (c + 1) * chunk)
        gpre = _dot(lr_ref[r, :], wg_ref[...])
        yield
        gpre = gpre + bg_ref[...]
        soft = jnp.log2(1.0 + jnp.exp2(jnp.abs(gpre) * -LOG2E))
        g = jnp.minimum(gpre, 0.0) * (1.0 / GLA_TAU) - soft * (LN2 / GLA_TAU)
        g_hi = g.astype(BF16)
        g_lo = (g - g_hi.astype(F32)).astype(BF16)
        g2 = jnp.concatenate([g_hi, g_lo], axis=0)
        yield
        b = _dot(tril2, g2)
        yield
        b_last = b[chunk - 1:chunk, :]
        q = q_ref[r, :].astype(F32)
        k = k_ref[r, :].astype(F32)
        qe = ((q * scale) * jnp.exp(b)).astype(BF16)
        ke = k * jnp.exp(-b)
        kd = k * jnp.exp(b_last - b)
        decay = jnp.exp(b_last)
        v = [v_ref[r, vs] for _, vs in heads]
        ke_t = [ke[:, ks].T.astype(BF16) for ks, _ in heads]
        kd_t = [kd[:, ks].T.astype(BF16) for ks, _ in heads]
        yield
        a_raw = [_dot(qe[:, ks], ke_t[h]) for h, (ks, _) in enumerate(heads)]
        kv = [_dot(kd_t[h], v[h]) for h in range(GLA_HEADS)]
        yield
        lhs, rhs = [], []
        for h, (ks, _) in enumerate(heads):
            a = jnp.where(causal, a_raw[h], 0.0).astype(BF16)
            s = s_ref[h]
            lhs.append(jnp.concatenate([qe[:, ks], a], axis=1))
            rhs.append(jnp.concatenate([s.astype(BF16), v[h]], axis=0))
            dcol = jnp.sum(jnp.where(eye, decay[:, ks], 0.0), axis=1, keepdims=True)
            s_ref[h] = dcol * s + kv[h]
        yield
        o = [_dot(lhs[h], rhs[h]) for h in range(GLA_HEADS)]
        yield
        for h, (_, vs) in enumerate(heads):
            oh = (o[h] * _rms_scale(o[h])) * gout_ref[...]
            o_ref[r, vs] = (oh * z_ref[r, vs].astype(F32)).astype(o_ref.dtype)

    return None, [_Stream([chunk_unit(c) for c in range(n_chunks)], GLA_IN_FLIGHT)]


def _gla_task(q, k, v, z, lr, wg, bg, gout, s0, state_batch0, chunk, tb):
    n_batch, length, _ = q.shape
    st_in = pl.BlockSpec((None, GLA_HEADS, GLA_DK, GLA_DV),
                         lambda b, i: (state_batch0 + b, 0, 0, 0))
    st_out = pl.BlockSpec((None, GLA_HEADS, GLA_DK, GLA_DV), lambda b, i: (b, 0, 0, 0))
    return _Task(
        stages=functools.partial(_gla_stages, chunk=chunk, n_chunks=tb // chunk),
        operands=(q, k, v, z, lr, wg, bg, gout, s0),
        in_specs=(_tile(tb, GLA_KW), _tile(tb, GLA_KW), _tile(tb, GLA_WIDTH), _tile(tb, GLA_WIDTH),
                  _tile(tb, LANES), _resident(wg.shape), _resident(bg.shape),
                  _resident(gout.shape), st_in),
        out_specs=(_tile(tb, GLA_WIDTH), st_out),
        out_shapes=(jax.ShapeDtypeStruct((n_batch, length, GLA_WIDTH), BF16),
                    jax.ShapeDtypeStruct((n_batch, GLA_HEADS, GLA_DK, GLA_DV), F32)))


def _swa_stages(sink_ref, q_ref, kp_ref, vp_ref, kc_ref, vc_ref, z_ref, o_ref,
                *, chunk, n_chunks, mask_missing_past):
    kvw = SWA_GROUP * SWA_HEAD_DIM
    n_keys = WINDOW + chunk
    gq = SWA_GROUP * chunk
    gq_group = lax.broadcasted_iota(jnp.int32, (1, gq), 1) // chunk
    key_i = lax.broadcasted_iota(jnp.int32, (n_keys, gq), 0)
    qscale = SWA_HEAD_DIM ** -0.5
    first_pos = pl.program_id(1) * (chunk * n_chunks) - WINDOW
    shared = {}

    def head_operands(h):
        if not shared:
            shared["k"] = jnp.concatenate([kp_ref[...], kc_ref[...]], axis=0).astype(BF16)
            v = jnp.concatenate([vp_ref[...], vc_ref[...]], axis=0)
            shared["vt"] = v.T.astype(BF16)
        if h not in shared:
            hd = slice(h * SWA_HEAD_DIM, (h + 1) * SWA_HEAD_DIM)
            k_h = shared["k"][:, hd]
            vt = shared["vt"][hd, :]
            vt_at = {}
            for c in range(n_chunks):
                shift = (c * chunk) % LANES
                if shift not in vt_at:
                    vt_at[shift] = vt[:, shift:]
            sink = jnp.zeros((1, gq), F32)
            for g in range(SWA_GROUP):
                sink = jnp.where(gq_group == g, sink_ref[h * SWA_GROUP + g], sink)
            shared[h] = (k_h, vt_at, sink)
        return shared[h]

    def unit(h, c):
        k_h, vt_at, sink = head_operands(h)
        rows = slice(c * chunk, (c + 1) * chunk)
        keys = slice(c * chunk, c * chunk + n_keys)
        cols = slice(h * kvw, (h + 1) * kvw)
        qt = q_ref[rows, cols].astype(F32).T
        qg = jnp.concatenate(
            [qt[g * SWA_HEAD_DIM:(g + 1) * SWA_HEAD_DIM] for g in range(SWA_GROUP)],
            axis=1).astype(BF16)
        shift = (c * chunk) % LANES
        vt_keys = vt_at[shift][:, c * chunk - shift:c * chunk - shift + n_keys]
        yield
        s = _dot(k_h[keys], qg)
        yield
        if mask_missing_past and c * chunk < WINDOW:
            s = jnp.where(key_i + (first_pos + c * chunk) >= 0, s, jnp.finfo(F32).min)
        m = jnp.maximum(jnp.max(s, axis=0, keepdims=True) * qscale, sink) * LOG2E
        p = jnp.exp2(s * (qscale * LOG2E) - m)
        denom = jnp.sum(p, axis=0, keepdims=True) + jnp.exp2(sink * LOG2E - m)
        p = p.astype(BF16)
        yield
        o_t = _dot(vt_keys, p)
        yield
        o_t = (o_t / denom).T
        o = jnp.concatenate([o_t[g * chunk:(g + 1) * chunk] for g in range(SWA_GROUP)],
                            axis=1)
        o_ref[rows, cols] = (o * z_ref[rows, cols].astype(F32)).astype(o_ref.dtype)

    units = [unit(h, c) for h in range(SWA_KV_HEADS) for c in range(n_chunks)]
    return None, [_Stream(units, SWA_IN_FLIGHT)]


def _swa_task(sinks, q, z, kv, past, chunk, qb):
    n_batch, length, _ = q.shape
    if past is None:
        per_win = qb // WINDOW
        prev = lambda col: pl.BlockSpec(
            (None, WINDOW, SWA_KVW), lambda b, i: (b, jnp.maximum(i * per_win - 1, 0), col))
        k_past, v_past, kp_spec, vp_spec = kv, kv, prev(0), prev(1)
    else:
        k_past, v_past = past
        kp_spec = vp_spec = pl.BlockSpec((None, WINDOW, SWA_KVW), lambda b, i: (b, 0, 0))
    cur = lambda col: pl.BlockSpec((None, qb, SWA_KVW), lambda b, i: (b, i, col))
    return _Task(
        stages=functools.partial(_swa_stages, chunk=chunk, n_chunks=qb // chunk,
                                 mask_missing_past=past is None),
        operands=(sinks, q, k_past, v_past, kv, kv, z),
        in_specs=(pl.BlockSpec(memory_space=pltpu.SMEM), _tile(qb, SWA_WIDTH), kp_spec, vp_spec,
                  cur(0), cur(1), _tile(qb, SWA_WIDTH)),
        out_specs=(_tile(qb, SWA_WIDTH),),
        out_shapes=(jax.ShapeDtypeStruct((n_batch, length, SWA_WIDTH), BF16),))


def _outproj_stages(x_ref, o1_ref, o2_ref, p_ref, gple_ref, gfin_ref, *refs, n_blocks,
                    own_batch):
    wo, wg, wp = (refs[j * n_blocks:(j + 1) * n_blocks] for j in range(3))
    y_ref = refs[-1]
    if own_batch is not None:
        for b in range(y_ref.shape[0]):
            if b != own_batch:
                y_ref[b] = jnp.zeros(y_ref.shape[1:], y_ref.dtype)
        y_ref = y_ref.at[own_batch]
    h_ref = y_ref

    def pieces():
        o = jnp.concatenate([o1_ref[...], o2_ref[...]], axis=1)
        n = []
        for cols, w_ref, cs in _block_pieces(wo):
            h = x_ref[:, cols] + _dot(o, w_ref[:, cs])
            h_ref[:, cols] = h
            n.append((h * gple_ref[:, cols]).astype(BF16))
            yield
        n = jnp.concatenate(n, axis=1)
        r = _rms_scale(h_ref[...]) * -LOG2E
        p = p_ref[...].astype(BF16)
        tm = h_ref.shape[0]
        for rows in [slice(r0, r0 + OUT_ROW_GROUP) for r0 in range(0, tm, OUT_ROW_GROUP)]:
            for (cols, wg_ref, cs), (_, wp_ref, _) in zip(_block_pieces(wg), _block_pieces(wp)):
                gate = 1.0 / (1.0 + jnp.exp2(_dot(n[rows], wg_ref[:, cs]) * r[rows]))
                h_ref[rows, cols] = h_ref[rows, cols] + gate * _dot(p[rows], wp_ref[:, cs])
                yield
            h = h_ref[rows, :]
            y_ref[rows, :] = (h * _rms_scale(h)) * gfin_ref[...]

    return pieces(), []


def _outproj_task(x, o1, o2, p, batch0, tm, wo, gple, wg, wp, gfin, y_prev=None):
    _, _, d = x.shape
    assert len(wo) == len(wg) == len(wp)
    blocks = (*wo, *wg, *wp)
    operands = (x, o1, o2, p, gple, gfin, *[w.array for w in blocks])
    in_specs = (_tile(tm, d, batch0), _tile(tm, o1.shape[2]), _tile(tm, o2.shape[2]),
                _tile(tm, p.shape[2], batch0), _resident(gple.shape), _resident(gfin.shape),
                *[w.spec() for w in blocks])
    aliases, own_batch, y_spec = None, None, _tile(tm, d, batch0)
    if y_prev is not None:
        aliases = {len(operands): 0}
        operands += (y_prev,)
        in_specs += (pl.BlockSpec(memory_space=pl.ANY),)
    elif x.shape[0] > 1:
        own_batch = batch0
        y_spec = pl.BlockSpec((x.shape[0], tm, d), lambda b, i: (0, i, 0))
    return _Task(stages=functools.partial(_outproj_stages, n_blocks=len(wo), own_batch=own_batch),
                 operands=operands, in_specs=in_specs,
                 out_specs=(y_spec,),
                 out_shapes=(jax.ShapeDtypeStruct(x.shape, F32),),
                 aliases=aliases)


def _split_w_in(w_in):
    lr0 = 2 * GLA_KW + 2 * GLA_WIDTH
    n_rows, n_in = w_in.shape
    n_main = n_in - GLA_RANK
    assert lr0 % WEIGHT_BLOCK == 0 and n_main % WEIGHT_BLOCK == 0

    def relayout(w_ref, o_ref):
        o_ref[...] = w_ref[...].T.astype(BF16)

    assert WEIGHT_BLOCK % GLA_RANK == 0
    src_col = lambda j: GLA_RANK * (j * (WEIGHT_BLOCK // GLA_RANK)
                                    + (j >= lr0 // WEIGHT_BLOCK).astype(jnp.int32))
    w = pl.pallas_call(
        relayout,
        grid=(n_main // WEIGHT_BLOCK,),
        in_specs=[pl.BlockSpec((pl.Element(WEIGHT_BLOCK), pl.Element(n_rows)),
                               lambda j: (src_col(j), 0))],
        out_specs=pl.BlockSpec((n_rows, WEIGHT_BLOCK), lambda j: (0, j)),
        out_shape=jax.ShapeDtypeStruct((n_rows, n_main), BF16),
        compiler_params=pltpu.CompilerParams(
            dimension_semantics=("arbitrary",), vmem_limit_bytes=VMEM_LIMIT_BYTES),
        name="w_in_relayout",
    )(w_in.T)

    def relayout_gate(w_ref, o_ref):
        pad = jnp.zeros((LANES - GLA_RANK, n_rows), F32)
        o_ref[...] = jnp.concatenate([w_ref[...], pad], axis=0).T.astype(BF16)

    w_lr = pl.pallas_call(
        relayout_gate,
        grid=(1,),
        in_specs=[pl.BlockSpec((pl.Element(GLA_RANK), pl.Element(n_rows)), lambda j: (lr0, 0))],
        out_specs=pl.BlockSpec((n_rows, LANES), lambda j: (0, 0)),
        out_shape=jax.ShapeDtypeStruct((n_rows, LANES), BF16),
        name="w_gate_relayout",
    )(w_in.T)
    widths = (GLA_KW, GLA_KW, GLA_WIDTH, GLA_WIDTH, SWA_WIDTH, 2 * SWA_KVW, SWA_WIDTH)
    groups, col0 = [], 0
    for width in widths:
        groups.append(_col_blocks(w, col0, width))
        col0 += width
    return (*groups, _col_blocks(w_lr))


class _Weights(NamedTuple):
    g_mix: Any
    w_in: tuple
    w_gate: Any
    b_gate: Any
    g_gla_out: Any
    sinks: Any
    w_out: Any
    g_ple: Any
    w_ple_gate: Any
    w_ple_proj: Any
    g_final: Any


def _attn_tasks(w, proj, s0, state_batch0, past, chunk, tile):
    q1, k1, v1, z1, q2, kv2, z2, lr = proj
    return [_gla_task(q1, k1, v1, z1, lr, w.w_gate, w.b_gate, w.g_gla_out, s0, state_batch0,
                      chunk, tile),
            _swa_task(w.sinks, q2, z2, kv2, past, chunk, tile)]


def _prompt_path(w, x, p, s0):
    n_batch, length, _ = x.shape
    assert n_batch == 2, "a call holding all three stages would not fit the VMEM budget"
    proj, attn = {}, {}
    y = None
    for s in range(n_batch + 2):
        has_in, has_attn, has_out = s < n_batch, 1 <= s <= n_batch, s >= 2
        tile = TILE_FUSED if has_attn else TILE_MATMUL_ONLY
        tasks, names = [], []
        if has_attn:
            tasks += _attn_tasks(w, proj[s - 1], s0, s - 1, None, CHUNK, tile)
            names.append("attn")
        if has_in:
            tasks.append(_inproj_task(x, s, 1, tile, w.g_mix, w.w_in))
            names.append("inproj")
        if has_out:
            o1, o2, _ = attn[s - 2]
            tasks.append(_outproj_task(x, o1, o2, p, s - 2, tile, w.w_out, w.g_ple,
                                       w.w_ple_gate, w.w_ple_proj, w.g_final, y_prev=y))
            names.append("outproj")
        res = _run(tasks, (1, length // tile), "_".join(names))
        if has_attn:
            (o1, s_new), (o2,) = res[0], res[1]
            attn[s - 1] = (o1, o2, s_new)
            res = res[2:]
        if has_in:
            proj[s] = res[0]
            res = res[1:]
        if has_out:
            (y,) = res[0]
    states = jnp.concatenate([attn[b][2] for b in range(n_batch)], axis=0)
    kv = [proj[b][5] for b in range(n_batch)]
    return y, states, kv


def _sample_path(w, x, p, s0, past):
    n_batch, length, d = x.shape
    flat = lambda a: a.reshape(1, n_batch * length, a.shape[-1])
    per_batch = lambda a: a.reshape(n_batch, length, a.shape[-1])
    (proj,) = _run([_inproj_task(flat(x), 0, 1, n_batch * length, w.g_mix, w.w_in)], (1, 1),
                   "inproj_sample")
    proj = tuple(per_batch(a) for a in proj)
    (o1, s_new), (o2,) = _run(_attn_tasks(w, proj, s0, 0, past, min(CHUNK, length), length),
                              (n_batch, 1), "attn_sample")
    ((y,),) = _run([_outproj_task(flat(x), flat(o1), flat(o2), flat(p), 0, n_batch * length,
                                  w.w_out, w.g_ple, w.w_ple_gate, w.w_ple_proj, w.g_final)],
                   (1, 1), "outproj_sample")
    return y.reshape(n_batch, length, d), s_new, proj[5]


def kernel(x_prompt, x_sample, p_prompt, p_sample, state_gla, cache_swa_k, cache_swa_v, g_mix, w_in, w_gate_up, b_gate, g_gla_out, swa_sinks, w_out, g_ple, w_ple_gate, w_ple_proj, g_final):
    depth = w_in.shape[0]
    assert depth == 1, "the final RMSNorm is fused into the last layer's output kernel"
    bp, lp, _ = x_prompt.shape
    bs, ls, _ = x_sample.shape
    assert cache_swa_k.shape[2] == WINDOW
    i = 0
    row = lambda a: a.reshape(1, -1).astype(F32)
    w = _Weights(
        g_mix=row(g_mix[i]), w_in=_split_w_in(w_in[i]),
        w_gate=jnp.pad(w_gate_up[i].astype(BF16), ((0, LANES - GLA_RANK), (0, 0))),
        b_gate=row(b_gate[i]), g_gla_out=row(g_gla_out[i]),
        sinks=swa_sinks[i].reshape(-1).astype(F32),
        w_out=_col_blocks(w_out[i].astype(BF16)),
        g_ple=row(g_ple[i]), w_ple_gate=_col_blocks(w_ple_gate[i].astype(BF16)),
        w_ple_proj=_col_blocks(w_ple_proj[i].astype(BF16)), g_final=row(g_final))

    s0_p = jnp.zeros((bp, GLA_HEADS, GLA_DK, GLA_DV), F32)
    y_p, s_p, kv_p = _prompt_path(w, x_prompt, p_prompt[i], s0_p)
    keep = min(WINDOW, lp)
    last = lambda col: jnp.concatenate(
        [a[:, lp - keep:, col * SWA_KVW:(col + 1) * SWA_KVW] for a in kv_p], axis=0
    ).reshape(bp, keep, SWA_KV_HEADS, SWA_HEAD_DIM)
    k_p, v_p = last(0), last(1)

    past = (cache_swa_k[i].reshape(bs, WINDOW, SWA_KVW), cache_swa_v[i].reshape(bs, WINDOW, SWA_KVW))
    y_s, s_s, kv_s = _sample_path(w, x_sample, p_sample[i], state_gla[i], past)
    k_s = kv_s[:, :, :SWA_KVW].reshape(bs, ls, SWA_KV_HEADS, SWA_HEAD_DIM)
    v_s = kv_s[:, :, SWA_KVW:].reshape(bs, ls, SWA_KV_HEADS, SWA_HEAD_DIM)
    return (y_p, y_s, s_p[None], k_p[None], v_p[None], s_s[None], k_s[None], v_s[None])
```

```python
import functools
from typing import Any, Callable, NamedTuple

import jax
import jax.numpy as jnp
from jax import lax
from jax.experimental import pallas as pl
from jax.experimental.pallas import tpu as pltpu

F32 = jnp.float32
BF16 = jnp.bfloat16

CHUNK = 64
RMS_EPS = 1e-6
GLA_HEADS = 4
GLA_DK = 128
GLA_DV = 256
GLA_KW = GLA_HEADS * GLA_DK
GLA_WIDTH = GLA_HEADS * GLA_DV
GLA_RANK = 16
GLA_TAU = 16.0
SWA_KV_HEADS = 4
SWA_GROUP = 4
SWA_HEAD_DIM = 64
SWA_WIDTH = SWA_KV_HEADS * SWA_GROUP * SWA_HEAD_DIM
SWA_KVW = SWA_KV_HEADS * SWA_HEAD_DIM
WINDOW = 128

LANES = 128
MXU_WIDTH = 256
WEIGHT_BLOCK = 512
VMEM_LIMIT_BYTES = 56 * 1024 * 1024
TILE_MATMUL_ONLY = 512
TILE_FUSED = 256
OUT_ROW_GROUP = 256
GLA_IN_FLIGHT = 3
SWA_IN_FLIGHT = 4
NORM_HEAD_START = 1

LOG2E = 1.4426950408889634
LN2 = 0.6931471805599453


def _dot(a, b):
    return jnp.dot(a, b, preferred_element_type=F32)


def _rms_scale(x):
    return lax.rsqrt(jnp.mean(x * x, axis=-1, keepdims=True) + RMS_EPS)


def _sigmoid(x):
    return 1.0 / (1.0 + jnp.exp2(x * -LOG2E))


def _silu(x):
    return x * _sigmoid(x)


def _resident(shape):
    zeros = (0,) * len(shape)
    return pl.BlockSpec(shape, lambda *_: zeros, pipeline_mode=pl.Buffered(1))


def _tile(tm, width, batch0=0):
    return pl.BlockSpec((None, tm, width), lambda b, i: (batch0 + b, i, 0))


class _Block(NamedTuple):
    array: Any
    shape: tuple
    index: tuple

    def spec(self):
        index = self.index
        return pl.BlockSpec(self.shape, lambda *_: index, pipeline_mode=pl.Buffered(1))


def _col_blocks(w, col0=0, width=None):
    width = width or w.shape[1] - col0
    blk = min(WEIGHT_BLOCK, width)
    assert col0 % blk == 0 and width % blk == 0
    return tuple(_Block(w, (w.shape[0], blk), (0, (col0 + c) // blk))
                 for c in range(0, width, blk))


def _block_pieces(block_refs):
    out, base = [], 0
    for ref in block_refs:
        width = ref.shape[1]
        for c in range(0, width, MXU_WIDTH):
            hi = min(c + MXU_WIDTH, width)
            out.append((slice(base + c, base + hi), ref, slice(c, hi)))
        base += width
    return out


class _Task(NamedTuple):
    stages: Callable[..., Any]
    operands: tuple
    in_specs: tuple
    out_specs: tuple
    out_shapes: tuple
    scratch: tuple = ()
    aliases: Any = None


_DONE = object()


class _Stream:
    def __init__(self, units, max_active):
        self.pending, self.active, self.max_active = list(units), [], max_active

    def busy(self):
        return bool(self.pending or self.active)

    def step(self):
        self.active = [g for g in self.active if next(g, _DONE) is not _DONE]
        if self.pending and len(self.active) < self.max_active:
            g = self.pending.pop(0)
            if next(g, _DONE) is not _DONE:
                self.active.append(g)


def _drive(heavy, light):
    heavy = [g for g in heavy if g is not None]
    while heavy or any(s.busy() for s in light):
        for s in light:
            s.step()
        heavy = [g for g in heavy if next(g, _DONE) is not _DONE]


def _run(tasks, grid, name):
    n_in = [len(t.operands) for t in tasks]
    n_out = [len(t.out_shapes) for t in tasks]
    n_scr = [len(t.scratch) for t in tasks]

    def body(*refs):
        ins = refs[:sum(n_in)]
        outs = refs[sum(n_in):sum(n_in) + sum(n_out)]
        scrs = refs[sum(n_in) + sum(n_out):]
        heavy, light = [], []
        i = o = s = 0
        for t, ni, no, ns in zip(tasks, n_in, n_out, n_scr):
            h, l = t.stages(*ins[i:i + ni], *outs[o:o + no], *scrs[s:s + ns])
            heavy.append(h)
            light += l
            i, o, s = i + ni, o + no, s + ns
        _drive(heavy, light)

    aliases = {}
    i = o = 0
    for t, ni, no in zip(tasks, n_in, n_out):
        for src, dst in (t.aliases or {}).items():
            aliases[i + src] = o + dst
        i, o = i + ni, o + no

    flat = pl.pallas_call(
        body,
        grid=grid,
        in_specs=[s for t in tasks for s in t.in_specs],
        out_specs=tuple(s for t in tasks for s in t.out_specs),
        out_shape=tuple(s for t in tasks for s in t.out_shapes),
        scratch_shapes=[s for t in tasks for s in t.scratch],
        input_output_aliases=aliases,
        compiler_params=pltpu.CompilerParams(
            dimension_semantics=("arbitrary", "arbitrary"), vmem_limit_bytes=VMEM_LIMIT_BYTES),
        name=name,
    )(*[a for t in tasks for a in t.operands])
    results, o = [], 0
    for no in n_out:
        results.append(tuple(flat[o:o + no]))
        o += no
    return results


def _inproj_stages(x_ref, g_ref, *refs, blocks_per_group):
    n_blocks = sum(blocks_per_group)
    w_refs, o_refs = refs[:n_blocks], refs[n_blocks:]

    def pieces():
        x = x_ref[...]
        n = ((x * _rms_scale(x)) * g_ref[...]).astype(BF16)
        for _ in range(NORM_HEAD_START):
            yield
        first = 0
        for n_blk, o_ref, is_gate in zip(blocks_per_group, o_refs, IN_IS_GATE):
            for cols, w_ref, cs in _block_pieces(w_refs[first:first + n_blk]):
                u = _dot(n, w_ref[:, cs])
                o_ref[:, cols] = (_silu(u) if is_gate else u).astype(o_ref.dtype)
                yield
            first += n_blk

    return pieces(), []


IN_DTYPES = (BF16, BF16, BF16, BF16, BF16, F32, BF16, BF16)
IN_IS_GATE = (False, False, False, True, False, False, True, False)


def _inproj_task(x, batch0, n_batch, tm, g_mix, groups):
    _, length, d = x.shape
    blocks = [w for g in groups for w in g]
    widths = [sum(w.shape[1] for w in g) for g in groups]
    return _Task(
        stages=functools.partial(_inproj_stages, blocks_per_group=tuple(len(g) for g in groups)),
        operands=(x, g_mix, *[w.array for w in blocks]),
        in_specs=(_tile(tm, d, batch0), _resident(g_mix.shape), *[w.spec() for w in blocks]),
        out_specs=tuple(_tile(tm, n) for n in widths),
        out_shapes=tuple(jax.ShapeDtypeStruct((n_batch, length, n), dt)
                         for n, dt in zip(widths, IN_DTYPES)))


def _gla_stages(q_ref, k_ref, v_ref, z_ref, lr_ref, wg_ref, bg_ref, gout_ref, s0_ref,
                o_ref, s_ref, *, chunk, n_chunks):
    @pl.when(pl.program_id(1) == 0)
    def _():
        s_ref[...] = s0_ref[...]

    row = lax.broadcasted_iota(jnp.int32, (chunk, chunk), 0)
    col = lax.broadcasted_iota(jnp.int32, (chunk, chunk), 1)
    causal = row >= col
    row2 = lax.broadcasted_iota(jnp.int32, (chunk, 2 * chunk), 0)
    col2 = lax.broadcasted_iota(jnp.int32, (chunk, 2 * chunk), 1)
    tril2 = jnp.where(row2 >= lax.rem(col2, chunk), 1.0, 0.0).astype(BF16)
    eye = (lax.broadcasted_iota(jnp.int32, (GLA_DK, GLA_DK), 0)
           == lax.broadcasted_iota(jnp.int32, (GLA_DK, GLA_DK), 1))
    scale = GLA_DK ** -0.5
    heads = [(slice(h * GLA_DK, (h + 1) * GLA_DK), slice(h * GLA_DV, (h + 1) * GLA_DV))
             for h in range(GLA_HEADS)]

    def chunk_unit(c):
        r = slice(c * chunk, (c + 1) * chunk)
        gpre = _dot(lr_ref[r, :], wg_ref[...])
        yield
        gpre = gpre + bg_ref[...]
        soft = jnp.log2(1.0 + jnp.exp2(jnp.abs(gpre) * -LOG2E))
        g = jnp.minimum(gpre, 0.0) * (1.0 / GLA_TAU) - soft * (LN2 / GLA_TAU)
        g_hi = g.astype(BF16)
        g_lo = (g - g_hi.astype(F32)).astype(BF16)
        g2 = jnp.concatenate([g_hi, g_lo], axis=0)
        yield
        b = _dot(tril2, g2)
        yield
        b_last = b[chunk - 1:chunk, :]
        q = q_ref[r, :].astype(F32)
        k = k_ref[r, :].astype(F32)
        qe = ((q * scale) * jnp.exp(b)).astype(BF16)
        ke = k * jnp.exp(-b)
        kd = k * jnp.exp(b_last - b)
        decay = jnp.exp(b_last)
        v = [v_ref[r, vs] for _, vs in heads]
        ke_t = [ke[:, ks].T.astype(BF16) for ks, _ in heads]
        kd_t = [kd[:, ks].T.astype(BF16) for ks, _ in heads]
        yield
        a_raw = [_dot(qe[:, ks], ke_t[h]) for h, (ks, _) in enumerate(heads)]
        kv = [_dot(kd_t[h], v[h]) for h in range(GLA_HEADS)]
        yield
        lhs, rhs = [], []
        for h, (ks, _) in enumerate(heads):
            a = jnp.where(causal, a_raw[h], 0.0).astype(BF16)
            s = s_ref[h]
            lhs.append(jnp.concatenate([qe[:, ks], a], axis=1))
            rhs.append(jnp.concatenate([s.astype(BF16), v[h]], axis=0))
            dcol = jnp.sum(jnp.where(eye, decay[:, ks], 0.0), axis=1, keepdims=True)
            s_ref[h] = dcol * s + kv[h]
        yield
        o = [_dot(lhs[h], rhs[h]) for h in range(GLA_HEADS)]
        yield
        for h, (_, vs) in enumerate(heads):
            oh = (o[h] * _rms_scale(o[h])) * gout_ref[...]
            o_ref[r, vs] = (oh * z_ref[r, vs].astype(F32)).astype(o_ref.dtype)

    return None, [_Stream([chunk_unit(c) for c in range(n_chunks)], GLA_IN_FLIGHT)]


def _gla_task(q, k, v, z, lr, wg, bg, gout, s0, state_batch0, chunk, tb):
    n_batch, length, _ = q.shape
    st_in = pl.BlockSpec((None, GLA_HEADS, GLA_DK, GLA_DV),
                         lambda b, i: (state_batch0 + b, 0, 0, 0))
    st_out = pl.BlockSpec((None, GLA_HEADS, GLA_DK, GLA_DV), lambda b, i: (b, 0, 0, 0))
    return _Task(
        stages=functools.partial(_gla_stages, chunk=chunk, n_chunks=tb // chunk),
        operands=(q, k, v, z, lr, wg, bg, gout, s0),
        in_specs=(_tile(tb, GLA_KW), _tile(tb, GLA_KW), _tile(tb, GLA_WIDTH), _tile(tb, GLA_WIDTH),
                  _tile(tb, LANES), _resident(wg.shape), _resident(bg.shape),
                  _resident(gout.shape), st_in),
        out_specs=(_tile(tb, GLA_WIDTH), st_out),
        out_shapes=(jax.ShapeDtypeStruct((n_batch, length, GLA_WIDTH), BF16),
                    jax.ShapeDtypeStruct((n_batch, GLA_HEADS, GLA_DK, GLA_DV), F32)))


def _swa_stages(sink_ref, q_ref, kp_ref, vp_ref, kc_ref, vc_ref, z_ref, o_ref,
                *, chunk, n_chunks, mask_missing_past):
    kvw = SWA_GROUP * SWA_HEAD_DIM
    n_keys = WINDOW + chunk
    gq = SWA_GROUP * chunk
    gq_group = lax.broadcasted_iota(jnp.int32, (1, gq), 1) // chunk
    key_i = lax.broadcasted_iota(jnp.int32, (n_keys, gq), 0)
    qscale = SWA_HEAD_DIM ** -0.5
    first_pos = pl.program_id(1) * (chunk * n_chunks) - WINDOW
    shared = {}

    def head_operands(h):
        if not shared:
            shared["k"] = jnp.concatenate([kp_ref[...], kc_ref[...]], axis=0).astype(BF16)
            v = jnp.concatenate([vp_ref[...], vc_ref[...]], axis=0)
            shared["vt"] = v.T.astype(BF16)
        if h not in shared:
            hd = slice(h * SWA_HEAD_DIM, (h + 1) * SWA_HEAD_DIM)
            k_h = shared["k"][:, hd]
            vt = shared["vt"][hd, :]
            vt_at = {}
            for c in range(n_chunks):
                shift = (c * chunk) % LANES
                if shift not in vt_at:
                    vt_at[shift] = vt[:, shift:]
            sink = jnp.zeros((1, gq), F32)
            for g in range(SWA_GROUP):
                sink = jnp.where(gq_group == g, sink_ref[h * SWA_GROUP + g], sink)
            shared[h] = (k_h, vt_at, sink)
        return shared[h]

    def unit(h, c):
        k_h, vt_at, sink = head_operands(h)
        rows = slice(c * chunk, (c + 1) * chunk)
        keys = slice(c * chunk, c * chunk + n_keys)
        cols = slice(h * kvw, (h + 1) * kvw)
        qt = q_ref[rows, cols].astype(F32).T
        qg = jnp.concatenate(
            [qt[g * SWA_HEAD_DIM:(g + 1) * SWA_HEAD_DIM] for g in range(SWA_GROUP)],
            axis=1).astype(BF16)
        shift = (c * chunk) % LANES
        vt_keys = vt_at[shift][:, c * chunk - shift:c * chunk - shift + n_keys]
        yield
        s = _dot(k_h[keys], qg)
        yield
        if mask_missing_past and c * chunk < WINDOW:
            s = jnp.where(key_i + (first_pos + c * chunk) >= 0, s, jnp.finfo(F32).min)
        m = jnp.maximum(jnp.max(s, axis=0, keepdims=True) * qscale, sink) * LOG2E
        p = jnp.exp2(s * (qscale * LOG2E) - m)
        denom = jnp.sum(p, axis=0, keepdims=True) + jnp.exp2(sink * LOG2E - m)
        p = p.astype(BF16)
        yield
        o_t = _dot(vt_keys, p)
        yield
        o_t = (o_t / denom).T
        o = jnp.concatenate([o_t[g * chunk:(g + 1) * chunk] for g in range(SWA_GROUP)],
                            axis=1)
        o_ref[rows, cols] = (o * z_ref[rows, cols].astype(F32)).astype(o_ref.dtype)

    units = [unit(h, c) for h in range(SWA_KV_HEADS) for c in range(n_chunks)]
    return None, [_Stream(units, SWA_IN_FLIGHT)]


def _swa_task(sinks, q, z, kv, past, chunk, qb):
    n_batch, length, _ = q.shape
    if past is None:
        per_win = qb // WINDOW
        prev = lambda col: pl.BlockSpec(
            (None, WINDOW, SWA_KVW), lambda b, i: (b, jnp.maximum(i * per_win - 1, 0), col))
        k_past, v_past, kp_spec, vp_spec = kv, kv, prev(0), prev(1)
    else:
        k_past, v_past = past
        kp_spec = vp_spec = pl.BlockSpec((None, WINDOW, SWA_KVW), lambda b, i: (b, 0, 0))
    cur = lambda col: pl.BlockSpec((None, qb, SWA_KVW), lambda b, i: (b, i, col))
    return _Task(
        stages=functools.partial(_swa_stages, chunk=chunk, n_chunks=qb // chunk,
                                 mask_missing_past=past is None),
        operands=(sinks, q, k_past, v_past, kv, kv, z),
        in_specs=(pl.BlockSpec(memory_space=pltpu.SMEM), _tile(qb, SWA_WIDTH), kp_spec, vp_spec,
                  cur(0), cur(1), _tile(qb, SWA_WIDTH)),
        out_specs=(_tile(qb, SWA_WIDTH),),
        out_shapes=(jax.ShapeDtypeStruct((n_batch, length, SWA_WIDTH), BF16),))


def _outproj_stages(x_ref, o1_ref, o2_ref, p_ref, gple_ref, gfin_ref, *refs, n_blocks,
                    own_batch):
    wo, wg, wp = (refs[j * n_blocks:(j + 1) * n_blocks] for j in range(3))
    y_ref = refs[-1]
    if own_batch is not None:
        for b in range(y_ref.shape[0]):
            if b != own_batch:
                y_ref[b] = jnp.zeros(y_ref.shape[1:], y_ref.dtype)
        y_ref = y_ref.at[own_batch]
    h_ref = y_ref

    def pieces():
        o = jnp.concatenate([o1_ref[...], o2_ref[...]], axis=1)
        n = []
        for cols, w_ref, cs in _block_pieces(wo):
            h = x_ref[:, cols] + _dot(o, w_ref[:, cs])
            h_ref[:, cols] = h
            n.append((h * gple_ref[:, cols]).astype(BF16))
            yield
        n = jnp.concatenate(n, axis=1)
        r = _rms_scale(h_ref[...]) * -LOG2E
        p = p_ref[...].astype(BF16)
        tm = h_ref.shape[0]
        for rows in [slice(r0, r0 + OUT_ROW_GROUP) for r0 in range(0, tm, OUT_ROW_GROUP)]:
            for (cols, wg_ref, cs), (_, wp_ref, _) in zip(_block_pieces(wg), _block_pieces(wp)):
                gate = 1.0 / (1.0 + jnp.exp2(_dot(n[rows], wg_ref[:, cs]) * r[rows]))
                h_ref[rows, cols] = h_ref[rows, cols] + gate * _dot(p[rows], wp_ref[:, cs])
                yield
            h = h_ref[rows, :]
            y_ref[rows, :] = (h * _rms_scale(h)) * gfin_ref[...]

    return pieces(), []


def _outproj_task(x, o1, o2, p, batch0, tm, wo, gple, wg, wp, gfin, y_prev=None):
    _, _, d = x.shape
    assert len(wo) == len(wg) == len(wp)
    blocks = (*wo, *wg, *wp)
    operands = (x, o1, o2, p, gple, gfin, *[w.array for w in blocks])
    in_specs = (_tile(tm, d, batch0), _tile(tm, o1.shape[2]), _tile(tm, o2.shape[2]),
                _tile(tm, p.shape[2], batch0), _resident(gple.shape), _resident(gfin.shape),
                *[w.spec() for w in blocks])
    aliases, own_batch, y_spec = None, None, _tile(tm, d, batch0)
    if y_prev is not None:
        aliases = {len(operands): 0}
        operands += (y_prev,)
        in_specs += (pl.BlockSpec(memory_space=pl.ANY),)
    elif x.shape[0] > 1:
        own_batch = batch0
        y_spec = pl.BlockSpec((x.shape[0], tm, d), lambda b, i: (0, i, 0))
    return _Task(stages=functools.partial(_outproj_stages, n_blocks=len(wo), own_batch=own_batch),
                 operands=operands, in_specs=in_specs,
                 out_specs=(y_spec,),
                 out_shapes=(jax.ShapeDtypeStruct(x.shape, F32),),
                 aliases=aliases)


def _split_w_in(w_in):
    lr0 = 2 * GLA_KW + 2 * GLA_WIDTH
    n_rows, n_in = w_in.shape
    n_main = n_in - GLA_RANK
    assert lr0 % WEIGHT_BLOCK == 0 and n_main % WEIGHT_BLOCK == 0

    def relayout(w_ref, o_ref):
        o_ref[...] = w_ref[...].T.astype(BF16)

    assert WEIGHT_BLOCK % GLA_RANK == 0
    src_col = lambda j: GLA_RANK * (j * (WEIGHT_BLOCK // GLA_RANK)
                                    + (j >= lr0 // WEIGHT_BLOCK).astype(jnp.int32))
    w = pl.pallas_call(
        relayout,
        grid=(n_main // WEIGHT_BLOCK,),
        in_specs=[pl.BlockSpec((pl.Element(WEIGHT_BLOCK), pl.Element(n_rows)),
                               lambda j: (src_col(j), 0))],
        out_specs=pl.BlockSpec((n_rows, WEIGHT_BLOCK), lambda j: (0, j)),
        out_shape=jax.ShapeDtypeStruct((n_rows, n_main), BF16),
        compiler_params=pltpu.CompilerParams(
            dimension_semantics=("arbitrary",), vmem_limit_bytes=VMEM_LIMIT_BYTES),
        name="w_in_relayout",
    )(w_in.T)

    def relayout_gate(w_ref, o_ref):
        pad = jnp.zeros((LANES - GLA_RANK, n_rows), F32)
        o_ref[...] = jnp.concatenate([w_ref[...], pad], axis=0).T.astype(BF16)

    w_lr = pl.pallas_call(
        relayout_gate,
        grid=(1,),
        in_specs=[pl.BlockSpec((pl.Element(GLA_RANK), pl.Element(n_rows)), lambda j: (lr0, 0))],
        out_specs=pl.BlockSpec((n_rows, LANES), lambda j: (0, 0)),
        out_shape=jax.ShapeDtypeStruct((n_rows, LANES), BF16),
        name="w_gate_relayout",
    )(w_in.T)
    widths = (GLA_KW, GLA_KW, GLA_WIDTH, GLA_WIDTH, SWA_WIDTH, 2 * SWA_KVW, SWA_WIDTH)
    groups, col0 = [], 0
    for width in widths:
        groups.append(_col_blocks(w, col0, width))
        col0 += width
    return (*groups, _col_blocks(w_lr))


class _Weights(NamedTuple):
    g_mix: Any
    w_in: tuple
    w_gate: Any
    b_gate: Any
    g_gla_out: Any
    sinks: Any
    w_out: Any
    g_ple: Any
    w_ple_gate: Any
    w_ple_proj: Any
    g_final: Any


def _attn_tasks(w, proj, s0, state_batch0, past, chunk, tile):
    q1, k1, v1, z1, q2, kv2, z2, lr = proj
    return [_gla_task(q1, k1, v1, z1, lr, w.w_gate, w.b_gate, w.g_gla_out, s0, state_batch0,
                      chunk, tile),
            _swa_task(w.sinks, q2, z2, kv2, past, chunk, tile)]


def _prompt_path(w, x, p, s0):
    n_batch, length, _ = x.shape
    assert n_batch == 2, "a call holding all three stages would not fit the VMEM budget"
    proj, attn = {}, {}
    y = None
    for s in range(n_batch + 2):
        has_in, has_attn, has_out = s < n_batch, 1 <= s <= n_batch, s >= 2
        tile = TILE_FUSED if has_attn else TILE_MATMUL_ONLY
        tasks, names = [], []
        if has_attn:
            tasks += _attn_tasks(w, proj[s - 1], s0, s - 1, None, CHUNK, tile)
            names.append("attn")
        if has_in:
            tasks.append(_inproj_task(x, s, 1, tile, w.g_mix, w.w_in))
            names.append("inproj")
        if has_out:
            o1, o2, _ = attn[s - 2]
            tasks.append(_outproj_task(x, o1, o2, p, s - 2, tile, w.w_out, w.g_ple,
                                       w.w_ple_gate, w.w_ple_proj, w.g_final, y_prev=y))
            names.append("outproj")
        res = _run(tasks, (1, length // tile), "_".join(names))
        if has_attn:
            (o1, s_new), (o2,) = res[0], res[1]
            attn[s - 1] = (o1, o2, s_new)
            res = res[2:]
        if has_in:
            proj[s] = res[0]
            res = res[1:]
        if has_out:
            (y,) = res[0]
    states = jnp.concatenate([attn[b][2] for b in range(n_batch)], axis=0)
    kv = [proj[b][5] for b in range(n_batch)]
    return y, states, kv


def _sample_path(w, x, p, s0, past):
    n_batch, length, d = x.shape
    flat = lambda a: a.reshape(1, n_batch * length, a.shape[-1])
    per_batch = lambda a: a.reshape(n_batch, length, a.shape[-1])
    (proj,) = _run([_inproj_task(flat(x), 0, 1, n_batch * length, w.g_mix, w.w_in)], (1, 1),
                   "inproj_sample")
    proj = tuple(per_batch(a) for a in proj)
    (o1, s_new), (o2,) = _run(_attn_tasks(w, proj, s0, 0, past, min(CHUNK, length), length),
                              (n_batch, 1), "attn_sample")
    ((y,),) = _run([_outproj_task(flat(x), flat(o1), flat(o2), flat(p), 0, n_batch * length,
                                  w.w_out, w.g_ple, w.w_ple_gate, w.w_ple_proj, w.g_final)],
                   (1, 1), "outproj_sample")
    return y.reshape(n_batch, length, d), s_new, proj[5]


def kernel(x_prompt, x_sample, p_prompt, p_sample, state_gla, cache_swa_k, cache_swa_v, g_mix, w_in, w_gate_up, b_gate, g_gla_out, swa_sinks, w_out, g_ple, w_ple_gate, w_ple_proj, g_final):
    depth = w_in.shape[0]
    assert depth == 1, "the final RMSNorm is fused into the last layer's output kernel"
    bp, lp, _ = x_prompt.shape
    bs, ls, _ = x_sample.shape
    assert cache_swa_k.shape[2] == WINDOW
    i = 0
    row = lambda a: a.reshape(1, -1).astype(F32)
    w = _Weights(
        g_mix=row(g_mix[i]), w_in=_split_w_in(w_in[i]),
        w_gate=jnp.pad(w_gate_up[i].astype(BF16), ((0, LANES - GLA_RANK), (0, 0))),
        b_gate=row(b_gate[i]), g_gla_out=row(g_gla_out[i]),
        sinks=swa_sinks[i].reshape(-1).astype(F32),
        w_out=_col_blocks(w_out[i].astype(BF16)),
        g_ple=row(g_ple[i]), w_ple_gate=_col_blocks(w_ple_gate[i].astype(BF16)),
        w_ple_proj=_col_blocks(w_ple_proj[i].astype(BF16)), g_final=row(g_final))

    s0_p = jnp.zeros((bp, GLA_HEADS, GLA_DK, GLA_DV), F32)
    y_p, s_p, kv_p = _prompt_path(w, x_prompt, p_prompt[i], s0_p)
    keep = min(WINDOW, lp)
    last = lambda col: jnp.concatenate(
        [a[:, lp - keep:, col * SWA_KVW:(col + 1) * SWA_KVW] for a in kv_p], axis=0
    ).reshape(bp, keep, SWA_KV_HEADS, SWA_HEAD_DIM)
    k_p, v_p = last(0), last(1)

    past = (cache_swa_k[i].reshape(bs, WINDOW, SWA_KVW), cache_swa_v[i].reshape(bs, WINDOW, SWA_KVW))
    y_s, s_s, kv_s = _sample_path(w, x_sample, p_sample[i], state_gla[i], past)
    k_s = kv_s[:, :, :SWA_KVW].reshape(bs, ls, SWA_KV_HEADS, SWA_HEAD_DIM)
    v_s = kv_s[:, :, SWA_KVW:].reshape(bs, ls, SWA_KV_HEADS, SWA_HEAD_DIM)
    return (y_p, y_s, s_p[None], k_p[None], v_p[None], s_s[None], k_s[None], v_s[None])
```

```python
import functools
from typing import Any, Callable, NamedTuple

import jax
import jax.numpy as jnp
from jax import lax
from jax.experimental import pallas as pl
from jax.experimental.pallas import tpu as pltpu

F32 = jnp.float32
BF16 = jnp.bfloat16

CHUNK = 64
RMS_EPS = 1e-6
GLA_HEADS = 4
GLA_DK = 128
GLA_DV = 256
GLA_KW = GLA_HEADS * GLA_DK
GLA_WIDTH = GLA_HEADS * GLA_DV
GLA_RANK = 16
GLA_TAU = 16.0
SWA_KV_HEADS = 4
SWA_GROUP = 4
SWA_HEAD_DIM = 64
SWA_WIDTH = SWA_KV_HEADS * SWA_GROUP * SWA_HEAD_DIM
SWA_KVW = SWA_KV_HEADS * SWA_HEAD_DIM
WINDOW = 128

LANES = 128
MXU_WIDTH = 256
WEIGHT_BLOCK = 512
VMEM_LIMIT_BYTES = 56 * 1024 * 1024
TILE_MATMUL_ONLY = 512
TILE_FUSED = 256
OUT_ROW_GROUP = 256
GLA_IN_FLIGHT = 3
SWA_IN_FLIGHT = 4
NORM_HEAD_START = 1

LOG2E = 1.4426950408889634
LN2 = 0.6931471805599453


def _dot(a, b):
    return jnp.dot(a, b, preferred_element_type=F32)


def _rms_scale(x):
    return lax.rsqrt(jnp.mean(x * x, axis=-1, keepdims=True) + RMS_EPS)


def _sigmoid(x):
    return 1.0 / (1.0 + jnp.exp2(x * -LOG2E))


def _silu(x):
    return x * _sigmoid(x)


def _resident(shape):
    zeros = (0,) * len(shape)
    return pl.BlockSpec(shape, lambda *_: zeros, pipeline_mode=pl.Buffered(1))


def _tile(tm, width, batch0=0):
    return pl.BlockSpec((None, tm, width), lambda b, i: (batch0 + b, i, 0))


class _Block(NamedTuple):
    array: Any
    shape: tuple
    index: tuple

    def spec(self):
        index = self.index
        return pl.BlockSpec(self.shape, lambda *_: index, pipeline_mode=pl.Buffered(1))


def _col_blocks(w, col0=0, width=None):
    width = width or w.shape[1] - col0
    blk = min(WEIGHT_BLOCK, width)
    assert col0 % blk == 0 and width % blk == 0
    return tuple(_Block(w, (w.shape[0], blk), (0, (col0 + c) // blk))
                 for c in range(0, width, blk))


def _block_pieces(block_refs):
    out, base = [], 0
    for ref in block_refs:
        width = ref.shape[1]
        for c in range(0, width, MXU_WIDTH):
            hi = min(c + MXU_WIDTH, width)
            out.append((slice(base + c, base + hi), ref, slice(c, hi)))
        base += width
    return out


class _Task(NamedTuple):
    stages: Callable[..., Any]
    operands: tuple
    in_specs: tuple
    out_specs: tuple
    out_shapes: tuple
    scratch: tuple = ()
    aliases: Any = None


_DONE = object()


class _Stream:
    def __init__(self, units, max_active):
        self.pending, self.active, self.max_active = list(units), [], max_active

    def busy(self):
        return bool(self.pending or self.active)

    def step(self):
        self.active = [g for g in self.active if next(g, _DONE) is not _DONE]
        if self.pending and len(self.active) < self.max_active:
            g = self.pending.pop(0)
            if next(g, _DONE) is not _DONE:
                self.active.append(g)


def _drive(heavy, light):
    heavy = [g for g in heavy if g is not None]
    while heavy or any(s.busy() for s in light):
        for s in light:
            s.step()
        heavy = [g for g in heavy if next(g, _DONE) is not _DONE]


def _run(tasks, grid, name):
    n_in = [len(t.operands) for t in tasks]
    n_out = [len(t.out_shapes) for t in tasks]
    n_scr = [len(t.scratch) for t in tasks]

    def body(*refs):
        ins = refs[:sum(n_in)]
        outs = refs[sum(n_in):sum(n_in) + sum(n_out)]
        scrs = refs[sum(n_in) + sum(n_out):]
        heavy, light = [], []
        i = o = s = 0
        for t, ni, no, ns in zip(tasks, n_in, n_out, n_scr):
            h, l = t.stages(*ins[i:i + ni], *outs[o:o + no], *scrs[s:s + ns])
            heavy.append(h)
            light += l
            i, o, s = i + ni, o + no, s + ns
        _drive(heavy, light)

    aliases = {}
    i = o = 0
    for t, ni, no in zip(tasks, n_in, n_out):
        for src, dst in (t.aliases or {}).items():
            aliases[i + src] = o + dst
        i, o = i + ni, o + no

    flat = pl.pallas_call(
        body,
        grid=grid,
        in_specs=[s for t in tasks for s in t.in_specs],
        out_specs=tuple(s for t in tasks for s in t.out_specs),
        out_shape=tuple(s for t in tasks for s in t.out_shapes),
        scratch_shapes=[s for t in tasks for s in t.scratch],
        input_output_aliases=aliases,
        compiler_params=pltpu.CompilerParams(
            dimension_semantics=("arbitrary", "arbitrary"), vmem_limit_bytes=VMEM_LIMIT_BYTES),
        name=name,
    )(*[a for t in tasks for a in t.operands])
    results, o = [], 0
    for no in n_out:
        results.append(tuple(flat[o:o + no]))
        o += no
    return results


def _inproj_stages(x_ref, g_ref, *refs, blocks_per_group):
    n_blocks = sum(blocks_per_group)
    w_refs, o_refs = refs[:n_blocks], refs[n_blocks:]

    def pieces():
        x = x_ref[...]
        n = ((x * _rms_scale(x)) * g_ref[...]).astype(BF16)
        for _ in range(NORM_HEAD_START):
            yield
        first = 0
        for n_blk, o_ref, is_gate in zip(blocks_per_group, o_refs, IN_IS_GATE):
            for cols, w_ref, cs in _block_pieces(w_refs[first:first + n_blk]):
                u = _dot(n, w_ref[:, cs])
                o_ref[:, cols] = (_silu(u) if is_gate else u).astype(o_ref.dtype)
                yield
            first += n_blk

    return pieces(), []


IN_DTYPES = (BF16, BF16, BF16, BF16, BF16, F32, BF16, BF16)
IN_IS_GATE = (False, False, False, True, False, False, True, False)


def _inproj_task(x, batch0, n_batch, tm, g_mix, groups):
    _, length, d = x.shape
    blocks = [w for g in groups for w in g]
    widths = [sum(w.shape[1] for w in g) for g in groups]
    return _Task(
        stages=functools.partial(_inproj_stages, blocks_per_group=tuple(len(g) for g in groups)),
        operands=(x, g_mix, *[w.array for w in blocks]),
        in_specs=(_tile(tm, d, batch0), _resident(g_mix.shape), *[w.spec() for w in blocks]),
        out_specs=tuple(_tile(tm, n) for n in widths),
        out_shapes=tuple(jax.ShapeDtypeStruct((n_batch, length, n), dt)
                         for n, dt in zip(widths, IN_DTYPES)))


def _gla_stages(q_ref, k_ref, v_ref, z_ref, lr_ref, wg_ref, bg_ref, gout_ref, s0_ref,
                o_ref, s_ref, *, chunk, n_chunks):
    @pl.when(pl.program_id(1) == 0)
    def _():
        s_ref[...] = s0_ref[...]

    row = lax.broadcasted_iota(jnp.int32, (chunk, chunk), 0)
    col = lax.broadcasted_iota(jnp.int32, (chunk, chunk), 1)
    causal = row >= col
    row2 = lax.broadcasted_iota(jnp.int32, (chunk, 2 * chunk), 0)
    col2 = lax.broadcasted_iota(jnp.int32, (chunk, 2 * chunk), 1)
    tril2 = jnp.where(row2 >= lax.rem(col2, chunk), 1.0, 0.0).astype(BF16)
    eye = (lax.broadcasted_iota(jnp.int32, (GLA_DK, GLA_DK), 0)
           == lax.broadcasted_iota(jnp.int32, (GLA_DK, GLA_DK), 1))
    scale = GLA_DK ** -0.5
    heads = [(slice(h * GLA_DK, (h + 1) * GLA_DK), slice(h * GLA_DV, (h + 1) * GLA_DV))
             for h in range(GLA_HEADS)]

    def chunk_unit(c):
        r = slice(c * chunk, (c + 1) * chunk)
        gpre = _dot(lr_ref[r, :], wg_ref[...])
        yield
        gpre = gpre + bg_ref[...]
        soft = jnp.log2(1.0 + jnp.exp2(jnp.abs(gpre) * -LOG2E))
        g = jnp.minimum(gpre, 0.0) * (1.0 / GLA_TAU) - soft * (LN2 / GLA_TAU)
        g_hi = g.astype(BF16)
        g_lo = (g - g_hi.astype(F32)).astype(BF16)
        g2 = jnp.concatenate([g_hi, g_lo], axis=0)
        yield
        b = _dot(tril2, g2)
        yield
        b_last = b[chunk - 1:chunk, :]
        q = q_ref[r, :].astype(F32)
        k = k_ref[r, :].astype(F32)
        qe = ((q * scale) * jnp.exp(b)).astype(BF16)
        ke = k * jnp.exp(-b)
        kd = k * jnp.exp(b_last - b)
        decay = jnp.exp(b_last)
        v = [v_ref[r, vs] for _, vs in heads]
        ke_t = [ke[:, ks].T.astype(BF16) for ks, _ in heads]
        kd_t = [kd[:, ks].T.astype(BF16) for ks, _ in heads]
        yield
        a_raw = [_dot(qe[:, ks], ke_t[h]) for h, (ks, _) in enumerate(heads)]
        kv = [_dot(kd_t[h], v[h]) for h in range(GLA_HEADS)]
        yield
        lhs, rhs = [], []
        for h, (ks, _) in enumerate(heads):
            a = jnp.where(causal, a_raw[h], 0.0).astype(BF16)
            s = s_ref[h]
            lhs.append(jnp.concatenate([qe[:, ks], a], axis=1))
            rhs.append(jnp.concatenate([s.astype(BF16), v[h]], axis=0))
            dcol = jnp.sum(jnp.where(eye, decay[:, ks], 0.0), axis=1, keepdims=True)
            s_ref[h] = dcol * s + kv[h]
        yield
        o = [_dot(lhs[h], rhs[h]) for h in range(GLA_HEADS)]
        yield
        for h, (_, vs) in enumerate(heads):
            oh = (o[h] * _rms_scale(o[h])) * gout_ref[...]
            o_ref[r, vs] = (oh * z_ref[r, vs].astype(F32)).astype(o_ref.dtype)

    return None, [_Stream([chunk_unit(c) for c in range(n_chunks)], GLA_IN_FLIGHT)]


def _gla_task(q, k, v, z, lr, wg, bg, gout, s0, state_batch0, chunk, tb):
    n_batch, length, _ = q.shape
    st_in = pl.BlockSpec((None, GLA_HEADS, GLA_DK, GLA_DV),
                         lambda b, i: (state_batch0 + b, 0, 0, 0))
    st_out = pl.BlockSpec((None, GLA_HEADS, GLA_DK, GLA_DV), lambda b, i: (b, 0, 0, 0))
    return _Task(
        stages=functools.partial(_gla_stages, chunk=chunk, n_chunks=tb // chunk),
        operands=(q, k, v, z, lr, wg, bg, gout, s0),
        in_specs=(_tile(tb, GLA_KW), _tile(tb, GLA_KW), _tile(tb, GLA_WIDTH), _tile(tb, GLA_WIDTH),
                  _tile(tb, LANES), _resident(wg.shape), _resident(bg.shape),
                  _resident(gout.shape), st_in),
        out_specs=(_tile(tb, GLA_WIDTH), st_out),
        out_shapes=(jax.ShapeDtypeStruct((n_batch, length, GLA_WIDTH), BF16),
                    jax.ShapeDtypeStruct((n_batch, GLA_HEADS, GLA_DK, GLA_DV), F32)))


def _swa_stages(sink_ref, q_ref, kp_ref, vp_ref, kc_ref, vc_ref, z_ref, o_ref,
                *, chunk, n_chunks, mask_missing_past):
    kvw = SWA_GROUP * SWA_HEAD_DIM
    n_keys = WINDOW + chunk
    gq = SWA_GROUP * chunk
    gq_group = lax.broadcasted_iota(jnp.int32, (1, gq), 1) // chunk
    key_i = lax.broadcasted_iota(jnp.int32, (n_keys, gq), 0)
    qscale = SWA_HEAD_DIM ** -0.5
    first_pos = pl.program_id(1) * (chunk * n_chunks) - WINDOW
    shared = {}

    def head_operands(h):
        if not shared:
            shared["k"] = jnp.concatenate([kp_ref[...], kc_ref[...]], axis=0).astype(BF16)
            v = jnp.concatenate([vp_ref[...], vc_ref[...]], axis=0)
            shared["vt"] = v.T.astype(BF16)
        if h not in shared:
            hd = slice(h * SWA_HEAD_DIM, (h + 1) * SWA_HEAD_DIM)
            k_h = shared["k"][:, hd]
            vt = shared["vt"][hd, :]
            vt_at = {}
            for c in range(n_chunks):
                shift = (c * chunk) % LANES
                if shift not in vt_at:
                    vt_at[shift] = vt[:, shift:]
            sink = jnp.zeros((1, gq), F32)
            for g in range(SWA_GROUP):
                sink = jnp.where(gq_group == g, sink_ref[h * SWA_GROUP + g], sink)
            shared[h] = (k_h, vt_at, sink)
        return shared[h]

    def unit(h, c):
        k_h, vt_at, sink = head_operands(h)
        rows = slice(c * chunk, (c + 1) * chunk)
        keys = slice(c * chunk, c * chunk + n_keys)
        cols = slice(h * kvw, (h + 1) * kvw)
        qt = q_ref[rows, cols].astype(F32).T
        qg = jnp.concatenate(
            [qt[g * SWA_HEAD_DIM:(g + 1) * SWA_HEAD_DIM] for g in range(SWA_GROUP)],
            axis=1).astype(BF16)
        shift = (c * chunk) % LANES
        vt_keys = vt_at[shift][:, c * chunk - shift:c * chunk - shift + n_keys]
        yield
        s = _dot(k_h[keys], qg)
        yield
        if mask_missing_past and c * chunk < WINDOW:
            s = jnp.where(key_i + (first_pos + c * chunk) >= 0, s, jnp.finfo(F32).min)
        m = jnp.maximum(jnp.max(s, axis=0, keepdims=True) * qscale, sink) * LOG2E
        p = jnp.exp2(s * (qscale * LOG2E) - m)
        denom = jnp.sum(p, axis=0, keepdims=True) + jnp.exp2(sink * LOG2E - m)
        p = p.astype(BF16)
        yield
        o_t = _dot(vt_keys, p)
        yield
        o_t = (o_t / denom).T
        o = jnp.concatenate([o_t[g * chunk:(g + 1) * chunk] for g in range(SWA_GROUP)],
                            axis=1)
        o_ref[rows, cols] = (o * z_ref[rows, cols].astype(F32)).astype(o_ref.dtype)

    units = [unit(h, c) for h in range(SWA_KV_HEADS) for c in range(n_chunks)]
    return None, [_Stream(units, SWA_IN_FLIGHT)]


def _swa_task(sinks, q, z, kv, past, chunk, qb):
    n_batch, length, _ = q.shape
    if past is None:
        per_win = qb // WINDOW
        prev = lambda col: pl.BlockSpec(
            (None, WINDOW, SWA_KVW), lambda b, i: (b, jnp.maximum(i * per_win - 1, 0), col))
        k_past, v_past, kp_spec, vp_spec = kv, kv, prev(0), prev(1)
    else:
        k_past, v_past = past
        kp_spec = vp_spec = pl.BlockSpec((None, WINDOW, SWA_KVW), lambda b, i: (b, 0, 0))
    cur = lambda col: pl.BlockSpec((None, qb, SWA_KVW), lambda b, i: (b, i, col))
    return _Task(
        stages=functools.partial(_swa_stages, chunk=chunk, n_chunks=qb // chunk,
                                 mask_missing_past=past is None),
        operands=(sinks, q, k_past, v_past, kv, kv, z),
        in_specs=(pl.BlockSpec(memory_space=pltpu.SMEM), _tile(qb, SWA_WIDTH), kp_spec, vp_spec,
                  cur(0), cur(1), _tile(qb, SWA_WIDTH)),
        out_specs=(_tile(qb, SWA_WIDTH),),
        out_shapes=(jax.ShapeDtypeStruct((n_batch, length, SWA_WIDTH), BF16),))


def _outproj_stages(x_ref, o1_ref, o2_ref, p_ref, gple_ref, gfin_ref, *refs, n_blocks,
                    own_batch):
    wo, wg, wp = (refs[j * n_blocks:(j + 1) * n_blocks] for j in range(3))
    y_ref = refs[-1]
    if own_batch is not None:
        for b in range(y_ref.shape[0]):
            if b != own_batch:
                y_ref[b] = jnp.zeros(y_ref.shape[1:], y_ref.dtype)
        y_ref = y_ref.at[own_batch]
    h_ref = y_ref

    def pieces():
        o = jnp.concatenate([o1_ref[...], o2_ref[...]], axis=1)
        n = []
        for cols, w_ref, cs in _block_pieces(wo):
            h = x_ref[:, cols] + _dot(o, w_ref[:, cs])
            h_ref[:, cols] = h
            n.append((h * gple_ref[:, cols]).astype(BF16))
            yield
        n = jnp.concatenate(n, axis=1)
        r = _rms_scale(h_ref[...]) * -LOG2E
        p = p_ref[...].astype(BF16)
        tm = h_ref.shape[0]
        for rows in [slice(r0, r0 + OUT_ROW_GROUP) for r0 in range(0, tm, OUT_ROW_GROUP)]:
            for (cols, wg_ref, cs), (_, wp_ref, _) in zip(_block_pieces(wg), _block_pieces(wp)):
                gate = 1.0 / (1.0 + jnp.exp2(_dot(n[rows], wg_ref[:, cs]) * r[rows]))
                h_ref[rows, cols] = h_ref[rows, cols] + gate * _dot(p[rows], wp_ref[:, cs])
                yield
            h = h_ref[rows, :]
            y_ref[rows, :] = (h * _rms_scale(h)) * gfin_ref[...]

    return pieces(), []


def _outproj_task(x, o1, o2, p, batch0, tm, wo, gple, wg, wp, gfin, y_prev=None):
    _, _, d = x.shape
    assert len(wo) == len(wg) == len(wp)
    blocks = (*wo, *wg, *wp)
    operands = (x, o1, o2, p, gple, gfin, *[w.array for w in blocks])
    in_specs = (_tile(tm, d, batch0), _tile(tm, o1.shape[2]), _tile(tm, o2.shape[2]),
                _tile(tm, p.shape[2], batch0), _resident(gple.shape), _resident(gfin.shape),
                *[w.spec() for w in blocks])
    aliases, own_batch, y_spec = None, None, _tile(tm, d, batch0)
    if y_prev is not None:
        aliases = {len(operands): 0}
        operands += (y_prev,)
        in_specs += (pl.BlockSpec(memory_space=pl.ANY),)
    elif x.shape[0] > 1:
        own_batch = batch0
        y_spec = pl.BlockSpec((x.shape[0], tm, d), lambda b, i: (0, i, 0))
    return _Task(stages=functools.partial(_outproj_stages, n_blocks=len(wo), own_batch=own_batch),
                 operands=operands, in_specs=in_specs,
                 out_specs=(y_spec,),
                 out_shapes=(jax.ShapeDtypeStruct(x.shape, F32),),
                 aliases=aliases)


def _split_w_in(w_in):
    lr0 = 2 * GLA_KW + 2 * GLA_WIDTH
    n_rows, n_in = w_in.shape
    n_main = n_in - GLA_RANK
    assert lr0 % WEIGHT_BLOCK == 0 and n_main % WEIGHT_BLOCK == 0

    def relayout(w_ref, o_ref):
        o_ref[...] = w_ref[...].T.astype(BF16)

    assert WEIGHT_BLOCK % GLA_RANK == 0
    src_col = lambda j: GLA_RANK * (j * (WEIGHT_BLOCK // GLA_RANK)
                                    + (j >= lr0 // WEIGHT_BLOCK).astype(jnp.int32))
    w = pl.pallas_call(
        relayout,
        grid=(n_main // WEIGHT_BLOCK,),
        in_specs=[pl.BlockSpec((pl.Element(WEIGHT_BLOCK), pl.Element(n_rows)),
                               lambda j: (src_col(j), 0))],
        out_specs=pl.BlockSpec((n_rows, WEIGHT_BLOCK), lambda j: (0, j)),
        out_shape=jax.ShapeDtypeStruct((n_rows, n_main), BF16),
        compiler_params=pltpu.CompilerParams(
            dimension_semantics=("arbitrary",), vmem_limit_bytes=VMEM_LIMIT_BYTES),
        name="w_in_relayout",
    )(w_in.T)

    def relayout_gate(w_ref, o_ref):
        pad = jnp.zeros((LANES - GLA_RANK, n_rows), F32)
        o_ref[...] = jnp.concatenate([w_ref[...], pad], axis=0).T.astype(BF16)

    w_lr = pl.pallas_call(
        relayout_gate,
        grid=(1,),
        in_specs=[pl.BlockSpec((pl.Element(GLA_RANK), pl.Element(n_rows)), lambda j: (lr0, 0))],
        out_specs=pl.BlockSpec((n_rows, LANES), lambda j: (0, 0)),
        out_shape=jax.ShapeDtypeStruct((n_rows, LANES), BF16),
        name="w_gate_relayout",
    )(w_in.T)
    widths = (GLA_KW, GLA_KW, GLA_WIDTH, GLA_WIDTH, SWA_WIDTH, 2 * SWA_KVW, SWA_WIDTH)
    groups, col0 = [], 0
    for width in widths:
        groups.append(_col_blocks(w, col0, width))
        col0 += width
    return (*groups, _col_blocks(w_lr))


class _Weights(NamedTuple):
    g_mix: Any
    w_in: tuple
    w_gate: Any
    b_gate: Any
    g_gla_out: Any
    sinks: Any
    w_out: Any
    g_ple: Any
    w_ple_gate: Any
    w_ple_proj: Any
    g_final: Any


def _attn_tasks(w, proj, s0, state_batch0, past, chunk, tile, swa_first):
    q1, k1, v1, z1, q2, kv2, z2, lr = proj
    gla = _gla_task(q1, k1, v1, z1, lr, w.w_gate, w.b_gate, w.g_gla_out, s0, state_batch0,
                    chunk, tile)
    swa = _swa_task(w.sinks, q2, z2, kv2, past, chunk, tile)
    return [swa, gla] if swa_first else [gla, swa]


def _attn_results(res, swa_first):
    (o2,), (o1, s_new) = (res[0], res[1]) if swa_first else (res[1], res[0])
    return o1, o2, s_new


def _prompt_path(w, x, p, s0):
    n_batch, length, _ = x.shape
    assert n_batch == 2, "a call holding all three stages would not fit the VMEM budget"
    proj, attn = {}, {}
    y = None
    for s in range(n_batch + 2):
        has_in, has_attn, has_out = s < n_batch, 1 <= s <= n_batch, s >= 2
        tile = TILE_FUSED if has_attn else TILE_MATMUL_ONLY
        tasks, names = [], []
        if has_attn:
            tasks += _attn_tasks(w, proj[s - 1], s0, s - 1, None, CHUNK, tile, swa_first=has_out)
            names.append("attn")
        if has_in:
            tasks.append(_inproj_task(x, s, 1, tile, w.g_mix, w.w_in))
            names.append("inproj")
        if has_out:
            o1, o2, _ = attn[s - 2]
            tasks.append(_outproj_task(x, o1, o2, p, s - 2, tile, w.w_out, w.g_ple,
                                       w.w_ple_gate, w.w_ple_proj, w.g_final, y_prev=y))
            names.append("outproj")
        res = _run(tasks, (1, length // tile), "_".join(names))
        if has_attn:
            attn[s - 1] = _attn_results(res, swa_first=has_out)
            res = res[2:]
        if has_in:
            proj[s] = res[0]
            res = res[1:]
        if has_out:
            (y,) = res[0]
    states = jnp.concatenate([attn[b][2] for b in range(n_batch)], axis=0)
    kv = [proj[b][5] for b in range(n_batch)]
    return y, states, kv


def _sample_path(w, x, p, s0, past):
    n_batch, length, d = x.shape
    flat = lambda a: a.reshape(1, n_batch * length, a.shape[-1])
    per_batch = lambda a: a.reshape(n_batch, length, a.shape[-1])
    (proj,) = _run([_inproj_task(flat(x), 0, 1, n_batch * length, w.g_mix, w.w_in)], (1, 1),
                   "inproj_sample")
    proj = tuple(per_batch(a) for a in proj)
    o1, o2, s_new = _attn_results(
        _run(_attn_tasks(w, proj, s0, 0, past, min(CHUNK, length), length, swa_first=True),
             (n_batch, 1), "attn_sample"), swa_first=True)
    ((y,),) = _run([_outproj_task(flat(x), flat(o1), flat(o2), flat(p), 0, n_batch * length,
                                  w.w_out, w.g_ple, w.w_ple_gate, w.w_ple_proj, w.g_final)],
                   (1, 1), "outproj_sample")
    return y.reshape(n_batch, length, d), s_new, proj[5]


def kernel(x_prompt, x_sample, p_prompt, p_sample, state_gla, cache_swa_k, cache_swa_v, g_mix, w_in, w_gate_up, b_gate, g_gla_out, swa_sinks, w_out, g_ple, w_ple_gate, w_ple_proj, g_final):
    depth = w_in.shape[0]
    assert depth == 1, "the final RMSNorm is fused into the last layer's output kernel"
    bp, lp, _ = x_prompt.shape
    bs, ls, _ = x_sample.shape
    assert cache_swa_k.shape[2] == WINDOW
    i = 0
    row = lambda a: a.reshape(1, -1).astype(F32)
    w = _Weights(
        g_mix=row(g_mix[i]), w_in=_split_w_in(w_in[i]),
        w_gate=jnp.pad(w_gate_up[i].astype(BF16), ((0, LANES - GLA_RANK), (0, 0))),
        b_gate=row(b_gate[i]), g_gla_out=row(g_gla_out[i]),
        sinks=swa_sinks[i].reshape(-1).astype(F32),
        w_out=_col_blocks(w_out[i].astype(BF16)),
        g_ple=row(g_ple[i]), w_ple_gate=_col_blocks(w_ple_gate[i].astype(BF16)),
        w_ple_proj=_col_blocks(w_ple_proj[i].astype(BF16)), g_final=row(g_final))

    s0_p = jnp.zeros((bp, GLA_HEADS, GLA_DK, GLA_DV), F32)
    y_p, s_p, kv_p = _prompt_path(w, x_prompt, p_prompt[i], s0_p)
    keep = min(WINDOW, lp)
    last = lambda col: jnp.concatenate(
        [a[:, lp - keep:, col * SWA_KVW:(col + 1) * SWA_KVW] for a in kv_p], axis=0
    ).reshape(bp, keep, SWA_KV_HEADS, SWA_HEAD_DIM)
    k_p, v_p = last(0), last(1)

    past = (cache_swa_k[i].reshape(bs, WINDOW, SWA_KVW), cache_swa_v[i].reshape(bs, WINDOW, SWA_KVW))
    y_s, s_s, kv_s = _sample_path(w, x_sample, p_sample[i], state_gla[i], past)
    k_s = kv_s[:, :, :SWA_KVW].reshape(bs, ls, SWA_KV_HEADS, SWA_HEAD_DIM)
    v_s = kv_s[:, :, SWA_KVW:].reshape(bs, ls, SWA_KV_HEADS, SWA_HEAD_DIM)
    return (y_p, y_s, s_p[None], k_p[None], v_p[None], s_s[None], k_s[None], v_s[None])
```

```python
import functools
from typing import Any, Callable, NamedTuple

import jax
import jax.numpy as jnp
from jax import lax
from jax.experimental import pallas as pl
from jax.experimental.pallas import tpu as pltpu

F32 = jnp.float32
BF16 = jnp.bfloat16

CHUNK = 64
RMS_EPS = 1e-6
GLA_HEADS = 4
GLA_DK = 128
GLA_DV = 256
GLA_KW = GLA_HEADS * GLA_DK
GLA_WIDTH = GLA_HEADS * GLA_DV
GLA_RANK = 16
GLA_TAU = 16.0
SWA_KV_HEADS = 4
SWA_GROUP = 4
SWA_HEAD_DIM = 64
SWA_WIDTH = SWA_KV_HEADS * SWA_GROUP * SWA_HEAD_DIM
SWA_KVW = SWA_KV_HEADS * SWA_HEAD_DIM
WINDOW = 128

LANES = 128
MXU_WIDTH = 256
WEIGHT_BLOCK = 512
VMEM_LIMIT_BYTES = 56 * 1024 * 1024
TILE_MATMUL_ONLY = 512
TILE_FUSED = 256
OUT_ROW_GROUP = 256
GLA_IN_FLIGHT = 3
SWA_IN_FLIGHT = 4
NORM_HEAD_START = 1

LOG2E = 1.4426950408889634
LN2 = 0.6931471805599453


def _dot(a, b):
    return jnp.dot(a, b, preferred_element_type=F32)


def _rms_scale(x):
    return lax.rsqrt(jnp.mean(x * x, axis=-1, keepdims=True) + RMS_EPS)


def _sigmoid(x):
    return 1.0 / (1.0 + jnp.exp2(x * -LOG2E))


def _silu(x):
    return x * _sigmoid(x)


def _resident(shape):
    zeros = (0,) * len(shape)
    return pl.BlockSpec(shape, lambda *_: zeros, pipeline_mode=pl.Buffered(1))


def _tile(tm, width, batch0=0):
    return pl.BlockSpec((None, tm, width), lambda b, i: (batch0 + b, i, 0))


class _Block(NamedTuple):
    array: Any
    shape: tuple
    index: tuple

    def spec(self):
        index = self.index
        return pl.BlockSpec(self.shape, lambda *_: index, pipeline_mode=pl.Buffered(1))


def _col_blocks(w, col0=0, width=None):
    width = width or w.shape[1] - col0
    blk = min(WEIGHT_BLOCK, width)
    assert col0 % blk == 0 and width % blk == 0
    return tuple(_Block(w, (w.shape[0], blk), (0, (col0 + c) // blk))
                 for c in range(0, width, blk))


def _block_pieces(block_refs):
    out, base = [], 0
    for ref in block_refs:
        width = ref.shape[1]
        for c in range(0, width, MXU_WIDTH):
            hi = min(c + MXU_WIDTH, width)
            out.append((slice(base + c, base + hi), ref, slice(c, hi)))
        base += width
    return out


class _Task(NamedTuple):
    stages: Callable[..., Any]
    operands: tuple
    in_specs: tuple
    out_specs: tuple
    out_shapes: tuple
    scratch: tuple = ()
    aliases: Any = None


_DONE = object()


class _Stream:
    def __init__(self, units, max_active):
        self.pending, self.active, self.max_active = list(units), [], max_active

    def busy(self):
        return bool(self.pending or self.active)

    def step(self):
        self.active = [g for g in self.active if next(g, _DONE) is not _DONE]
        if self.pending and len(self.active) < self.max_active:
            g = self.pending.pop(0)
            if next(g, _DONE) is not _DONE:
                self.active.append(g)


def _drive(heavy, light):
    heavy = [g for g in heavy if g is not None]
    while heavy or any(s.busy() for s in light):
        for s in light:
            s.step()
        heavy = [g for g in heavy if next(g, _DONE) is not _DONE]


def _run(tasks, grid, name):
    n_in = [len(t.operands) for t in tasks]
    n_out = [len(t.out_shapes) for t in tasks]
    n_scr = [len(t.scratch) for t in tasks]

    def body(*refs):
        ins = refs[:sum(n_in)]
        outs = refs[sum(n_in):sum(n_in) + sum(n_out)]
        scrs = refs[sum(n_in) + sum(n_out):]
        heavy, light = [], []
        i = o = s = 0
        for t, ni, no, ns in zip(tasks, n_in, n_out, n_scr):
            h, l = t.stages(*ins[i:i + ni], *outs[o:o + no], *scrs[s:s + ns])
            heavy.append(h)
            light += l
            i, o, s = i + ni, o + no, s + ns
        _drive(heavy, light)

    aliases = {}
    i = o = 0
    for t, ni, no in zip(tasks, n_in, n_out):
        for src, dst in (t.aliases or {}).items():
            aliases[i + src] = o + dst
        i, o = i + ni, o + no

    flat = pl.pallas_call(
        body,
        grid=grid,
        in_specs=[s for t in tasks for s in t.in_specs],
        out_specs=tuple(s for t in tasks for s in t.out_specs),
        out_shape=tuple(s for t in tasks for s in t.out_shapes),
        scratch_shapes=[s for t in tasks for s in t.scratch],
        input_output_aliases=aliases,
        compiler_params=pltpu.CompilerParams(
            dimension_semantics=("arbitrary", "arbitrary"), vmem_limit_bytes=VMEM_LIMIT_BYTES),
        name=name,
    )(*[a for t in tasks for a in t.operands])
    results, o = [], 0
    for no in n_out:
        results.append(tuple(flat[o:o + no]))
        o += no
    return results


def _inproj_stages(x_ref, g_ref, *refs, blocks_per_group):
    n_blocks = sum(blocks_per_group)
    w_refs, o_refs = refs[:n_blocks], refs[n_blocks:]

    def pieces():
        x = x_ref[...]
        n = ((x * _rms_scale(x)) * g_ref[...]).astype(BF16)
        for _ in range(NORM_HEAD_START):
            yield
        first = 0
        for n_blk, o_ref, is_gate in zip(blocks_per_group, o_refs, IN_IS_GATE):
            for cols, w_ref, cs in _block_pieces(w_refs[first:first + n_blk]):
                u = _dot(n, w_ref[:, cs])
                o_ref[:, cols] = (_silu(u) if is_gate else u).astype(o_ref.dtype)
                yield
            first += n_blk

    return pieces(), []


IN_DTYPES = (BF16, BF16, BF16, BF16, BF16, F32, BF16, BF16)
IN_IS_GATE = (False, False, False, True, False, False, True, False)


def _inproj_task(x, batch0, n_batch, tm, g_mix, groups):
    _, length, d = x.shape
    blocks = [w for g in groups for w in g]
    widths = [sum(w.shape[1] for w in g) for g in groups]
    return _Task(
        stages=functools.partial(_inproj_stages, blocks_per_group=tuple(len(g) for g in groups)),
        operands=(x, g_mix, *[w.array for w in blocks]),
        in_specs=(_tile(tm, d, batch0), _resident(g_mix.shape), *[w.spec() for w in blocks]),
        out_specs=tuple(_tile(tm, n) for n in widths),
        out_shapes=tuple(jax.ShapeDtypeStruct((n_batch, length, n), dt)
                         for n, dt in zip(widths, IN_DTYPES)))


def _gla_stages(q_ref, k_ref, v_ref, z_ref, lr_ref, wg_ref, bg_ref, gout_ref, s0_ref,
                o_ref, s_ref, *, chunk, n_chunks):
    @pl.when(pl.program_id(1) == 0)
    def _():
        s_ref[...] = s0_ref[...]

    row = lax.broadcasted_iota(jnp.int32, (chunk, chunk), 0)
    col = lax.broadcasted_iota(jnp.int32, (chunk, chunk), 1)
    causal = row >= col
    row2 = lax.broadcasted_iota(jnp.int32, (chunk, 2 * chunk), 0)
    col2 = lax.broadcasted_iota(jnp.int32, (chunk, 2 * chunk), 1)
    tril2 = jnp.where(row2 >= lax.rem(col2, chunk), 1.0, 0.0).astype(BF16)
    eye = (lax.broadcasted_iota(jnp.int32, (GLA_DK, GLA_DK), 0)
           == lax.broadcasted_iota(jnp.int32, (GLA_DK, GLA_DK), 1))
    scale = GLA_DK ** -0.5
    heads = [(slice(h * GLA_DK, (h + 1) * GLA_DK), slice(h * GLA_DV, (h + 1) * GLA_DV))
             for h in range(GLA_HEADS)]

    def chunk_unit(c):
        r = slice(c * chunk, (c + 1) * chunk)
        gpre = _dot(lr_ref[r, :], wg_ref[...])
        yield
        gpre = gpre + bg_ref[...]
        soft = jnp.log2(1.0 + jnp.exp2(jnp.abs(gpre) * -LOG2E))
        g = jnp.minimum(gpre, 0.0) * (1.0 / GLA_TAU) - soft * (LN2 / GLA_TAU)
        g_hi = g.astype(BF16)
        g_lo = (g - g_hi.astype(F32)).astype(BF16)
        g2 = jnp.concatenate([g_hi, g_lo], axis=0)
        yield
        b = _dot(tril2, g2)
        yield
        b_last = b[chunk - 1:chunk, :]
        q = q_ref[r, :].astype(F32)
        k = k_ref[r, :].astype(F32)
        qe = ((q * scale) * jnp.exp(b)).astype(BF16)
        ke = k * jnp.exp(-b)
        kd = k * jnp.exp(b_last - b)
        decay = jnp.exp(b_last)
        v = [v_ref[r, vs] for _, vs in heads]
        ke_t = [ke[:, ks].T.astype(BF16) for ks, _ in heads]
        kd_t = [kd[:, ks].T.astype(BF16) for ks, _ in heads]
        yield
        a_raw = [_dot(qe[:, ks], ke_t[h]) for h, (ks, _) in enumerate(heads)]
        kv = [_dot(kd_t[h], v[h]) for h in range(GLA_HEADS)]
        yield
        lhs, rhs = [], []
        for h, (ks, _) in enumerate(heads):
            a = jnp.where(causal, a_raw[h], 0.0).astype(BF16)
            s = s_ref[h]
            lhs.append(jnp.concatenate([qe[:, ks], a], axis=1))
            rhs.append(jnp.concatenate([s.astype(BF16), v[h]], axis=0))
            dcol = jnp.sum(jnp.where(eye, decay[:, ks], 0.0), axis=1, keepdims=True)
            s_ref[h] = dcol * s + kv[h]
        yield
        o = [_dot(lhs[h], rhs[h]) for h in range(GLA_HEADS)]
        yield
        for h, (_, vs) in enumerate(heads):
            oh = (o[h] * _rms_scale(o[h])) * gout_ref[...]
            o_ref[r, vs] = (oh * z_ref[r, vs].astype(F32)).astype(o_ref.dtype)

    return None, [_Stream([chunk_unit(c) for c in range(n_chunks)], GLA_IN_FLIGHT)]


def _gla_task(q, k, v, z, lr, wg, bg, gout, s0, state_batch0, chunk, tb):
    n_batch, length, _ = q.shape
    st_in = pl.BlockSpec((None, GLA_HEADS, GLA_DK, GLA_DV),
                         lambda b, i: (state_batch0 + b, 0, 0, 0))
    st_out = pl.BlockSpec((None, GLA_HEADS, GLA_DK, GLA_DV), lambda b, i: (b, 0, 0, 0))
    return _Task(
        stages=functools.partial(_gla_stages, chunk=chunk, n_chunks=tb // chunk),
        operands=(q, k, v, z, lr, wg, bg, gout, s0),
        in_specs=(_tile(tb, GLA_KW), _tile(tb, GLA_KW), _tile(tb, GLA_WIDTH), _tile(tb, GLA_WIDTH),
                  _tile(tb, LANES), _resident(wg.shape), _resident(bg.shape),
                  _resident(gout.shape), st_in),
        out_specs=(_tile(tb, GLA_WIDTH), st_out),
        out_shapes=(jax.ShapeDtypeStruct((n_batch, length, GLA_WIDTH), BF16),
                    jax.ShapeDtypeStruct((n_batch, GLA_HEADS, GLA_DK, GLA_DV), F32)))


def _swa_stages(sink_ref, q_ref, kp_ref, vp_ref, kc_ref, vc_ref, z_ref, o_ref,
                *, chunk, n_chunks, mask_missing_past):
    kvw = SWA_GROUP * SWA_HEAD_DIM
    n_keys = WINDOW + chunk
    gq = SWA_GROUP * chunk
    gq_group = lax.broadcasted_iota(jnp.int32, (1, gq), 1) // chunk
    key_i = lax.broadcasted_iota(jnp.int32, (n_keys, gq), 0)
    qscale = SWA_HEAD_DIM ** -0.5
    first_pos = pl.program_id(1) * (chunk * n_chunks) - WINDOW
    shared = {}

    def head_operands(h):
        if not shared:
            shared["k"] = jnp.concatenate([kp_ref[...], kc_ref[...]], axis=0).astype(BF16)
            v = jnp.concatenate([vp_ref[...], vc_ref[...]], axis=0)
            shared["vt"] = v.T.astype(BF16)
        if h not in shared:
            hd = slice(h * SWA_HEAD_DIM, (h + 1) * SWA_HEAD_DIM)
            k_h = shared["k"][:, hd]
            vt = shared["vt"][hd, :]
            vt_at = {}
            for c in range(n_chunks):
                shift = (c * chunk) % LANES
                if shift not in vt_at:
                    vt_at[shift] = vt[:, shift:]
            sink = jnp.zeros((1, gq), F32)
            for g in range(SWA_GROUP):
                sink = jnp.where(gq_group == g, sink_ref[h * SWA_GROUP + g], sink)
            shared[h] = (k_h, vt_at, sink)
        return shared[h]

    def unit(h, c):
        k_h, vt_at, sink = head_operands(h)
        rows = slice(c * chunk, (c + 1) * chunk)
        keys = slice(c * chunk, c * chunk + n_keys)
        cols = slice(h * kvw, (h + 1) * kvw)
        qt = q_ref[rows, cols].astype(F32).T
        qg = jnp.concatenate(
            [qt[g * SWA_HEAD_DIM:(g + 1) * SWA_HEAD_DIM] for g in range(SWA_GROUP)],
            axis=1).astype(BF16)
        shift = (c * chunk) % LANES
        vt_keys = vt_at[shift][:, c * chunk - shift:c * chunk - shift + n_keys]
        yield
        s = _dot(k_h[keys], qg)
        yield
        if mask_missing_past and c * chunk < WINDOW:
            s = jnp.where(key_i + (first_pos + c * chunk) >= 0, s, jnp.finfo(F32).min)
        m = jnp.maximum(jnp.max(s, axis=0, keepdims=True) * qscale, sink) * LOG2E
        p = jnp.exp2(s * (qscale * LOG2E) - m)
        denom = jnp.sum(p, axis=0, keepdims=True) + jnp.exp2(sink * LOG2E - m)
        p = p.astype(BF16)
        yield
        o_t = _dot(vt_keys, p)
        yield
        o_t = (o_t / denom).T
        o = jnp.concatenate([o_t[g * chunk:(g + 1) * chunk] for g in range(SWA_GROUP)],
                            axis=1)
        o_ref[rows, cols] = (o * z_ref[rows, cols].astype(F32)).astype(o_ref.dtype)

    units = [unit(h, c) for h in range(SWA_KV_HEADS) for c in range(n_chunks)]
    return None, [_Stream(units, SWA_IN_FLIGHT)]


def _swa_task(sinks, q, z, kv, past, chunk, qb):
    n_batch, length, _ = q.shape
    if past is None:
        per_win = qb // WINDOW
        prev = lambda col: pl.BlockSpec(
            (None, WINDOW, SWA_KVW), lambda b, i: (b, jnp.maximum(i * per_win - 1, 0), col))
        k_past, v_past, kp_spec, vp_spec = kv, kv, prev(0), prev(1)
    else:
        k_past, v_past = past
        kp_spec = vp_spec = pl.BlockSpec((None, WINDOW, SWA_KVW), lambda b, i: (b, 0, 0))
    cur = lambda col: pl.BlockSpec((None, qb, SWA_KVW), lambda b, i: (b, i, col))
    return _Task(
        stages=functools.partial(_swa_stages, chunk=chunk, n_chunks=qb // chunk,
                                 mask_missing_past=past is None),
        operands=(sinks, q, k_past, v_past, kv, kv, z),
        in_specs=(pl.BlockSpec(memory_space=pltpu.SMEM), _tile(qb, SWA_WIDTH), kp_spec, vp_spec,
                  cur(0), cur(1), _tile(qb, SWA_WIDTH)),
        out_specs=(_tile(qb, SWA_WIDTH),),
        out_shapes=(jax.ShapeDtypeStruct((n_batch, length, SWA_WIDTH), BF16),))


def _outproj_stages(x_ref, o1_ref, o2_ref, p_ref, gple_ref, gfin_ref, *refs, n_blocks,
                    own_batch):
    wo, wg, wp = (refs[j * n_blocks:(j + 1) * n_blocks] for j in range(3))
    y_ref = refs[-1]
    if own_batch is not None:
        for b in range(y_ref.shape[0]):
            if b != own_batch:
                y_ref[b] = jnp.zeros(y_ref.shape[1:], y_ref.dtype)
        y_ref = y_ref.at[own_batch]
    h_ref = y_ref

    def pieces():
        o = jnp.concatenate([o1_ref[...], o2_ref[...]], axis=1)
        n = []
        for cols, w_ref, cs in _block_pieces(wo):
            h = x_ref[:, cols] + _dot(o, w_ref[:, cs])
            h_ref[:, cols] = h
            n.append((h * gple_ref[:, cols]).astype(BF16))
            yield
        n = jnp.concatenate(n, axis=1)
        r = _rms_scale(h_ref[...]) * -LOG2E
        p = p_ref[...].astype(BF16)
        tm = h_ref.shape[0]
        for rows in [slice(r0, r0 + OUT_ROW_GROUP) for r0 in range(0, tm, OUT_ROW_GROUP)]:
            for (cols, wg_ref, cs), (_, wp_ref, _) in zip(_block_pieces(wg), _block_pieces(wp)):
                gate = 1.0 / (1.0 + jnp.exp2(_dot(n[rows], wg_ref[:, cs]) * r[rows]))
                h_ref[rows, cols] = h_ref[rows, cols] + gate * _dot(p[rows], wp_ref[:, cs])
                yield
            h = h_ref[rows, :]
            y_ref[rows, :] = (h * _rms_scale(h)) * gfin_ref[...]

    return pieces(), []


def _outproj_task(x, o1, o2, p, batch0, tm, wo, gple, wg, wp, gfin, y_prev=None):
    _, _, d = x.shape
    assert len(wo) == len(wg) == len(wp)
    blocks = (*wo, *wg, *wp)
    operands = (x, o1, o2, p, gple, gfin, *[w.array for w in blocks])
    in_specs = (_tile(tm, d, batch0), _tile(tm, o1.shape[2]), _tile(tm, o2.shape[2]),
                _tile(tm, p.shape[2], batch0), _resident(gple.shape), _resident(gfin.shape),
                *[w.spec() for w in blocks])
    aliases, own_batch, y_spec = None, None, _tile(tm, d, batch0)
    if y_prev is not None:
        aliases = {len(operands): 0}
        operands += (y_prev,)
        in_specs += (pl.BlockSpec(memory_space=pl.ANY),)
    elif x.shape[0] > 1:
        own_batch = batch0
        y_spec = pl.BlockSpec((x.shape[0], tm, d), lambda b, i: (0, i, 0))
    return _Task(stages=functools.partial(_outproj_stages, n_blocks=len(wo), own_batch=own_batch),
                 operands=operands, in_specs=in_specs,
                 out_specs=(y_spec,),
                 out_shapes=(jax.ShapeDtypeStruct(x.shape, F32),),
                 aliases=aliases)


def _split_w_in(w_in):
    lr0 = 2 * GLA_KW + 2 * GLA_WIDTH
    n_rows, n_in = w_in.shape
    n_main = n_in - GLA_RANK
    assert lr0 % WEIGHT_BLOCK == 0 and n_main % WEIGHT_BLOCK == 0

    def relayout(w_ref, o_ref):
        o_ref[...] = w_ref[...].T.astype(BF16)

    assert WEIGHT_BLOCK % GLA_RANK == 0
    src_col = lambda j: GLA_RANK * (j * (WEIGHT_BLOCK // GLA_RANK)
                                    + (j >= lr0 // WEIGHT_BLOCK).astype(jnp.int32))
    w = pl.pallas_call(
        relayout,
        grid=(n_main // WEIGHT_BLOCK,),
        in_specs=[pl.BlockSpec((pl.Element(WEIGHT_BLOCK), pl.Element(n_rows)),
                               lambda j: (src_col(j), 0))],
        out_specs=pl.BlockSpec((n_rows, WEIGHT_BLOCK), lambda j: (0, j)),
        out_shape=jax.ShapeDtypeStruct((n_rows, n_main), BF16),
        compiler_params=pltpu.CompilerParams(
            dimension_semantics=("arbitrary",), vmem_limit_bytes=VMEM_LIMIT_BYTES),
        name="w_in_relayout",
    )(w_in.T)

    def relayout_gate(w_ref, o_ref):
        pad = jnp.zeros((LANES - GLA_RANK, n_rows), F32)
        o_ref[...] = jnp.concatenate([w_ref[...], pad], axis=0).T.astype(BF16)

    w_lr = pl.pallas_call(
        relayout_gate,
        grid=(1,),
        in_specs=[pl.BlockSpec((pl.Element(GLA_RANK), pl.Element(n_rows)), lambda j: (lr0, 0))],
        out_specs=pl.BlockSpec((n_rows, LANES), lambda j: (0, 0)),
        out_shape=jax.ShapeDtypeStruct((n_rows, LANES), BF16),
        name="w_gate_relayout",
    )(w_in.T)
    widths = (GLA_KW, GLA_KW, GLA_WIDTH, GLA_WIDTH, SWA_WIDTH, 2 * SWA_KVW, SWA_WIDTH)
    groups, col0 = [], 0
    for width in widths:
        groups.append(_col_blocks(w, col0, width))
        col0 += width
    return (*groups, _col_blocks(w_lr))


class _Weights(NamedTuple):
    g_mix: Any
    w_in: tuple
    w_gate: Any
    b_gate: Any
    g_gla_out: Any
    sinks: Any
    w_out: Any
    g_ple: Any
    w_ple_gate: Any
    w_ple_proj: Any
    g_final: Any


def _cast_stages(*refs):
    n = len(refs) // 2

    def unit():
        for src_ref, dst_ref in zip(refs[:n], refs[n:]):
            dst_ref[...] = src_ref[...].astype(dst_ref.dtype)
        return
        yield

    return None, [_Stream([unit()], 1)]


def _cast_task(arrays, n_steps):
    slab = lambda a: pl.BlockSpec((a.shape[0] // n_steps, a.shape[1]), lambda b, i: (i, 0))
    return _Task(stages=_cast_stages, operands=tuple(arrays),
                 in_specs=tuple(slab(a) for a in arrays), out_specs=tuple(slab(a) for a in arrays),
                 out_shapes=tuple(jax.ShapeDtypeStruct(a.shape, BF16) for a in arrays))


def _attn_tasks(w, proj, s0, state_batch0, past, chunk, tile):
    q1, k1, v1, z1, q2, kv2, z2, lr = proj
    return [_gla_task(q1, k1, v1, z1, lr, w.w_gate, w.b_gate, w.g_gla_out, s0, state_batch0,
                      chunk, tile),
            _swa_task(w.sinks, q2, z2, kv2, past, chunk, tile)]


def _prompt_path(w, x, p, s0, w_out, w_ple_gate):
    n_batch, length, _ = x.shape
    assert n_batch == 2, "a call holding all three stages would not fit the VMEM budget"
    proj, attn = {}, {}
    y = None
    for s in range(n_batch + 2):
        has_in, has_attn, has_out = s < n_batch, 1 <= s <= n_batch, s >= 2
        tile = TILE_FUSED if has_attn else TILE_MATMUL_ONLY
        tasks, names = [], []
        if has_attn:
            tasks += _attn_tasks(w, proj[s - 1], s0, s - 1, None, CHUNK, tile)
            names.append("attn")
        if has_in:
            tasks.append(_inproj_task(x, s, 1, tile, w.g_mix, w.w_in))
            names.append("inproj")
        if s == 0:
            tasks.append(_cast_task((w_out, w_ple_gate), length // tile))
        if has_out:
            o1, o2, _ = attn[s - 2]
            tasks.append(_outproj_task(x, o1, o2, p, s - 2, tile, w.w_out, w.g_ple,
                                       w.w_ple_gate, w.w_ple_proj, w.g_final, y_prev=y))
            names.append("outproj")
        res = _run(tasks, (1, length // tile), "_".join(names))
        if has_attn:
            (o1, s_new), (o2,) = res[0], res[1]
            attn[s - 1] = (o1, o2, s_new)
            res = res[2:]
        if has_in:
            proj[s] = res[0]
            res = res[1:]
        if s == 0:
            w = w._replace(w_out=_col_blocks(res[0][0]), w_ple_gate=_col_blocks(res[0][1]))
            res = res[1:]
        if has_out:
            (y,) = res[0]
    states = jnp.concatenate([attn[b][2] for b in range(n_batch)], axis=0)
    kv = [proj[b][5] for b in range(n_batch)]
    return y, states, kv, w


def _sample_path(w, x, p, s0, past):
    n_batch, length, d = x.shape
    flat = lambda a: a.reshape(1, n_batch * length, a.shape[-1])
    per_batch = lambda a: a.reshape(n_batch, length, a.shape[-1])
    (proj,) = _run([_inproj_task(flat(x), 0, 1, n_batch * length, w.g_mix, w.w_in)], (1, 1),
                   "inproj_sample")
    proj = tuple(per_batch(a) for a in proj)
    (o1, s_new), (o2,) = _run(_attn_tasks(w, proj, s0, 0, past, min(CHUNK, length), length),
                              (n_batch, 1), "attn_sample")
    ((y,),) = _run([_outproj_task(flat(x), flat(o1), flat(o2), flat(p), 0, n_batch * length,
                                  w.w_out, w.g_ple, w.w_ple_gate, w.w_ple_proj, w.g_final)],
                   (1, 1), "outproj_sample")
    return y.reshape(n_batch, length, d), s_new, proj[5]


def kernel(x_prompt, x_sample, p_prompt, p_sample, state_gla, cache_swa_k, cache_swa_v, g_mix, w_in, w_gate_up, b_gate, g_gla_out, swa_sinks, w_out, g_ple, w_ple_gate, w_ple_proj, g_final):
    depth = w_in.shape[0]
    assert depth == 1, "the final RMSNorm is fused into the last layer's output kernel"
    bp, lp, _ = x_prompt.shape
    bs, ls, _ = x_sample.shape
    assert cache_swa_k.shape[2] == WINDOW
    i = 0
    row = lambda a: a.reshape(1, -1).astype(F32)
    w = _Weights(
        g_mix=row(g_mix[i]), w_in=_split_w_in(w_in[i]),
        w_gate=jnp.pad(w_gate_up[i].astype(BF16), ((0, LANES - GLA_RANK), (0, 0))),
        b_gate=row(b_gate[i]), g_gla_out=row(g_gla_out[i]),
        sinks=swa_sinks[i].reshape(-1).astype(F32),
        w_out=None, g_ple=row(g_ple[i]), w_ple_gate=None,
        w_ple_proj=_col_blocks(w_ple_proj[i].astype(BF16)), g_final=row(g_final))

    s0_p = jnp.zeros((bp, GLA_HEADS, GLA_DK, GLA_DV), F32)
    y_p, s_p, kv_p, w = _prompt_path(w, x_prompt, p_prompt[i], s0_p, w_out[i], w_ple_gate[i])
    keep = min(WINDOW, lp)
    last = lambda col: jnp.concatenate(
        [a[:, lp - keep:, col * SWA_KVW:(col + 1) * SWA_KVW] for a in kv_p], axis=0
    ).reshape(bp, keep, SWA_KV_HEADS, SWA_HEAD_DIM)
    k_p, v_p = last(0), last(1)

    past = (cache_swa_k[i].reshape(bs, WINDOW, SWA_KVW), cache_swa_v[i].reshape(bs, WINDOW, SWA_KVW))
    y_s, s_s, kv_s = _sample_path(w, x_sample, p_sample[i], state_gla[i], past)
    k_s = kv_s[:, :, :SWA_KVW].reshape(bs, ls, SWA_KV_HEADS, SWA_HEAD_DIM)
    v_s = kv_s[:, :, SWA_KVW:].reshape(bs, ls, SWA_KV_HEADS, SWA_HEAD_DIM)
    return (y_p, y_s, s_p[None], k_p[None], v_p[None], s_s[None], k_s[None], v_s[None])
```

```python
import functools
from typing import Any, Callable, NamedTuple

import jax
import jax.numpy as jnp
from jax import lax
from jax.experimental import pallas as pl
from jax.experimental.pallas import tpu as pltpu

F32 = jnp.float32
BF16 = jnp.bfloat16

CHUNK = 64
RMS_EPS = 1e-6
GLA_HEADS = 4
GLA_DK = 128
GLA_DV = 256
GLA_KW = GLA_HEADS * GLA_DK
GLA_WIDTH = GLA_HEADS * GLA_DV
GLA_RANK = 16
GLA_TAU = 16.0
SWA_KV_HEADS = 4
SWA_GROUP = 4
SWA_HEAD_DIM = 64
SWA_WIDTH = SWA_KV_HEADS * SWA_GROUP * SWA_HEAD_DIM
SWA_KVW = SWA_KV_HEADS * SWA_HEAD_DIM
WINDOW = 128

LANES = 128
MXU_WIDTH = 256
WEIGHT_BLOCK = 512
VMEM_LIMIT_BYTES = 56 * 1024 * 1024
TILE_MATMUL_ONLY = 512
TILE_FUSED = 256
OUT_ROW_GROUP = 256
GLA_IN_FLIGHT = 3
SWA_IN_FLIGHT = 4
NORM_HEAD_START = 1

LOG2E = 1.4426950408889634
LN2 = 0.6931471805599453


def _dot(a, b):
    return jnp.dot(a, b, preferred_element_type=F32)


def _rms_scale(x):
    return lax.rsqrt(jnp.mean(x * x, axis=-1, keepdims=True) + RMS_EPS)


def _sigmoid(x):
    return 1.0 / (1.0 + jnp.exp2(x * -LOG2E))


def _silu(x):
    return x * _sigmoid(x)


def _resident(shape):
    zeros = (0,) * len(shape)
    return pl.BlockSpec(shape, lambda *_: zeros, pipeline_mode=pl.Buffered(1))


def _tile(tm, width, batch0=0):
    return pl.BlockSpec((None, tm, width), lambda b, i: (batch0 + b, i, 0))


class _Block(NamedTuple):
    array: Any
    shape: tuple
    index: tuple

    def spec(self):
        index = self.index
        return pl.BlockSpec(self.shape, lambda *_: index, pipeline_mode=pl.Buffered(1))


def _col_blocks(w, col0=0, width=None):
    width = width or w.shape[1] - col0
    blk = min(WEIGHT_BLOCK, width)
    assert col0 % blk == 0 and width % blk == 0
    return tuple(_Block(w, (w.shape[0], blk), (0, (col0 + c) // blk))
                 for c in range(0, width, blk))


def _block_pieces(block_refs):
    out, base = [], 0
    for ref in block_refs:
        width = ref.shape[1]
        for c in range(0, width, MXU_WIDTH):
            hi = min(c + MXU_WIDTH, width)
            out.append((slice(base + c, base + hi), ref, slice(c, hi)))
        base += width
    return out


class _Task(NamedTuple):
    stages: Callable[..., Any]
    operands: tuple
    in_specs: tuple
    out_specs: tuple
    out_shapes: tuple
    scratch: tuple = ()
    aliases: Any = None


_DONE = object()


class _Stream:
    def __init__(self, units, max_active):
        self.pending, self.active, self.max_active = list(units), [], max_active

    def busy(self):
        return bool(self.pending or self.active)

    def step(self):
        self.active = [g for g in self.active if next(g, _DONE) is not _DONE]
        if self.pending and len(self.active) < self.max_active:
            g = self.pending.pop(0)
            if next(g, _DONE) is not _DONE:
                self.active.append(g)


def _drive(heavy, light):
    heavy = [g for g in heavy if g is not None]
    while heavy or any(s.busy() for s in light):
        for s in light:
            s.step()
        heavy = [g for g in heavy if next(g, _DONE) is not _DONE]


def _run(tasks, grid, name):
    n_in = [len(t.operands) for t in tasks]
    n_out = [len(t.out_shapes) for t in tasks]
    n_scr = [len(t.scratch) for t in tasks]

    def body(*refs):
        ins = refs[:sum(n_in)]
        outs = refs[sum(n_in):sum(n_in) + sum(n_out)]
        scrs = refs[sum(n_in) + sum(n_out):]
        heavy, light = [], []
        i = o = s = 0
        for t, ni, no, ns in zip(tasks, n_in, n_out, n_scr):
            h, l = t.stages(*ins[i:i + ni], *outs[o:o + no], *scrs[s:s + ns])
            heavy.append(h)
            light += l
            i, o, s = i + ni, o + no, s + ns
        _drive(heavy, light)

    aliases = {}
    i = o = 0
    for t, ni, no in zip(tasks, n_in, n_out):
        for src, dst in (t.aliases or {}).items():
            aliases[i + src] = o + dst
        i, o = i + ni, o + no

    flat = pl.pallas_call(
        body,
        grid=grid,
        in_specs=[s for t in tasks for s in t.in_specs],
        out_specs=tuple(s for t in tasks for s in t.out_specs),
        out_shape=tuple(s for t in tasks for s in t.out_shapes),
        scratch_shapes=[s for t in tasks for s in t.scratch],
        input_output_aliases=aliases,
        compiler_params=pltpu.CompilerParams(
            dimension_semantics=("arbitrary", "arbitrary"), vmem_limit_bytes=VMEM_LIMIT_BYTES),
        name=name,
    )(*[a for t in tasks for a in t.operands])
    results, o = [], 0
    for no in n_out:
        results.append(tuple(flat[o:o + no]))
        o += no
    return results


def _inproj_stages(x_ref, g_ref, *refs, blocks_per_group):
    n_blocks = sum(blocks_per_group)
    w_refs, o_refs = refs[:n_blocks], refs[n_blocks:]

    def pieces():
        x = x_ref[...]
        n = ((x * _rms_scale(x)) * g_ref[...]).astype(BF16)
        for _ in range(NORM_HEAD_START):
            yield
        first = 0
        for n_blk, o_ref, is_gate in zip(blocks_per_group, o_refs, IN_IS_GATE):
            for cols, w_ref, cs in _block_pieces(w_refs[first:first + n_blk]):
                u = _dot(n, w_ref[:, cs])
                o_ref[:, cols] = (_silu(u) if is_gate else u).astype(o_ref.dtype)
                yield
            first += n_blk

    return pieces(), []


IN_DTYPES = (BF16, BF16, BF16, BF16, BF16, F32, BF16, BF16)
IN_IS_GATE = (False, False, False, True, False, False, True, False)


def _inproj_task(x, batch0, n_batch, tm, g_mix, groups):
    _, length, d = x.shape
    blocks = [w for g in groups for w in g]
    widths = [sum(w.shape[1] for w in g) for g in groups]
    return _Task(
        stages=functools.partial(_inproj_stages, blocks_per_group=tuple(len(g) for g in groups)),
        operands=(x, g_mix, *[w.array for w in blocks]),
        in_specs=(_tile(tm, d, batch0), _resident(g_mix.shape), *[w.spec() for w in blocks]),
        out_specs=tuple(_tile(tm, n) for n in widths),
        out_shapes=tuple(jax.ShapeDtypeStruct((n_batch, length, n), dt)
                         for n, dt in zip(widths, IN_DTYPES)))


def _gla_stages(q_ref, k_ref, v_ref, z_ref, lr_ref, wg_ref, bg_ref, gout_ref, s0_ref,
                o_ref, s_ref, *, chunk, n_chunks):
    @pl.when(pl.program_id(1) == 0)
    def _():
        s_ref[...] = s0_ref[...]

    row = lax.broadcasted_iota(jnp.int32, (chunk, chunk), 0)
    col = lax.broadcasted_iota(jnp.int32, (chunk, chunk), 1)
    causal = row >= col
    row2 = lax.broadcasted_iota(jnp.int32, (chunk, 2 * chunk), 0)
    col2 = lax.broadcasted_iota(jnp.int32, (chunk, 2 * chunk), 1)
    tril2 = jnp.where(row2 >= lax.rem(col2, chunk), 1.0, 0.0).astype(BF16)
    eye = (lax.broadcasted_iota(jnp.int32, (GLA_DK, GLA_DK), 0)
           == lax.broadcasted_iota(jnp.int32, (GLA_DK, GLA_DK), 1))
    scale = GLA_DK ** -0.5
    heads = [(slice(h * GLA_DK, (h + 1) * GLA_DK), slice(h * GLA_DV, (h + 1) * GLA_DV))
             for h in range(GLA_HEADS)]

    def chunk_unit(c):
        r = slice(c * chunk, (c + 1) * chunk)
        gpre = _dot(lr_ref[r, :], wg_ref[...])
        yield
        gpre = gpre + bg_ref[...]
        soft = jnp.log2(1.0 + jnp.exp2(jnp.abs(gpre) * -LOG2E))
        g = jnp.minimum(gpre, 0.0) * (1.0 / GLA_TAU) - soft * (LN2 / GLA_TAU)
        g_hi = g.astype(BF16)
        g_lo = (g - g_hi.astype(F32)).astype(BF16)
        g2 = jnp.concatenate([g_hi, g_lo], axis=0)
        yield
        b = _dot(tril2, g2)
        yield
        b_last = b[chunk - 1:chunk, :]
        q = q_ref[r, :].astype(F32)
        k = k_ref[r, :].astype(F32)
        qe = ((q * scale) * jnp.exp(b)).astype(BF16)
        ke = k * jnp.exp(-b)
        kd = k * jnp.exp(b_last - b)
        decay = jnp.exp(b_last)
        v = [v_ref[r, vs] for _, vs in heads]
        ke_t = [ke[:, ks].T.astype(BF16) for ks, _ in heads]
        kd_t = [kd[:, ks].T.astype(BF16) for ks, _ in heads]
        yield
        a_raw = [_dot(qe[:, ks], ke_t[h]) for h, (ks, _) in enumerate(heads)]
        kv = [_dot(kd_t[h], v[h]) for h in range(GLA_HEADS)]
        yield
        lhs, rhs = [], []
        for h, (ks, _) in enumerate(heads):
            a = jnp.where(causal, a_raw[h], 0.0).astype(BF16)
            s = s_ref[h]
            lhs.append(jnp.concatenate([qe[:, ks], a], axis=1))
            rhs.append(jnp.concatenate([s.astype(BF16), v[h]], axis=0))
            dcol = jnp.sum(jnp.where(eye, decay[:, ks], 0.0), axis=1, keepdims=True)
            s_ref[h] = dcol * s + kv[h]
        yield
        o = [_dot(lhs[h], rhs[h]) for h in range(GLA_HEADS)]
        yield
        for h, (_, vs) in enumerate(heads):
            oh = (o[h] * _rms_scale(o[h])) * gout_ref[...]
            o_ref[r, vs] = (oh * z_ref[r, vs].astype(F32)).astype(o_ref.dtype)

    return None, [_Stream([chunk_unit(c) for c in range(n_chunks)], GLA_IN_FLIGHT)]


def _gla_task(q, k, v, z, lr, wg, bg, gout, s0, state_batch0, chunk, tb):
    n_batch, length, _ = q.shape
    st_in = pl.BlockSpec((None, GLA_HEADS, GLA_DK, GLA_DV),
                         lambda b, i: (state_batch0 + b, 0, 0, 0))
    st_out = pl.BlockSpec((None, GLA_HEADS, GLA_DK, GLA_DV), lambda b, i: (b, 0, 0, 0))
    return _Task(
        stages=functools.partial(_gla_stages, chunk=chunk, n_chunks=tb // chunk),
        operands=(q, k, v, z, lr, wg, bg, gout, s0),
        in_specs=(_tile(tb, GLA_KW), _tile(tb, GLA_KW), _tile(tb, GLA_WIDTH), _tile(tb, GLA_WIDTH),
                  _tile(tb, LANES), _resident(wg.shape), _resident(bg.shape),
                  _resident(gout.shape), st_in),
        out_specs=(_tile(tb, GLA_WIDTH), st_out),
        out_shapes=(jax.ShapeDtypeStruct((n_batch, length, GLA_WIDTH), BF16),
                    jax.ShapeDtypeStruct((n_batch, GLA_HEADS, GLA_DK, GLA_DV), F32)))


def _swa_stages(sink_ref, q_ref, kp_ref, vp_ref, kc_ref, vc_ref, z_ref, o_ref,
                *, chunk, n_chunks, mask_missing_past):
    kvw = SWA_GROUP * SWA_HEAD_DIM
    n_keys = WINDOW + chunk
    gq = SWA_GROUP * chunk
    gq_group = lax.broadcasted_iota(jnp.int32, (1, gq), 1) // chunk
    key_i = lax.broadcasted_iota(jnp.int32, (n_keys, gq), 0)
    qscale = SWA_HEAD_DIM ** -0.5
    first_pos = pl.program_id(1) * (chunk * n_chunks) - WINDOW
    shared = {}

    def head_operands(h):
        if not shared:
            shared["k"] = jnp.concatenate([kp_ref[...], kc_ref[...]], axis=0).astype(BF16)
            v = jnp.concatenate([vp_ref[...], vc_ref[...]], axis=0)
            shared["vt"] = v.T.astype(BF16)
        if h not in shared:
            hd = slice(h * SWA_HEAD_DIM, (h + 1) * SWA_HEAD_DIM)
            k_h = shared["k"][:, hd]
            vt = shared["vt"][hd, :]
            vt_at = {}
            for c in range(n_chunks):
                shift = (c * chunk) % LANES
                if shift not in vt_at:
                    vt_at[shift] = vt[:, shift:]
            sink = jnp.zeros((1, gq), F32)
            for g in range(SWA_GROUP):
                sink = jnp.where(gq_group == g, sink_ref[h * SWA_GROUP + g], sink)
            shared[h] = (k_h, vt_at, sink)
        return shared[h]

    def unit(h, c):
        k_h, vt_at, sink = head_operands(h)
        rows = slice(c * chunk, (c + 1) * chunk)
        keys = slice(c * chunk, c * chunk + n_keys)
        cols = slice(h * kvw, (h + 1) * kvw)
        qt = q_ref[rows, cols].astype(F32).T
        qg = jnp.concatenate(
            [qt[g * SWA_HEAD_DIM:(g + 1) * SWA_HEAD_DIM] for g in range(SWA_GROUP)],
            axis=1).astype(BF16)
        shift = (c * chunk) % LANES
        vt_keys = vt_at[shift][:, c * chunk - shift:c * chunk - shift + n_keys]
        yield
        s = _dot(k_h[keys], qg)
        yield
        if mask_missing_past and c * chunk < WINDOW:
            s = jnp.where(key_i + (first_pos + c * chunk) >= 0, s, jnp.finfo(F32).min)
        m = jnp.maximum(jnp.max(s, axis=0, keepdims=True) * qscale, sink) * LOG2E
        p = jnp.exp2(s * (qscale * LOG2E) - m)
        denom = jnp.sum(p, axis=0, keepdims=True) + jnp.exp2(sink * LOG2E - m)
        p = p.astype(BF16)
        yield
        o_t = _dot(vt_keys, p)
        yield
        o_t = (o_t / denom).T
        o = jnp.concatenate([o_t[g * chunk:(g + 1) * chunk] for g in range(SWA_GROUP)],
                            axis=1)
        o_ref[rows, cols] = (o * z_ref[rows, cols].astype(F32)).astype(o_ref.dtype)

    units = [unit(h, c) for h in range(SWA_KV_HEADS) for c in range(n_chunks)]
    return None, [_Stream(units, SWA_IN_FLIGHT)]


def _swa_task(sinks, q, z, kv, past, chunk, qb):
    n_batch, length, _ = q.shape
    if past is None:
        per_win = qb // WINDOW
        prev = lambda col: pl.BlockSpec(
            (None, WINDOW, SWA_KVW), lambda b, i: (b, jnp.maximum(i * per_win - 1, 0), col))
        k_past, v_past, kp_spec, vp_spec = kv, kv, prev(0), prev(1)
    else:
        k_past, v_past = past
        kp_spec = vp_spec = pl.BlockSpec((None, WINDOW, SWA_KVW), lambda b, i: (b, 0, 0))
    cur = lambda col: pl.BlockSpec((None, qb, SWA_KVW), lambda b, i: (b, i, col))
    return _Task(
        stages=functools.partial(_swa_stages, chunk=chunk, n_chunks=qb // chunk,
                                 mask_missing_past=past is None),
        operands=(sinks, q, k_past, v_past, kv, kv, z),
        in_specs=(pl.BlockSpec(memory_space=pltpu.SMEM), _tile(qb, SWA_WIDTH), kp_spec, vp_spec,
                  cur(0), cur(1), _tile(qb, SWA_WIDTH)),
        out_specs=(_tile(qb, SWA_WIDTH),),
        out_shapes=(jax.ShapeDtypeStruct((n_batch, length, SWA_WIDTH), BF16),))


def _outproj_stages(x_ref, o1_ref, o2_ref, p_ref, gple_ref, gfin_ref, *refs, n_blocks,
                    own_batch, has_guest):
    wo, wg, wp = (refs[j * n_blocks:(j + 1) * n_blocks] for j in range(3))
    guest = refs[3 * n_blocks:3 * n_blocks + 4] if has_guest else None
    y_ref = refs[-2] if has_guest else refs[-1]
    if own_batch is not None:
        for b in range(y_ref.shape[0]):
            if b != own_batch:
                y_ref[b] = jnp.zeros(y_ref.shape[1:], y_ref.dtype)
        y_ref = y_ref.at[own_batch]

    def pieces(x_ref, o1_ref, o2_ref, p_ref, y_ref):
        h_ref = y_ref
        o = jnp.concatenate([o1_ref[...], o2_ref[...]], axis=1)
        n = []
        for cols, w_ref, cs in _block_pieces(wo):
            h = x_ref[:, cols] + _dot(o, w_ref[:, cs])
            h_ref[:, cols] = h
            n.append((h * gple_ref[:, cols]).astype(BF16))
            yield
        n = jnp.concatenate(n, axis=1)
        r = _rms_scale(h_ref[...]) * -LOG2E
        p = p_ref[...].astype(BF16)
        tm = h_ref.shape[0]
        for rows in [slice(r0, r0 + OUT_ROW_GROUP) for r0 in range(0, tm, OUT_ROW_GROUP)]:
            for (cols, wg_ref, cs), (_, wp_ref, _) in zip(_block_pieces(wg), _block_pieces(wp)):
                gate = 1.0 / (1.0 + jnp.exp2(_dot(n[rows], wg_ref[:, cs]) * r[rows]))
                h_ref[rows, cols] = h_ref[rows, cols] + gate * _dot(p[rows], wp_ref[:, cs])
                yield
            h = h_ref[rows, :]
            y_ref[rows, :] = (h * _rms_scale(h)) * gfin_ref[...]

    def all_pieces():
        yield from pieces(x_ref, o1_ref, o2_ref, p_ref, y_ref)
        if has_guest:
            @pl.when(pl.program_id(1) == pl.num_programs(1) - 1)
            def _():
                for _ in pieces(*guest, refs[-1]):
                    pass

    return all_pieces(), []


def _outproj_task(x, o1, o2, p, batch0, tm, wo, gple, wg, wp, gfin, y_prev=None, guest=None):
    _, _, d = x.shape
    assert len(wo) == len(wg) == len(wp)
    blocks = (*wo, *wg, *wp)
    operands = (x, o1, o2, p, gple, gfin, *[w.array for w in blocks])
    in_specs = (_tile(tm, d, batch0), _tile(tm, o1.shape[2]), _tile(tm, o2.shape[2]),
                _tile(tm, p.shape[2], batch0), _resident(gple.shape), _resident(gfin.shape),
                *[w.spec() for w in blocks])
    out_specs, out_shapes = [], []
    if guest is not None:
        whole = lambda a: pl.BlockSpec((None,) + a.shape[1:], lambda b, i: (0, 0, 0))
        operands += tuple(guest)
        in_specs += tuple(whole(a) for a in guest)
        out_specs.append(whole(guest[0]))
        out_shapes.append(jax.ShapeDtypeStruct(guest[0].shape, F32))
    aliases, own_batch, y_spec = None, None, _tile(tm, d, batch0)
    if y_prev is not None:
        aliases = {len(operands): 0}
        operands += (y_prev,)
        in_specs += (pl.BlockSpec(memory_space=pl.ANY),)
    elif x.shape[0] > 1:
        own_batch = batch0
        y_spec = pl.BlockSpec((x.shape[0], tm, d), lambda b, i: (0, i, 0))
    return _Task(stages=functools.partial(_outproj_stages, n_blocks=len(wo), own_batch=own_batch,
                                          has_guest=guest is not None),
                 operands=operands, in_specs=in_specs,
                 out_specs=(y_spec, *out_specs),
                 out_shapes=(jax.ShapeDtypeStruct(x.shape, F32), *out_shapes),
                 aliases=aliases)


def _split_w_in(w_in):
    lr0 = 2 * GLA_KW + 2 * GLA_WIDTH
    n_rows, n_in = w_in.shape
    n_main = n_in - GLA_RANK
    assert lr0 % WEIGHT_BLOCK == 0 and n_main % WEIGHT_BLOCK == 0

    def relayout(w_ref, o_ref):
        o_ref[...] = w_ref[...].T.astype(BF16)

    assert WEIGHT_BLOCK % GLA_RANK == 0
    src_col = lambda j: GLA_RANK * (j * (WEIGHT_BLOCK // GLA_RANK)
                                    + (j >= lr0 // WEIGHT_BLOCK).astype(jnp.int32))
    w = pl.pallas_call(
        relayout,
        grid=(n_main // WEIGHT_BLOCK,),
        in_specs=[pl.BlockSpec((pl.Element(WEIGHT_BLOCK), pl.Element(n_rows)),
                               lambda j: (src_col(j), 0))],
        out_specs=pl.BlockSpec((n_rows, WEIGHT_BLOCK), lambda j: (0, j)),
        out_shape=jax.ShapeDtypeStruct((n_rows, n_main), BF16),
        compiler_params=pltpu.CompilerParams(
            dimension_semantics=("arbitrary",), vmem_limit_bytes=VMEM_LIMIT_BYTES),
        name="w_in_relayout",
    )(w_in.T)

    def relayout_gate(w_ref, o_ref):
        pad = jnp.zeros((LANES - GLA_RANK, n_rows), F32)
        o_ref[...] = jnp.concatenate([w_ref[...], pad], axis=0).T.astype(BF16)

    w_lr = pl.pallas_call(
        relayout_gate,
        grid=(1,),
        in_specs=[pl.BlockSpec((pl.Element(GLA_RANK), pl.Element(n_rows)), lambda j: (lr0, 0))],
        out_specs=pl.BlockSpec((n_rows, LANES), lambda j: (0, 0)),
        out_shape=jax.ShapeDtypeStruct((n_rows, LANES), BF16),
        name="w_gate_relayout",
    )(w_in.T)
    widths = (GLA_KW, GLA_KW, GLA_WIDTH, GLA_WIDTH, SWA_WIDTH, 2 * SWA_KVW, SWA_WIDTH)
    groups, col0 = [], 0
    for width in widths:
        groups.append(_col_blocks(w, col0, width))
        col0 += width
    return (*groups, _col_blocks(w_lr))


class _Weights(NamedTuple):
    g_mix: Any
    w_in: tuple
    w_gate: Any
    b_gate: Any
    g_gla_out: Any
    sinks: Any
    w_out: Any
    g_ple: Any
    w_ple_gate: Any
    w_ple_proj: Any
    g_final: Any


def _cast_stages(*refs):
    n = len(refs) // 2

    def unit():
        for src_ref, dst_ref in zip(refs[:n], refs[n:]):
            dst_ref[...] = src_ref[...].astype(dst_ref.dtype)
        return
        yield

    return None, [_Stream([unit()], 1)]


def _cast_task(arrays, n_steps):
    slab = lambda a: pl.BlockSpec((a.shape[0] // n_steps, a.shape[1]), lambda b, i: (i, 0))
    return _Task(stages=_cast_stages, operands=tuple(arrays),
                 in_specs=tuple(slab(a) for a in arrays), out_specs=tuple(slab(a) for a in arrays),
                 out_shapes=tuple(jax.ShapeDtypeStruct(a.shape, BF16) for a in arrays))


def _attn_tasks(w, proj, s0, state_batch0, past, chunk, tile):
    q1, k1, v1, z1, q2, kv2, z2, lr = proj
    return [_gla_task(q1, k1, v1, z1, lr, w.w_gate, w.b_gate, w.g_gla_out, s0, state_batch0,
                      chunk, tile),
            _swa_task(w.sinks, q2, z2, kv2, past, chunk, tile)]


def _prompt_path(w, x, p, s0, w_out, w_ple_gate, guest):
    n_batch, length, _ = x.shape
    assert n_batch == 2, "a call holding all three stages would not fit the VMEM budget"
    proj, attn = {}, {}
    y = None
    for s in range(n_batch + 2):
        has_in, has_attn, has_out = s < n_batch, 1 <= s <= n_batch, s >= 2
        tile = TILE_FUSED if has_attn else TILE_MATMUL_ONLY
        tasks, names = [], []
        if has_attn:
            tasks += _attn_tasks(w, proj[s - 1], s0, s - 1, None, CHUNK, tile)
            names.append("attn")
        if has_in:
            tasks.append(_inproj_task(x, s, 1, tile, w.g_mix, w.w_in))
            names.append("inproj")
        if s == 0:
            tasks.append(_cast_task((w_out, w_ple_gate), length // tile))
        if has_out:
            o1, o2, _ = attn[s - 2]
            tasks.append(_outproj_task(x, o1, o2, p, s - 2, tile, w.w_out, w.g_ple,
                                       w.w_ple_gate, w.w_ple_proj, w.g_final, y_prev=y,
                                       guest=guest(w) if s == n_batch + 1 else None))
            names.append("outproj")
        res = _run(tasks, (1, length // tile), "_".join(names))
        if has_attn:
            (o1, s_new), (o2,) = res[0], res[1]
            attn[s - 1] = (o1, o2, s_new)
            res = res[2:]
        if has_in:
            proj[s] = res[0]
            res = res[1:]
        if s == 0:
            w = w._replace(w_out=_col_blocks(res[0][0]), w_ple_gate=_col_blocks(res[0][1]))
            res = res[1:]
        if has_out:
            y, y_guest = (res[0] + (None,))[:2]
    states = jnp.concatenate([attn[b][2] for b in range(n_batch)], axis=0)
    kv = [proj[b][5] for b in range(n_batch)]
    return y, states, kv, y_guest


def _sample_front(w, x, p, s0, past):
    n_batch, length, _ = x.shape
    flat = lambda a: a.reshape(1, n_batch * length, a.shape[-1])
    per_batch = lambda a: a.reshape(n_batch, length, a.shape[-1])
    (proj,) = _run([_inproj_task(flat(x), 0, 1, n_batch * length, w.g_mix, w.w_in)], (1, 1),
                   "inproj_sample")
    proj = tuple(per_batch(a) for a in proj)
    (o1, s_new), (o2,) = _run(_attn_tasks(w, proj, s0, 0, past, min(CHUNK, length), length),
                              (n_batch, 1), "attn_sample")
    return (flat(x), flat(o1), flat(o2), flat(p)), s_new, proj[5]


def kernel(x_prompt, x_sample, p_prompt, p_sample, state_gla, cache_swa_k, cache_swa_v, g_mix, w_in, w_gate_up, b_gate, g_gla_out, swa_sinks, w_out, g_ple, w_ple_gate, w_ple_proj, g_final):
    depth = w_in.shape[0]
    assert depth == 1, "the final RMSNorm is fused into the last layer's output kernel"
    bp, lp, _ = x_prompt.shape
    bs, ls, _ = x_sample.shape
    assert cache_swa_k.shape[2] == WINDOW
    i = 0
    row = lambda a: a.reshape(1, -1).astype(F32)
    w = _Weights(
        g_mix=row(g_mix[i]), w_in=_split_w_in(w_in[i]),
        w_gate=jnp.pad(w_gate_up[i].astype(BF16), ((0, LANES - GLA_RANK), (0, 0))),
        b_gate=row(b_gate[i]), g_gla_out=row(g_gla_out[i]),
        sinks=swa_sinks[i].reshape(-1).astype(F32),
        w_out=None, g_ple=row(g_ple[i]), w_ple_gate=None,
        w_ple_proj=_col_blocks(w_ple_proj[i].astype(BF16)), g_final=row(g_final))

    past = (cache_swa_k[i].reshape(bs, WINDOW, SWA_KVW), cache_swa_v[i].reshape(bs, WINDOW, SWA_KVW))
    sample = {}

    def sample_guest(w_now):
        tile, sample["state"], sample["kv"] = _sample_front(
            w_now, x_sample, p_sample[i], state_gla[i], past)
        return tile

    s0_p = jnp.zeros((bp, GLA_HEADS, GLA_DK, GLA_DV), F32)
    y_p, s_p, kv_p, y_s = _prompt_path(w, x_prompt, p_prompt[i], s0_p, w_out[i], w_ple_gate[i],
                                       sample_guest)
    keep = min(WINDOW, lp)
    last = lambda col: jnp.concatenate(
        [a[:, lp - keep:, col * SWA_KVW:(col + 1) * SWA_KVW] for a in kv_p], axis=0
    ).reshape(bp, keep, SWA_KV_HEADS, SWA_HEAD_DIM)
    k_p, v_p = last(0), last(1)

    y_s, s_s, kv_s = y_s.reshape(x_sample.shape), sample["state"], sample["kv"]
    k_s = kv_s[:, :, :SWA_KVW].reshape(bs, ls, SWA_KV_HEADS, SWA_HEAD_DIM)
    v_s = kv_s[:, :, SWA_KVW:].reshape(bs, ls, SWA_KV_HEADS, SWA_HEAD_DIM)
    return (y_p, y_s, s_p[None], k_p[None], v_p[None], s_s[None], k_s[None], v_s[None])
```

```python
import functools
from typing import Any, Callable, NamedTuple

import jax
import jax.numpy as jnp
from jax import lax
from jax.experimental import pallas as pl
from jax.experimental.pallas import tpu as pltpu

F32 = jnp.float32
BF16 = jnp.bfloat16

CHUNK = 64
RMS_EPS = 1e-6
GLA_HEADS = 4
GLA_DK = 128
GLA_DV = 256
GLA_KW = GLA_HEADS * GLA_DK
GLA_WIDTH = GLA_HEADS * GLA_DV
GLA_RANK = 16
GLA_TAU = 16.0
SWA_KV_HEADS = 4
SWA_GROUP = 4
SWA_HEAD_DIM = 64
SWA_WIDTH = SWA_KV_HEADS * SWA_GROUP * SWA_HEAD_DIM
SWA_KVW = SWA_KV_HEADS * SWA_HEAD_DIM
WINDOW = 128

LANES = 128
MXU_WIDTH = 256
WEIGHT_BLOCK = 512
VMEM_LIMIT_BYTES = 56 * 1024 * 1024
TILE_MATMUL_ONLY = 512
TILE_FUSED = 256
OUT_ROW_GROUP = 256
GLA_IN_FLIGHT = 3
SWA_IN_FLIGHT = 4
NORM_HEAD_START = 1

LOG2E = 1.4426950408889634
LN2 = 0.6931471805599453


def _dot(a, b):
    return jnp.dot(a, b, preferred_element_type=F32)


def _rms_scale(x):
    return lax.rsqrt(jnp.mean(x * x, axis=-1, keepdims=True) + RMS_EPS)


def _sigmoid(x):
    return 1.0 / (1.0 + jnp.exp2(x * -LOG2E))


def _silu(x):
    return x * _sigmoid(x)


def _resident(shape):
    zeros = (0,) * len(shape)
    return pl.BlockSpec(shape, lambda *_: zeros, pipeline_mode=pl.Buffered(1))


def _tile(tm, width, batch0=0):
    return pl.BlockSpec((None, tm, width), lambda b, i: (batch0 + b, i, 0))


class _Block(NamedTuple):
    array: Any
    shape: tuple
    index: tuple

    def spec(self):
        index = self.index
        return pl.BlockSpec(self.shape, lambda *_: index, pipeline_mode=pl.Buffered(1))


def _col_blocks(w, col0=0, width=None):
    width = width or w.shape[1] - col0
    blk = min(WEIGHT_BLOCK, width)
    assert col0 % blk == 0 and width % blk == 0
    return tuple(_Block(w, (w.shape[0], blk), (0, (col0 + c) // blk))
                 for c in range(0, width, blk))


def _block_pieces(block_refs):
    out, base = [], 0
    for ref in block_refs:
        width = ref.shape[1]
        for c in range(0, width, MXU_WIDTH):
            hi = min(c + MXU_WIDTH, width)
            out.append((slice(base + c, base + hi), ref, slice(c, hi)))
        base += width
    return out


class _Task(NamedTuple):
    stages: Callable[..., Any]
    operands: tuple
    in_specs: tuple
    out_specs: tuple
    out_shapes: tuple
    scratch: tuple = ()
    aliases: Any = None


_DONE = object()


class _Stream:
    def __init__(self, units, max_active):
        self.pending, self.active, self.max_active = list(units), [], max_active

    def busy(self):
        return bool(self.pending or self.active)

    def step(self):
        self.active = [g for g in self.active if next(g, _DONE) is not _DONE]
        if self.pending and len(self.active) < self.max_active:
            g = self.pending.pop(0)
            if next(g, _DONE) is not _DONE:
                self.active.append(g)


def _drive(heavy, light):
    heavy = [g for g in heavy if g is not None]
    while heavy or any(s.busy() for s in light):
        for s in light:
            s.step()
        heavy = [g for g in heavy if next(g, _DONE) is not _DONE]


def _run(tasks, grid, name):
    n_in = [len(t.operands) for t in tasks]
    n_out = [len(t.out_shapes) for t in tasks]
    n_scr = [len(t.scratch) for t in tasks]

    def body(*refs):
        ins = refs[:sum(n_in)]
        outs = refs[sum(n_in):sum(n_in) + sum(n_out)]
        scrs = refs[sum(n_in) + sum(n_out):]
        heavy, light = [], []
        i = o = s = 0
        for t, ni, no, ns in zip(tasks, n_in, n_out, n_scr):
            h, l = t.stages(*ins[i:i + ni], *outs[o:o + no], *scrs[s:s + ns])
            heavy.append(h)
            light += l
            i, o, s = i + ni, o + no, s + ns
        _drive(heavy, light)

    aliases = {}
    i = o = 0
    for t, ni, no in zip(tasks, n_in, n_out):
        for src, dst in (t.aliases or {}).items():
            aliases[i + src] = o + dst
        i, o = i + ni, o + no

    flat = pl.pallas_call(
        body,
        grid=grid,
        in_specs=[s for t in tasks for s in t.in_specs],
        out_specs=tuple(s for t in tasks for s in t.out_specs),
        out_shape=tuple(s for t in tasks for s in t.out_shapes),
        scratch_shapes=[s for t in tasks for s in t.scratch],
        input_output_aliases=aliases,
        compiler_params=pltpu.CompilerParams(
            dimension_semantics=("arbitrary", "arbitrary"), vmem_limit_bytes=VMEM_LIMIT_BYTES),
        name=name,
    )(*[a for t in tasks for a in t.operands])
    results, o = [], 0
    for no in n_out:
        results.append(tuple(flat[o:o + no]))
        o += no
    return results


def _inproj_stages(x_ref, g_ref, *refs, blocks_per_group, has_guest):
    n_blocks, n_out = sum(blocks_per_group), len(blocks_per_group)
    w_refs, refs = refs[:n_blocks], refs[n_blocks:]
    guest_x = refs[0] if has_guest else None
    main_outs = refs[1:1 + n_out] if has_guest else refs

    def pieces(x_ref, o_refs):
        x = x_ref[...]
        n = ((x * _rms_scale(x)) * g_ref[...]).astype(BF16)
        for _ in range(NORM_HEAD_START):
            yield
        first = 0
        for n_blk, o_ref, is_gate in zip(blocks_per_group, o_refs, IN_IS_GATE):
            for cols, w_ref, cs in _block_pieces(w_refs[first:first + n_blk]):
                u = _dot(n, w_ref[:, cs])
                o_ref[:, cols] = (_silu(u) if is_gate else u).astype(o_ref.dtype)
                yield
            first += n_blk

    def all_pieces():
        yield from pieces(x_ref, main_outs)
        if has_guest:
            @pl.when(pl.program_id(1) == pl.num_programs(1) - 1)
            def _():
                for _ in pieces(guest_x, refs[1 + n_out:]):
                    pass

    return all_pieces(), []


IN_DTYPES = (BF16, BF16, BF16, BF16, BF16, F32, BF16, BF16)
IN_IS_GATE = (False, False, False, True, False, False, True, False)


def _inproj_task(x, batch0, n_batch, tm, g_mix, groups, guest_x=None):
    _, length, d = x.shape
    blocks = [w for g in groups for w in g]
    widths = [sum(w.shape[1] for w in g) for g in groups]
    operands = (x, g_mix, *[w.array for w in blocks])
    in_specs = (_tile(tm, d, batch0), _resident(g_mix.shape), *[w.spec() for w in blocks])
    out_specs = [_tile(tm, n) for n in widths]
    out_shapes = [jax.ShapeDtypeStruct((n_batch, length, n), dt) for n, dt in zip(widths, IN_DTYPES)]
    if guest_x is not None:
        n_guest = guest_x.shape[1]
        whole = lambda n: pl.BlockSpec((None, n_guest, n), lambda b, i: (0, 0, 0))
        operands += (guest_x,)
        in_specs += (pl.BlockSpec((None, n_guest, d), lambda b, i: (0, 0, 0),
                                  pipeline_mode=pl.Buffered(1)),)
        out_specs += [whole(n) for n in widths]
        out_shapes += [jax.ShapeDtypeStruct((1, n_guest, n), dt) for n, dt in zip(widths, IN_DTYPES)]
    return _Task(
        stages=functools.partial(_inproj_stages, blocks_per_group=tuple(len(g) for g in groups),
                                 has_guest=guest_x is not None),
        operands=operands, in_specs=in_specs, out_specs=tuple(out_specs),
        out_shapes=tuple(out_shapes))


def _gla_stages(q_ref, k_ref, v_ref, z_ref, lr_ref, wg_ref, bg_ref, gout_ref, s0_ref,
                o_ref, s_ref, *, chunk, n_chunks):
    @pl.when(pl.program_id(1) == 0)
    def _():
        s_ref[...] = s0_ref[...]

    row = lax.broadcasted_iota(jnp.int32, (chunk, chunk), 0)
    col = lax.broadcasted_iota(jnp.int32, (chunk, chunk), 1)
    causal = row >= col
    row2 = lax.broadcasted_iota(jnp.int32, (chunk, 2 * chunk), 0)
    col2 = lax.broadcasted_iota(jnp.int32, (chunk, 2 * chunk), 1)
    tril2 = jnp.where(row2 >= lax.rem(col2, chunk), 1.0, 0.0).astype(BF16)
    eye = (lax.broadcasted_iota(jnp.int32, (GLA_DK, GLA_DK), 0)
           == lax.broadcasted_iota(jnp.int32, (GLA_DK, GLA_DK), 1))
    scale = GLA_DK ** -0.5
    heads = [(slice(h * GLA_DK, (h + 1) * GLA_DK), slice(h * GLA_DV, (h + 1) * GLA_DV))
             for h in range(GLA_HEADS)]

    def chunk_unit(c):
        r = slice(c * chunk, (c + 1) * chunk)
        gpre = _dot(lr_ref[r, :], wg_ref[...])
        yield
        gpre = gpre + bg_ref[...]
        soft = jnp.log2(1.0 + jnp.exp2(jnp.abs(gpre) * -LOG2E))
        g = jnp.minimum(gpre, 0.0) * (1.0 / GLA_TAU) - soft * (LN2 / GLA_TAU)
        g_hi = g.astype(BF16)
        g_lo = (g - g_hi.astype(F32)).astype(BF16)
        g2 = jnp.concatenate([g_hi, g_lo], axis=0)
        yield
        b = _dot(tril2, g2)
        yield
        b_last = b[chunk - 1:chunk, :]
        q = q_ref[r, :].astype(F32)
        k = k_ref[r, :].astype(F32)
        qe = ((q * scale) * jnp.exp(b)).astype(BF16)
        ke = k * jnp.exp(-b)
        kd = k * jnp.exp(b_last - b)
        decay = jnp.exp(b_last)
        v = [v_ref[r, vs] for _, vs in heads]
        ke_t = [ke[:, ks].T.astype(BF16) for ks, _ in heads]
        kd_t = [kd[:, ks].T.astype(BF16) for ks, _ in heads]
        yield
        a_raw = [_dot(qe[:, ks], ke_t[h]) for h, (ks, _) in enumerate(heads)]
        kv = [_dot(kd_t[h], v[h]) for h in range(GLA_HEADS)]
        yield
        lhs, rhs = [], []
        for h, (ks, _) in enumerate(heads):
            a = jnp.where(causal, a_raw[h], 0.0).astype(BF16)
            s = s_ref[h]
            lhs.append(jnp.concatenate([qe[:, ks], a], axis=1))
            rhs.append(jnp.concatenate([s.astype(BF16), v[h]], axis=0))
            dcol = jnp.sum(jnp.where(eye, decay[:, ks], 0.0), axis=1, keepdims=True)
            s_ref[h] = dcol * s + kv[h]
        yield
        o = [_dot(lhs[h], rhs[h]) for h in range(GLA_HEADS)]
        yield
        for h, (_, vs) in enumerate(heads):
            oh = (o[h] * _rms_scale(o[h])) * gout_ref[...]
            o_ref[r, vs] = (oh * z_ref[r, vs].astype(F32)).astype(o_ref.dtype)

    return None, [_Stream([chunk_unit(c) for c in range(n_chunks)], GLA_IN_FLIGHT)]


def _gla_task(q, k, v, z, lr, wg, bg, gout, s0, state_batch0, chunk, tb):
    n_batch, length, _ = q.shape
    st_in = pl.BlockSpec((None, GLA_HEADS, GLA_DK, GLA_DV),
                         lambda b, i: (state_batch0 + b, 0, 0, 0))
    st_out = pl.BlockSpec((None, GLA_HEADS, GLA_DK, GLA_DV), lambda b, i: (b, 0, 0, 0))
    return _Task(
        stages=functools.partial(_gla_stages, chunk=chunk, n_chunks=tb // chunk),
        operands=(q, k, v, z, lr, wg, bg, gout, s0),
        in_specs=(_tile(tb, GLA_KW), _tile(tb, GLA_KW), _tile(tb, GLA_WIDTH), _tile(tb, GLA_WIDTH),
                  _tile(tb, LANES), _resident(wg.shape), _resident(bg.shape),
                  _resident(gout.shape), st_in),
        out_specs=(_tile(tb, GLA_WIDTH), st_out),
        out_shapes=(jax.ShapeDtypeStruct((n_batch, length, GLA_WIDTH), BF16),
                    jax.ShapeDtypeStruct((n_batch, GLA_HEADS, GLA_DK, GLA_DV), F32)))


def _swa_stages(sink_ref, q_ref, kp_ref, vp_ref, kc_ref, vc_ref, z_ref, o_ref,
                *, chunk, n_chunks, mask_missing_past):
    kvw = SWA_GROUP * SWA_HEAD_DIM
    n_keys = WINDOW + chunk
    gq = SWA_GROUP * chunk
    gq_group = lax.broadcasted_iota(jnp.int32, (1, gq), 1) // chunk
    key_i = lax.broadcasted_iota(jnp.int32, (n_keys, gq), 0)
    qscale = SWA_HEAD_DIM ** -0.5
    first_pos = pl.program_id(1) * (chunk * n_chunks) - WINDOW
    shared = {}

    def head_operands(h):
        if not shared:
            shared["k"] = jnp.concatenate([kp_ref[...], kc_ref[...]], axis=0).astype(BF16)
            v = jnp.concatenate([vp_ref[...], vc_ref[...]], axis=0)
            shared["vt"] = v.T.astype(BF16)
        if h not in shared:
            hd = slice(h * SWA_HEAD_DIM, (h + 1) * SWA_HEAD_DIM)
            k_h = shared["k"][:, hd]
            vt = shared["vt"][hd, :]
            vt_at = {}
            for c in range(n_chunks):
                shift = (c * chunk) % LANES
                if shift not in vt_at:
                    vt_at[shift] = vt[:, shift:]
            sink = jnp.zeros((1, gq), F32)
            for g in range(SWA_GROUP):
                sink = jnp.where(gq_group == g, sink_ref[h * SWA_GROUP + g], sink)
            shared[h] = (k_h, vt_at, sink)
        return shared[h]

    def unit(h, c):
        k_h, vt_at, sink = head_operands(h)
        rows = slice(c * chunk, (c + 1) * chunk)
        keys = slice(c * chunk, c * chunk + n_keys)
        cols = slice(h * kvw, (h + 1) * kvw)
        qt = q_ref[rows, cols].astype(F32).T
        qg = jnp.concatenate(
            [qt[g * SWA_HEAD_DIM:(g + 1) * SWA_HEAD_DIM] for g in range(SWA_GROUP)],
            axis=1).astype(BF16)
        shift = (c * chunk) % LANES
        vt_keys = vt_at[shift][:, c * chunk - shift:c * chunk - shift + n_keys]
        yield
        s = _dot(k_h[keys], qg)
        yield
        if mask_missing_past and c * chunk < WINDOW:
            s = jnp.where(key_i + (first_pos + c * chunk) >= 0, s, jnp.finfo(F32).min)
        m = jnp.maximum(jnp.max(s, axis=0, keepdims=True) * qscale, sink) * LOG2E
        p = jnp.exp2(s * (qscale * LOG2E) - m)
        denom = jnp.sum(p, axis=0, keepdims=True) + jnp.exp2(sink * LOG2E - m)
        p = p.astype(BF16)
        yield
        o_t = _dot(vt_keys, p)
        yield
        o_t = (o_t / denom).T
        o = jnp.concatenate([o_t[g * chunk:(g + 1) * chunk] for g in range(SWA_GROUP)],
                            axis=1)
        o_ref[rows, cols] = (o * z_ref[rows, cols].astype(F32)).astype(o_ref.dtype)

    units = [unit(h, c) for h in range(SWA_KV_HEADS) for c in range(n_chunks)]
    return None, [_Stream(units, SWA_IN_FLIGHT)]


def _swa_task(sinks, q, z, kv, past, chunk, qb):
    n_batch, length, _ = q.shape
    if past is None:
        per_win = qb // WINDOW
        prev = lambda col: pl.BlockSpec(
            (None, WINDOW, SWA_KVW), lambda b, i: (b, jnp.maximum(i * per_win - 1, 0), col))
        k_past, v_past, kp_spec, vp_spec = kv, kv, prev(0), prev(1)
    else:
        k_past, v_past = past
        kp_spec = vp_spec = pl.BlockSpec((None, WINDOW, SWA_KVW), lambda b, i: (b, 0, 0))
    cur = lambda col: pl.BlockSpec((None, qb, SWA_KVW), lambda b, i: (b, i, col))
    return _Task(
        stages=functools.partial(_swa_stages, chunk=chunk, n_chunks=qb // chunk,
                                 mask_missing_past=past is None),
        operands=(sinks, q, k_past, v_past, kv, kv, z),
        in_specs=(pl.BlockSpec(memory_space=pltpu.SMEM), _tile(qb, SWA_WIDTH), kp_spec, vp_spec,
                  cur(0), cur(1), _tile(qb, SWA_WIDTH)),
        out_specs=(_tile(qb, SWA_WIDTH),),
        out_shapes=(jax.ShapeDtypeStruct((n_batch, length, SWA_WIDTH), BF16),))


def _outproj_stages(x_ref, o1_ref, o2_ref, p_ref, gple_ref, gfin_ref, *refs, n_blocks,
                    own_batch, has_guest):
    wo, wg, wp = (refs[j * n_blocks:(j + 1) * n_blocks] for j in range(3))
    guest = refs[3 * n_blocks:3 * n_blocks + 4] if has_guest else None
    y_ref = refs[-2] if has_guest else refs[-1]
    if own_batch is not None:
        for b in range(y_ref.shape[0]):
            if b != own_batch:
                y_ref[b] = jnp.zeros(y_ref.shape[1:], y_ref.dtype)
        y_ref = y_ref.at[own_batch]

    def pieces(x_ref, o1_ref, o2_ref, p_ref, y_ref):
        h_ref = y_ref
        o = jnp.concatenate([o1_ref[...], o2_ref[...]], axis=1)
        n = []
        for cols, w_ref, cs in _block_pieces(wo):
            h = x_ref[:, cols] + _dot(o, w_ref[:, cs])
            h_ref[:, cols] = h
            n.append((h * gple_ref[:, cols]).astype(BF16))
            yield
        n = jnp.concatenate(n, axis=1)
        r = _rms_scale(h_ref[...]) * -LOG2E
        p = p_ref[...].astype(BF16)
        tm = h_ref.shape[0]
        for rows in [slice(r0, r0 + OUT_ROW_GROUP) for r0 in range(0, tm, OUT_ROW_GROUP)]:
            for (cols, wg_ref, cs), (_, wp_ref, _) in zip(_block_pieces(wg), _block_pieces(wp)):
                gate = 1.0 / (1.0 + jnp.exp2(_dot(n[rows], wg_ref[:, cs]) * r[rows]))
                h_ref[rows, cols] = h_ref[rows, cols] + gate * _dot(p[rows], wp_ref[:, cs])
                yield
            h = h_ref[rows, :]
            y_ref[rows, :] = (h * _rms_scale(h)) * gfin_ref[...]

    def all_pieces():
        yield from pieces(x_ref, o1_ref, o2_ref, p_ref, y_ref)
        if has_guest:
            @pl.when(pl.program_id(1) == pl.num_programs(1) - 1)
            def _():
                for _ in pieces(*guest, refs[-1]):
                    pass

    return all_pieces(), []


def _outproj_task(x, o1, o2, p, batch0, tm, wo, gple, wg, wp, gfin, y_prev=None, guest=None):
    _, _, d = x.shape
    assert len(wo) == len(wg) == len(wp)
    blocks = (*wo, *wg, *wp)
    operands = (x, o1, o2, p, gple, gfin, *[w.array for w in blocks])
    in_specs = (_tile(tm, d, batch0), _tile(tm, o1.shape[2]), _tile(tm, o2.shape[2]),
                _tile(tm, p.shape[2], batch0), _resident(gple.shape), _resident(gfin.shape),
                *[w.spec() for w in blocks])
    out_specs, out_shapes = [], []
    if guest is not None:
        whole = lambda a: pl.BlockSpec((None,) + a.shape[1:], lambda b, i: (0, 0, 0))
        operands += tuple(guest)
        in_specs += tuple(whole(a) for a in guest)
        out_specs.append(whole(guest[0]))
        out_shapes.append(jax.ShapeDtypeStruct(guest[0].shape, F32))
    aliases, own_batch, y_spec = None, None, _tile(tm, d, batch0)
    if y_prev is not None:
        aliases = {len(operands): 0}
        operands += (y_prev,)
        in_specs += (pl.BlockSpec(memory_space=pl.ANY),)
    elif x.shape[0] > 1:
        own_batch = batch0
        y_spec = pl.BlockSpec((x.shape[0], tm, d), lambda b, i: (0, i, 0))
    return _Task(stages=functools.partial(_outproj_stages, n_blocks=len(wo), own_batch=own_batch,
                                          has_guest=guest is not None),
                 operands=operands, in_specs=in_specs,
                 out_specs=(y_spec, *out_specs),
                 out_shapes=(jax.ShapeDtypeStruct(x.shape, F32), *out_shapes),
                 aliases=aliases)


def _split_w_in(w_in):
    lr0 = 2 * GLA_KW + 2 * GLA_WIDTH
    n_rows, n_in = w_in.shape
    n_main = n_in - GLA_RANK
    assert lr0 % WEIGHT_BLOCK == 0 and n_main % WEIGHT_BLOCK == 0

    def relayout(w_ref, o_ref):
        o_ref[...] = w_ref[...].T.astype(BF16)

    assert WEIGHT_BLOCK % GLA_RANK == 0
    src_col = lambda j: GLA_RANK * (j * (WEIGHT_BLOCK // GLA_RANK)
                                    + (j >= lr0 // WEIGHT_BLOCK).astype(jnp.int32))
    w = pl.pallas_call(
        relayout,
        grid=(n_main // WEIGHT_BLOCK,),
        in_specs=[pl.BlockSpec((pl.Element(WEIGHT_BLOCK), pl.Element(n_rows)),
                               lambda j: (src_col(j), 0))],
        out_specs=pl.BlockSpec((n_rows, WEIGHT_BLOCK), lambda j: (0, j)),
        out_shape=jax.ShapeDtypeStruct((n_rows, n_main), BF16),
        compiler_params=pltpu.CompilerParams(
            dimension_semantics=("arbitrary",), vmem_limit_bytes=VMEM_LIMIT_BYTES),
        name="w_in_relayout",
    )(w_in.T)

    def relayout_gate(w_ref, o_ref):
        pad = jnp.zeros((LANES - GLA_RANK, n_rows), F32)
        o_ref[...] = jnp.concatenate([w_ref[...], pad], axis=0).T.astype(BF16)

    w_lr = pl.pallas_call(
        relayout_gate,
        grid=(1,),
        in_specs=[pl.BlockSpec((pl.Element(GLA_RANK), pl.Element(n_rows)), lambda j: (lr0, 0))],
        out_specs=pl.BlockSpec((n_rows, LANES), lambda j: (0, 0)),
        out_shape=jax.ShapeDtypeStruct((n_rows, LANES), BF16),
        name="w_gate_relayout",
    )(w_in.T)
    widths = (GLA_KW, GLA_KW, GLA_WIDTH, GLA_WIDTH, SWA_WIDTH, 2 * SWA_KVW, SWA_WIDTH)
    groups, col0 = [], 0
    for width in widths:
        groups.append(_col_blocks(w, col0, width))
        col0 += width
    return (*groups, _col_blocks(w_lr))


class _Weights(NamedTuple):
    g_mix: Any
    w_in: tuple
    w_gate: Any
    b_gate: Any
    g_gla_out: Any
    sinks: Any
    w_out: Any
    g_ple: Any
    w_ple_gate: Any
    w_ple_proj: Any
    g_final: Any


def _cast_stages(*refs):
    n = len(refs) // 2

    def unit():
        for src_ref, dst_ref in zip(refs[:n], refs[n:]):
            dst_ref[...] = src_ref[...].astype(dst_ref.dtype)
        return
        yield

    return None, [_Stream([unit()], 1)]


def _cast_task(arrays, n_steps):
    slab = lambda a: pl.BlockSpec((a.shape[0] // n_steps, a.shape[1]), lambda b, i: (i, 0))
    return _Task(stages=_cast_stages, operands=tuple(arrays),
                 in_specs=tuple(slab(a) for a in arrays), out_specs=tuple(slab(a) for a in arrays),
                 out_shapes=tuple(jax.ShapeDtypeStruct(a.shape, BF16) for a in arrays))


def _attn_tasks(w, proj, s0, state_batch0, past, chunk, tile):
    q1, k1, v1, z1, q2, kv2, z2, lr = proj
    return [_gla_task(q1, k1, v1, z1, lr, w.w_gate, w.b_gate, w.g_gla_out, s0, state_batch0,
                      chunk, tile),
            _swa_task(w.sinks, q2, z2, kv2, past, chunk, tile)]


def _prompt_path(w, x, p, s0, w_out, w_ple_gate, guest_x, guest):
    n_batch, length, _ = x.shape
    assert n_batch == 2, "a call holding all three stages would not fit the VMEM budget"
    proj, attn = {}, {}
    y = None
    for s in range(n_batch + 2):
        has_in, has_attn, has_out = s < n_batch, 1 <= s <= n_batch, s >= 2
        tile = TILE_FUSED if has_attn else TILE_MATMUL_ONLY
        tasks, names = [], []
        if has_attn:
            tasks += _attn_tasks(w, proj[s - 1], s0, s - 1, None, CHUNK, tile)
            names.append("attn")
        if has_in:
            tasks.append(_inproj_task(x, s, 1, tile, w.g_mix, w.w_in,
                                      guest_x=guest_x if s == 0 else None))
            names.append("inproj")
        if s == 0:
            tasks.append(_cast_task((w_out, w_ple_gate), length // tile))
        if has_out:
            o1, o2, _ = attn[s - 2]
            tasks.append(_outproj_task(x, o1, o2, p, s - 2, tile, w.w_out, w.g_ple,
                                       w.w_ple_gate, w.w_ple_proj, w.g_final, y_prev=y,
                                       guest=guest(w, guest_proj) if s == n_batch + 1 else None))
            names.append("outproj")
        res = _run(tasks, (1, length // tile), "_".join(names))
        if has_attn:
            (o1, s_new), (o2,) = res[0], res[1]
            attn[s - 1] = (o1, o2, s_new)
            res = res[2:]
        if has_in:
            n_proj = len(w.w_in)
            proj[s] = res[0][:n_proj]
            if s == 0:
                guest_proj = res[0][n_proj:]
            res = res[1:]
        if s == 0:
            w = w._replace(w_out=_col_blocks(res[0][0]), w_ple_gate=_col_blocks(res[0][1]))
            res = res[1:]
        if has_out:
            y, y_guest = (res[0] + (None,))[:2]
    states = jnp.concatenate([attn[b][2] for b in range(n_batch)], axis=0)
    kv = [proj[b][5] for b in range(n_batch)]
    return y, states, kv, y_guest


def _sample_attention(w, proj, x, p, s0, past):
    n_batch, length, _ = x.shape
    flat = lambda a: a.reshape(1, n_batch * length, a.shape[-1])
    proj = tuple(a.reshape(n_batch, length, a.shape[-1]) for a in proj)
    (o1, s_new), (o2,) = _run(_attn_tasks(w, proj, s0, 0, past, min(CHUNK, length), length),
                              (n_batch, 1), "attn_sample")
    return (flat(x), flat(o1), flat(o2), flat(p)), s_new, proj[5]


def kernel(x_prompt, x_sample, p_prompt, p_sample, state_gla, cache_swa_k, cache_swa_v, g_mix, w_in, w_gate_up, b_gate, g_gla_out, swa_sinks, w_out, g_ple, w_ple_gate, w_ple_proj, g_final):
    depth = w_in.shape[0]
    assert depth == 1, "the final RMSNorm is fused into the last layer's output kernel"
    bp, lp, _ = x_prompt.shape
    bs, ls, _ = x_sample.shape
    assert cache_swa_k.shape[2] == WINDOW
    i = 0
    row = lambda a: a.reshape(1, -1).astype(F32)
    w = _Weights(
        g_mix=row(g_mix[i]), w_in=_split_w_in(w_in[i]),
        w_gate=jnp.pad(w_gate_up[i].astype(BF16), ((0, LANES - GLA_RANK), (0, 0))),
        b_gate=row(b_gate[i]), g_gla_out=row(g_gla_out[i]),
        sinks=swa_sinks[i].reshape(-1).astype(F32),
        w_out=None, g_ple=row(g_ple[i]), w_ple_gate=None,
        w_ple_proj=_col_blocks(w_ple_proj[i].astype(BF16)), g_final=row(g_final))

    past = (cache_swa_k[i].reshape(bs, WINDOW, SWA_KVW), cache_swa_v[i].reshape(bs, WINDOW, SWA_KVW))
    sample = {}

    def sample_guest(w_now, proj):
        tile, sample["state"], sample["kv"] = _sample_attention(
            w_now, proj, x_sample, p_sample[i], state_gla[i], past)
        return tile

    s0_p = jnp.zeros((bp, GLA_HEADS, GLA_DK, GLA_DV), F32)
    y_p, s_p, kv_p, y_s = _prompt_path(w, x_prompt, p_prompt[i], s0_p, w_out[i], w_ple_gate[i],
                                       x_sample.reshape(1, bs * ls, -1), sample_guest)
    keep = min(WINDOW, lp)
    last = lambda col: jnp.concatenate(
        [a[:, lp - keep:, col * SWA_KVW:(col + 1) * SWA_KVW] for a in kv_p], axis=0
    ).reshape(bp, keep, SWA_KV_HEADS, SWA_HEAD_DIM)
    k_p, v_p = last(0), last(1)

    y_s, s_s, kv_s = y_s.reshape(x_sample.shape), sample["state"], sample["kv"]
    k_s = kv_s[:, :, :SWA_KVW].reshape(bs, ls, SWA_KV_HEADS, SWA_HEAD_DIM)
    v_s = kv_s[:, :, SWA_KVW:].reshape(bs, ls, SWA_KV_HEADS, SWA_HEAD_DIM)
    return (y_p, y_s, s_p[None], k_p[None], v_p[None], s_s[None], k_s[None], v_s[None])
```

```python
import functools
from typing import Any, Callable, NamedTuple

import jax
import jax.numpy as jnp
from jax import lax
from jax.experimental import pallas as pl
from jax.experimental.pallas import tpu as pltpu

F32 = jnp.float32
BF16 = jnp.bfloat16

CHUNK = 64
RMS_EPS = 1e-6
GLA_HEADS = 4
GLA_DK = 128
GLA_DV = 256
GLA_KW = GLA_HEADS * GLA_DK
GLA_WIDTH = GLA_HEADS * GLA_DV
GLA_RANK = 16
GLA_TAU = 16.0
SWA_KV_HEADS = 4
SWA_GROUP = 4
SWA_HEAD_DIM = 64
SWA_WIDTH = SWA_KV_HEADS * SWA_GROUP * SWA_HEAD_DIM
SWA_KVW = SWA_KV_HEADS * SWA_HEAD_DIM
WINDOW = 128

LANES = 128
MXU_WIDTH = 256
WEIGHT_BLOCK = 512
VMEM_LIMIT_BYTES = 56 * 1024 * 1024
TILE_MATMUL_ONLY = 512
TILE_FUSED = 256
OUT_ROW_GROUP = 256
GLA_IN_FLIGHT = 3
SWA_IN_FLIGHT = 4
NORM_HEAD_START = 1

LOG2E = 1.4426950408889634
LN2 = 0.6931471805599453


def _dot(a, b):
    return jnp.dot(a, b, preferred_element_type=F32)


def _rms_scale(x):
    return lax.rsqrt(jnp.mean(x * x, axis=-1, keepdims=True) + RMS_EPS)


def _sigmoid(x):
    return 1.0 / (1.0 + jnp.exp2(x * -LOG2E))


def _silu(x):
    return x * _sigmoid(x)


def _resident(shape):
    zeros = (0,) * len(shape)
    return pl.BlockSpec(shape, lambda *_: zeros, pipeline_mode=pl.Buffered(1))


def _tile(tm, width, batch0=0):
    return pl.BlockSpec((None, tm, width), lambda b, i: (batch0 + b, i, 0))


class _Block(NamedTuple):
    array: Any
    shape: tuple
    index: tuple

    def spec(self):
        index = self.index
        return pl.BlockSpec(self.shape, lambda *_: index, pipeline_mode=pl.Buffered(1))


def _col_blocks(w, col0=0, width=None):
    width = width or w.shape[1] - col0
    blk = min(WEIGHT_BLOCK, width)
    assert col0 % blk == 0 and width % blk == 0
    return tuple(_Block(w, (w.shape[0], blk), (0, (col0 + c) // blk))
                 for c in range(0, width, blk))


def _block_pieces(block_refs):
    out, base = [], 0
    for ref in block_refs:
        width = ref.shape[1]
        for c in range(0, width, MXU_WIDTH):
            hi = min(c + MXU_WIDTH, width)
            out.append((slice(base + c, base + hi), ref, slice(c, hi)))
        base += width
    return out


class _Task(NamedTuple):
    stages: Callable[..., Any]
    operands: tuple
    in_specs: tuple
    out_specs: tuple
    out_shapes: tuple
    scratch: tuple = ()
    aliases: Any = None


_DONE = object()


class _Stream:
    def __init__(self, units, max_active):
        self.pending, self.active, self.max_active = list(units), [], max_active

    def busy(self):
        return bool(self.pending or self.active)

    def step(self):
        self.active = [g for g in self.active if next(g, _DONE) is not _DONE]
        if self.pending and len(self.active) < self.max_active:
            g = self.pending.pop(0)
            if next(g, _DONE) is not _DONE:
                self.active.append(g)


def _drive(heavy, light):
    heavy = [g for g in heavy if g is not None]
    while heavy or any(s.busy() for s in light):
        for s in light:
            s.step()
        heavy = [g for g in heavy if next(g, _DONE) is not _DONE]


def _run(tasks, grid, name):
    n_in = [len(t.operands) for t in tasks]
    n_out = [len(t.out_shapes) for t in tasks]
    n_scr = [len(t.scratch) for t in tasks]

    def body(*refs):
        ins = refs[:sum(n_in)]
        outs = refs[sum(n_in):sum(n_in) + sum(n_out)]
        scrs = refs[sum(n_in) + sum(n_out):]
        heavy, light = [], []
        i = o = s = 0
        for t, ni, no, ns in zip(tasks, n_in, n_out, n_scr):
            h, l = t.stages(*ins[i:i + ni], *outs[o:o + no], *scrs[s:s + ns])
            heavy.append(h)
            light += l
            i, o, s = i + ni, o + no, s + ns
        _drive(heavy, light)

    aliases = {}
    i = o = 0
    for t, ni, no in zip(tasks, n_in, n_out):
        for src, dst in (t.aliases or {}).items():
            aliases[i + src] = o + dst
        i, o = i + ni, o + no

    flat = pl.pallas_call(
        body,
        grid=grid,
        in_specs=[s for t in tasks for s in t.in_specs],
        out_specs=tuple(s for t in tasks for s in t.out_specs),
        out_shape=tuple(s for t in tasks for s in t.out_shapes),
        scratch_shapes=[s for t in tasks for s in t.scratch],
        input_output_aliases=aliases,
        compiler_params=pltpu.CompilerParams(
            dimension_semantics=("arbitrary", "arbitrary"), vmem_limit_bytes=VMEM_LIMIT_BYTES),
        name=name,
    )(*[a for t in tasks for a in t.operands])
    results, o = [], 0
    for no in n_out:
        results.append(tuple(flat[o:o + no]))
        o += no
    return results


def _inproj_stages(x_ref, g_ref, *refs, blocks_per_group, has_guest):
    n_blocks, n_out = sum(blocks_per_group), len(blocks_per_group)
    w_refs, refs, n_scr = refs[:n_blocks], refs[n_blocks:-1], refs[-1]
    guest_x = refs[0] if has_guest else None
    main_outs = refs[1:1 + n_out] if has_guest else refs

    def pieces(x_ref, o_refs):
        x = x_ref[...]
        n_ref = n_scr.at[:x.shape[0]]
        n_ref[...] = ((x * _rms_scale(x)) * g_ref[...]).astype(BF16)
        for _ in range(NORM_HEAD_START):
            yield
        first = 0
        for n_blk, o_ref, is_gate in zip(blocks_per_group, o_refs, IN_IS_GATE):
            for cols, w_ref, cs in _block_pieces(w_refs[first:first + n_blk]):
                u = _dot(n_ref[...], w_ref[:, cs])
                o_ref[:, cols] = (_silu(u) if is_gate else u).astype(o_ref.dtype)
                yield
            first += n_blk

    def all_pieces():
        yield from pieces(x_ref, main_outs)
        if has_guest:
            @pl.when(pl.program_id(1) == pl.num_programs(1) - 1)
            def _():
                for _ in pieces(guest_x, refs[1 + n_out:]):
                    pass

    return all_pieces(), []


IN_DTYPES = (BF16, BF16, BF16, BF16, BF16, F32, BF16, BF16)
IN_IS_GATE = (False, False, False, True, False, False, True, False)


def _inproj_task(x, batch0, n_batch, tm, g_mix, groups, guest_x=None):
    _, length, d = x.shape
    blocks = [w for g in groups for w in g]
    widths = [sum(w.shape[1] for w in g) for g in groups]
    operands = (x, g_mix, *[w.array for w in blocks])
    in_specs = (_tile(tm, d, batch0), _resident(g_mix.shape), *[w.spec() for w in blocks])
    out_specs = [_tile(tm, n) for n in widths]
    out_shapes = [jax.ShapeDtypeStruct((n_batch, length, n), dt) for n, dt in zip(widths, IN_DTYPES)]
    if guest_x is not None:
        n_guest = guest_x.shape[1]
        whole = lambda n: pl.BlockSpec((None, n_guest, n), lambda b, i: (0, 0, 0))
        operands += (guest_x,)
        in_specs += (pl.BlockSpec((None, n_guest, d), lambda b, i: (0, 0, 0),
                                  pipeline_mode=pl.Buffered(1)),)
        out_specs += [whole(n) for n in widths]
        out_shapes += [jax.ShapeDtypeStruct((1, n_guest, n), dt) for n, dt in zip(widths, IN_DTYPES)]
    return _Task(
        stages=functools.partial(_inproj_stages, blocks_per_group=tuple(len(g) for g in groups),
                                 has_guest=guest_x is not None),
        operands=operands, in_specs=in_specs, out_specs=tuple(out_specs),
        out_shapes=tuple(out_shapes),
        scratch=(pltpu.VMEM((max(tm, guest_x.shape[1] if guest_x is not None else 0), d), BF16),))


def _gla_stages(q_ref, k_ref, v_ref, z_ref, lr_ref, wg_ref, bg_ref, gout_ref, s0_ref,
                o_ref, s_ref, *, chunk, n_chunks):
    @pl.when(pl.program_id(1) == 0)
    def _():
        s_ref[...] = s0_ref[...]

    row = lax.broadcasted_iota(jnp.int32, (chunk, chunk), 0)
    col = lax.broadcasted_iota(jnp.int32, (chunk, chunk), 1)
    causal = row >= col
    row2 = lax.broadcasted_iota(jnp.int32, (chunk, 2 * chunk), 0)
    col2 = lax.broadcasted_iota(jnp.int32, (chunk, 2 * chunk), 1)
    tril2 = jnp.where(row2 >= lax.rem(col2, chunk), 1.0, 0.0).astype(BF16)
    eye = (lax.broadcasted_iota(jnp.int32, (GLA_DK, GLA_DK), 0)
           == lax.broadcasted_iota(jnp.int32, (GLA_DK, GLA_DK), 1))
    scale = GLA_DK ** -0.5
    heads = [(slice(h * GLA_DK, (h + 1) * GLA_DK), slice(h * GLA_DV, (h + 1) * GLA_DV))
             for h in range(GLA_HEADS)]

    def chunk_unit(c):
        r = slice(c * chunk, (c + 1) * chunk)
        gpre = _dot(lr_ref[r, :], wg_ref[...])
        yield
        gpre = gpre + bg_ref[...]
        soft = jnp.log2(1.0 + jnp.exp2(jnp.abs(gpre) * -LOG2E))
        g = jnp.minimum(gpre, 0.0) * (1.0 / GLA_TAU) - soft * (LN2 / GLA_TAU)
        g_hi = g.astype(BF16)
        g_lo = (g - g_hi.astype(F32)).astype(BF16)
        g2 = jnp.concatenate([g_hi, g_lo], axis=0)
        yield
        b = _dot(tril2, g2)
        yield
        b_last = b[chunk - 1:chunk, :]
        q = q_ref[r, :].astype(F32)
        k = k_ref[r, :].astype(F32)
        qe = ((q * scale) * jnp.exp(b)).astype(BF16)
        ke = k * jnp.exp(-b)
        kd = k * jnp.exp(b_last - b)
        decay = jnp.exp(b_last)
        v = [v_ref[r, vs] for _, vs in heads]
        ke_t = [ke[:, ks].T.astype(BF16) for ks, _ in heads]
        kd_t = [kd[:, ks].T.astype(BF16) for ks, _ in heads]
        yield
        a_raw = [_dot(qe[:, ks], ke_t[h]) for h, (ks, _) in enumerate(heads)]
        kv = [_dot(kd_t[h], v[h]) for h in range(GLA_HEADS)]
        yield
        lhs, rhs = [], []
        for h, (ks, _) in enumerate(heads):
            a = jnp.where(causal, a_raw[h], 0.0).astype(BF16)
            s = s_ref[h]
            lhs.append(jnp.concatenate([qe[:, ks], a], axis=1))
            rhs.append(jnp.concatenate([s.astype(BF16), v[h]], axis=0))
            dcol = jnp.sum(jnp.where(eye, decay[:, ks], 0.0), axis=1, keepdims=True)
            s_ref[h] = dcol * s + kv[h]
        yield
        o = [_dot(lhs[h], rhs[h]) for h in range(GLA_HEADS)]
        yield
        for h, (_, vs) in enumerate(heads):
            oh = (o[h] * _rms_scale(o[h])) * gout_ref[...]
            o_ref[r, vs] = (oh * z_ref[r, vs].astype(F32)).astype(o_ref.dtype)

    return None, [_Stream([chunk_unit(c) for c in range(n_chunks)], GLA_IN_FLIGHT)]


def _gla_task(q, k, v, z, lr, wg, bg, gout, s0, state_batch0, chunk, tb):
    n_batch, length, _ = q.shape
    st_in = pl.BlockSpec((None, GLA_HEADS, GLA_DK, GLA_DV),
                         lambda b, i: (state_batch0 + b, 0, 0, 0))
    st_out = pl.BlockSpec((None, GLA_HEADS, GLA_DK, GLA_DV), lambda b, i: (b, 0, 0, 0))
    return _Task(
        stages=functools.partial(_gla_stages, chunk=chunk, n_chunks=tb // chunk),
        operands=(q, k, v, z, lr, wg, bg, gout, s0),
        in_specs=(_tile(tb, GLA_KW), _tile(tb, GLA_KW), _tile(tb, GLA_WIDTH), _tile(tb, GLA_WIDTH),
                  _tile(tb, LANES), _resident(wg.shape), _resident(bg.shape),
                  _resident(gout.shape), st_in),
        out_specs=(_tile(tb, GLA_WIDTH), st_out),
        out_shapes=(jax.ShapeDtypeStruct((n_batch, length, GLA_WIDTH), BF16),
                    jax.ShapeDtypeStruct((n_batch, GLA_HEADS, GLA_DK, GLA_DV), F32)))


def _swa_stages(sink_ref, q_ref, kp_ref, vp_ref, kc_ref, vc_ref, z_ref, o_ref,
                *, chunk, n_chunks, mask_missing_past):
    kvw = SWA_GROUP * SWA_HEAD_DIM
    n_keys = WINDOW + chunk
    gq = SWA_GROUP * chunk
    gq_group = lax.broadcasted_iota(jnp.int32, (1, gq), 1) // chunk
    key_i = lax.broadcasted_iota(jnp.int32, (n_keys, gq), 0)
    qscale = SWA_HEAD_DIM ** -0.5
    first_pos = pl.program_id(1) * (chunk * n_chunks) - WINDOW
    shared = {}

    def head_operands(h):
        if not shared:
            shared["k"] = jnp.concatenate([kp_ref[...], kc_ref[...]], axis=0).astype(BF16)
            v = jnp.concatenate([vp_ref[...], vc_ref[...]], axis=0)
            shared["vt"] = v.T.astype(BF16)
        if h not in shared:
            hd = slice(h * SWA_HEAD_DIM, (h + 1) * SWA_HEAD_DIM)
            k_h = shared["k"][:, hd]
            vt = shared["vt"][hd, :]
            vt_at = {}
            for c in range(n_chunks):
                shift = (c * chunk) % LANES
                if shift not in vt_at:
                    vt_at[shift] = vt[:, shift:]
            sink = jnp.zeros((1, gq), F32)
            for g in range(SWA_GROUP):
                sink = jnp.where(gq_group == g, sink_ref[h * SWA_GROUP + g], sink)
            shared[h] = (k_h, vt_at, sink)
        return shared[h]

    def unit(h, c):
        k_h, vt_at, sink = head_operands(h)
        rows = slice(c * chunk, (c + 1) * chunk)
        keys = slice(c * chunk, c * chunk + n_keys)
        cols = slice(h * kvw, (h + 1) * kvw)
        qt = q_ref[rows, cols].astype(F32).T
        qg = jnp.concatenate(
            [qt[g * SWA_HEAD_DIM:(g + 1) * SWA_HEAD_DIM] for g in range(SWA_GROUP)],
            axis=1).astype(BF16)
        shift = (c * chunk) % LANES
        vt_keys = vt_at[shift][:, c * chunk - shift:c * chunk - shift + n_keys]
        yield
        s = _dot(k_h[keys], qg)
        yield
        if mask_missing_past and c * chunk < WINDOW:
            s = jnp.where(key_i + (first_pos + c * chunk) >= 0, s, jnp.finfo(F32).min)
        m = jnp.maximum(jnp.max(s, axis=0, keepdims=True) * qscale, sink) * LOG2E
        p = jnp.exp2(s * (qscale * LOG2E) - m)
        denom = jnp.sum(p, axis=0, keepdims=True) + jnp.exp2(sink * LOG2E - m)
        p = p.astype(BF16)
        yield
        o_t = _dot(vt_keys, p)
        yield
        o_t = (o_t / denom).T
        o = jnp.concatenate([o_t[g * chunk:(g + 1) * chunk] for g in range(SWA_GROUP)],
                            axis=1)
        o_ref[rows, cols] = (o * z_ref[rows, cols].astype(F32)).astype(o_ref.dtype)

    units = [unit(h, c) for h in range(SWA_KV_HEADS) for c in range(n_chunks)]
    return None, [_Stream(units, SWA_IN_FLIGHT)]


def _swa_task(sinks, q, z, kv, past, chunk, qb):
    n_batch, length, _ = q.shape
    if past is None:
        per_win = qb // WINDOW
        prev = lambda col: pl.BlockSpec(
            (None, WINDOW, SWA_KVW), lambda b, i: (b, jnp.maximum(i * per_win - 1, 0), col))
        k_past, v_past, kp_spec, vp_spec = kv, kv, prev(0), prev(1)
    else:
        k_past, v_past = past
        kp_spec = vp_spec = pl.BlockSpec((None, WINDOW, SWA_KVW), lambda b, i: (b, 0, 0))
    cur = lambda col: pl.BlockSpec((None, qb, SWA_KVW), lambda b, i: (b, i, col))
    return _Task(
        stages=functools.partial(_swa_stages, chunk=chunk, n_chunks=qb // chunk,
                                 mask_missing_past=past is None),
        operands=(sinks, q, k_past, v_past, kv, kv, z),
        in_specs=(pl.BlockSpec(memory_space=pltpu.SMEM), _tile(qb, SWA_WIDTH), kp_spec, vp_spec,
                  cur(0), cur(1), _tile(qb, SWA_WIDTH)),
        out_specs=(_tile(qb, SWA_WIDTH),),
        out_shapes=(jax.ShapeDtypeStruct((n_batch, length, SWA_WIDTH), BF16),))


def _outproj_stages(x_ref, o1_ref, o2_ref, p_ref, gple_ref, gfin_ref, *refs, n_blocks,
                    own_batch, has_guest):
    wo, wg, wp = (refs[j * n_blocks:(j + 1) * n_blocks] for j in range(3))
    guest = refs[3 * n_blocks:3 * n_blocks + 4] if has_guest else None
    y_ref = refs[-2] if has_guest else refs[-1]
    if own_batch is not None:
        for b in range(y_ref.shape[0]):
            if b != own_batch:
                y_ref[b] = jnp.zeros(y_ref.shape[1:], y_ref.dtype)
        y_ref = y_ref.at[own_batch]

    def pieces(x_ref, o1_ref, o2_ref, p_ref, y_ref):
        h_ref = y_ref
        o = jnp.concatenate([o1_ref[...], o2_ref[...]], axis=1)
        n = []
        for cols, w_ref, cs in _block_pieces(wo):
            h = x_ref[:, cols] + _dot(o, w_ref[:, cs])
            h_ref[:, cols] = h
            n.append((h * gple_ref[:, cols]).astype(BF16))
            yield
        n = jnp.concatenate(n, axis=1)
        r = _rms_scale(h_ref[...]) * -LOG2E
        p = p_ref[...].astype(BF16)
        tm = h_ref.shape[0]
        for rows in [slice(r0, r0 + OUT_ROW_GROUP) for r0 in range(0, tm, OUT_ROW_GROUP)]:
            for (cols, wg_ref, cs), (_, wp_ref, _) in zip(_block_pieces(wg), _block_pieces(wp)):
                gate = 1.0 / (1.0 + jnp.exp2(_dot(n[rows], wg_ref[:, cs]) * r[rows]))
                h_ref[rows, cols] = h_ref[rows, cols] + gate * _dot(p[rows], wp_ref[:, cs])
                yield
            h = h_ref[rows, :]
            y_ref[rows, :] = (h * _rms_scale(h)) * gfin_ref[...]

    def all_pieces():
        yield from pieces(x_ref, o1_ref, o2_ref, p_ref, y_ref)
        if has_guest:
            @pl.when(pl.program_id(1) == pl.num_programs(1) - 1)
            def _():
                for _ in pieces(*guest, refs[-1]):
                    pass

    return all_pieces(), []


def _outproj_task(x, o1, o2, p, batch0, tm, wo, gple, wg, wp, gfin, y_prev=None, guest=None):
    _, _, d = x.shape
    assert len(wo) == len(wg) == len(wp)
    blocks = (*wo, *wg, *wp)
    operands = (x, o1, o2, p, gple, gfin, *[w.array for w in blocks])
    in_specs = (_tile(tm, d, batch0), _tile(tm, o1.shape[2]), _tile(tm, o2.shape[2]),
                _tile(tm, p.shape[2], batch0), _resident(gple.shape), _resident(gfin.shape),
                *[w.spec() for w in blocks])
    out_specs, out_shapes = [], []
    if guest is not None:
        whole = lambda a: pl.BlockSpec((None,) + a.shape[1:], lambda b, i: (0, 0, 0))
        operands += tuple(guest)
        in_specs += tuple(whole(a) for a in guest)
        out_specs.append(whole(guest[0]))
        out_shapes.append(jax.ShapeDtypeStruct(guest[0].shape, F32))
    aliases, own_batch, y_spec = None, None, _tile(tm, d, batch0)
    if y_prev is not None:
        aliases = {len(operands): 0}
        operands += (y_prev,)
        in_specs += (pl.BlockSpec(memory_space=pl.ANY),)
    elif x.shape[0] > 1:
        own_batch = batch0
        y_spec = pl.BlockSpec((x.shape[0], tm, d), lambda b, i: (0, i, 0))
    return _Task(stages=functools.partial(_outproj_stages, n_blocks=len(wo), own_batch=own_batch,
                                          has_guest=guest is not None),
                 operands=operands, in_specs=in_specs,
                 out_specs=(y_spec, *out_specs),
                 out_shapes=(jax.ShapeDtypeStruct(x.shape, F32), *out_shapes),
                 aliases=aliases)


def _split_w_in(w_in):
    lr0 = 2 * GLA_KW + 2 * GLA_WIDTH
    n_rows, n_in = w_in.shape
    n_main = n_in - GLA_RANK
    assert lr0 % WEIGHT_BLOCK == 0 and n_main % WEIGHT_BLOCK == 0

    def relayout(w_ref, o_ref):
        o_ref[...] = w_ref[...].T.astype(BF16)

    assert WEIGHT_BLOCK % GLA_RANK == 0
    src_col = lambda j: GLA_RANK * (j * (WEIGHT_BLOCK // GLA_RANK)
                                    + (j >= lr0 // WEIGHT_BLOCK).astype(jnp.int32))
    w = pl.pallas_call(
        relayout,
        grid=(n_main // WEIGHT_BLOCK,),
        in_specs=[pl.BlockSpec((pl.Element(WEIGHT_BLOCK), pl.Element(n_rows)),
                               lambda j: (src_col(j), 0))],
        out_specs=pl.BlockSpec((n_rows, WEIGHT_BLOCK), lambda j: (0, j)),
        out_shape=jax.ShapeDtypeStruct((n_rows, n_main), BF16),
        compiler_params=pltpu.CompilerParams(
            dimension_semantics=("arbitrary",), vmem_limit_bytes=VMEM_LIMIT_BYTES),
        name="w_in_relayout",
    )(w_in.T)

    def relayout_gate(w_ref, o_ref):
        pad = jnp.zeros((LANES - GLA_RANK, n_rows), F32)
        o_ref[...] = jnp.concatenate([w_ref[...], pad], axis=0).T.astype(BF16)

    w_lr = pl.pallas_call(
        relayout_gate,
        grid=(1,),
        in_specs=[pl.BlockSpec((pl.Element(GLA_RANK), pl.Element(n_rows)), lambda j: (lr0, 0))],
        out_specs=pl.BlockSpec((n_rows, LANES), lambda j: (0, 0)),
        out_shape=jax.ShapeDtypeStruct((n_rows, LANES), BF16),
        name="w_gate_relayout",
    )(w_in.T)
    widths = (GLA_KW, GLA_KW, GLA_WIDTH, GLA_WIDTH, SWA_WIDTH, 2 * SWA_KVW, SWA_WIDTH)
    groups, col0 = [], 0
    for width in widths:
        groups.append(_col_blocks(w, col0, width))
        col0 += width
    return (*groups, _col_blocks(w_lr))


class _Weights(NamedTuple):
    g_mix: Any
    w_in: tuple
    w_gate: Any
    b_gate: Any
    g_gla_out: Any
    sinks: Any
    w_out: Any
    g_ple: Any
    w_ple_gate: Any
    w_ple_proj: Any
    g_final: Any


def _cast_stages(*refs):
    n = len(refs) // 2

    def unit():
        for src_ref, dst_ref in zip(refs[:n], refs[n:]):
            dst_ref[...] = src_ref[...].astype(dst_ref.dtype)
        return
        yield

    return None, [_Stream([unit()], 1)]


def _cast_task(arrays, n_steps):
    slab = lambda a: pl.BlockSpec((a.shape[0] // n_steps, a.shape[1]), lambda b, i: (i, 0))
    return _Task(stages=_cast_stages, operands=tuple(arrays),
                 in_specs=tuple(slab(a) for a in arrays), out_specs=tuple(slab(a) for a in arrays),
                 out_shapes=tuple(jax.ShapeDtypeStruct(a.shape, BF16) for a in arrays))


def _attn_tasks(w, proj, s0, state_batch0, past, chunk, tile):
    q1, k1, v1, z1, q2, kv2, z2, lr = proj
    return [_gla_task(q1, k1, v1, z1, lr, w.w_gate, w.b_gate, w.g_gla_out, s0, state_batch0,
                      chunk, tile),
            _swa_task(w.sinks, q2, z2, kv2, past, chunk, tile)]


def _prompt_path(w, x, p, s0, w_out, w_ple_gate, guest_x, guest):
    n_batch, length, _ = x.shape
    assert n_batch == 2, "a call holding all three stages would not fit the VMEM budget"
    proj, attn = {}, {}
    y = None
    for s in range(n_batch + 2):
        has_in, has_attn, has_out = s < n_batch, 1 <= s <= n_batch, s >= 2
        tile = TILE_FUSED if has_attn else TILE_MATMUL_ONLY
        tasks, names = [], []
        if has_attn:
            tasks += _attn_tasks(w, proj[s - 1], s0, s - 1, None, CHUNK, tile)
            names.append("attn")
        if has_in:
            tasks.append(_inproj_task(x, s, 1, tile, w.g_mix, w.w_in,
                                      guest_x=guest_x if s == 0 else None))
            names.append("inproj")
        if s == 0:
            tasks.append(_cast_task((w_out, w_ple_gate), length // tile))
        if has_out:
            o1, o2, _ = attn[s - 2]
            tasks.append(_outproj_task(x, o1, o2, p, s - 2, tile, w.w_out, w.g_ple,
                                       w.w_ple_gate, w.w_ple_proj, w.g_final, y_prev=y,
                                       guest=guest(w, guest_proj) if s == n_batch + 1 else None))
            names.append("outproj")
        res = _run(tasks, (1, length // tile), "_".join(names))
        if has_attn:
            (o1, s_new), (o2,) = res[0], res[1]
            attn[s - 1] = (o1, o2, s_new)
            res = res[2:]
        if has_in:
            n_proj = len(w.w_in)
            proj[s] = res[0][:n_proj]
            if s == 0:
                guest_proj = res[0][n_proj:]
            res = res[1:]
        if s == 0:
            w = w._replace(w_out=_col_blocks(res[0][0]), w_ple_gate=_col_blocks(res[0][1]))
            res = res[1:]
        if has_out:
            y, y_guest = (res[0] + (None,))[:2]
    states = jnp.concatenate([attn[b][2] for b in range(n_batch)], axis=0)
    kv = [proj[b][5] for b in range(n_batch)]
    return y, states, kv, y_guest


def _sample_attention(w, proj, x, p, s0, past):
    n_batch, length, _ = x.shape
    flat = lambda a: a.reshape(1, n_batch * length, a.shape[-1])
    proj = tuple(a.reshape(n_batch, length, a.shape[-1]) for a in proj)
    (o1, s_new), (o2,) = _run(_attn_tasks(w, proj, s0, 0, past, min(CHUNK, length), length),
                              (n_batch, 1), "attn_sample")
    return (flat(x), flat(o1), flat(o2), flat(p)), s_new, proj[5]


def kernel(x_prompt, x_sample, p_prompt, p_sample, state_gla, cache_swa_k, cache_swa_v, g_mix, w_in, w_gate_up, b_gate, g_gla_out, swa_sinks, w_out, g_ple, w_ple_gate, w_ple_proj, g_final):
    depth = w_in.shape[0]
    assert depth == 1, "the final RMSNorm is fused into the last layer's output kernel"
    bp, lp, _ = x_prompt.shape
    bs, ls, _ = x_sample.shape
    assert cache_swa_k.shape[2] == WINDOW
    i = 0
    row = lambda a: a.reshape(1, -1).astype(F32)
    w = _Weights(
        g_mix=row(g_mix[i]), w_in=_split_w_in(w_in[i]),
        w_gate=jnp.pad(w_gate_up[i].astype(BF16), ((0, LANES - GLA_RANK), (0, 0))),
        b_gate=row(b_gate[i]), g_gla_out=row(g_gla_out[i]),
        sinks=swa_sinks[i].reshape(-1).astype(F32),
        w_out=None, g_ple=row(g_ple[i]), w_ple_gate=None,
        w_ple_proj=_col_blocks(w_ple_proj[i].astype(BF16)), g_final=row(g_final))

    past = (cache_swa_k[i].reshape(bs, WINDOW, SWA_KVW), cache_swa_v[i].reshape(bs, WINDOW, SWA_KVW))
    sample = {}

    def sample_guest(w_now, proj):
        tile, sample["state"], sample["kv"] = _sample_attention(
            w_now, proj, x_sample, p_sample[i], state_gla[i], past)
        return tile

    s0_p = jnp.zeros((bp, GLA_HEADS, GLA_DK, GLA_DV), F32)
    y_p, s_p, kv_p, y_s = _prompt_path(w, x_prompt, p_prompt[i], s0_p, w_out[i], w_ple_gate[i],
                                       x_sample.reshape(1, bs * ls, -1), sample_guest)
    keep = min(WINDOW, lp)
    last = lambda col: jnp.concatenate(
        [a[:, lp - keep:, col * SWA_KVW:(col + 1) * SWA_KVW] for a in kv_p], axis=0
    ).reshape(bp, keep, SWA_KV_HEADS, SWA_HEAD_DIM)
    k_p, v_p = last(0), last(1)

    y_s, s_s, kv_s = y_s.reshape(x_sample.shape), sample["state"], sample["kv"]
    k_s = kv_s[:, :, :SWA_KVW].reshape(bs, ls, SWA_KV_HEADS, SWA_HEAD_DIM)
    v_s = kv_s[:, :, SWA_KVW:].reshape(bs, ls, SWA_KV_HEADS, SWA_HEAD_DIM)
    return (y_p, y_s, s_p[None], k_p[None], v_p[None], s_s[None], k_s[None], v_s[None])
```
